```python
import math
import jax, jax.numpy as jnp
from jax import lax
import numpy as np

D_MODEL = 2048
BATCH = 4
SEQ = 2048
DEPTH = 4

MEM_LEN = 256
A_HEAD_DIM = 128
A_HEADS = D_MODEL // 256
A_KV_HEADS = A_HEADS // 4
A_GROUP = A_HEADS // A_KV_HEADS
WINDOW = 128
BLOCK = 128
N_BUCKETS = 32
MAX_DISTANCE = 128
B_HEADS = D_MODEL // 512
B_KEY_DIM = 64
B_VAL_DIM = 128
GATE_RANK = 16
GATE_TAU = 16.0
GLA_CHUNK = 16
C_WIDTH = D_MODEL // 4
C_BLOCKS = 4
C_BLOCK_DIM = C_WIDTH // C_BLOCKS
CONV_WIDTH = 4
CONV_LEFT = 2
LRU_C = 8.0
X_HEADS = 4
X_HEAD_DIM = D_MODEL // X_HEADS
D_FF = 4 * D_MODEL
EPS = 1e-6
NEG_INF = -1e30

A_Q = A_HEADS * A_HEAD_DIM
A_KV = A_KV_HEADS * A_HEAD_DIM
B_QK = B_HEADS * B_KEY_DIM
B_V = B_HEADS * B_VAL_DIM
SPLIT_SIZES = (A_Q, A_KV, A_KV, B_QK, B_QK, B_V, B_V, GATE_RANK, GATE_RANK, C_WIDTH, C_WIDTH)
D_IN = sum(SPLIT_SIZES)
D_MIX = A_Q + B_V + C_WIDTH

kernel_name = 'hymba_style_parallel_hybrid_encoder'


def rmsnorm(x, g):
    xf = x.astype(jnp.float32)
    y = xf * lax.rsqrt(jnp.mean(jnp.square(xf), axis=-1, keepdims=True) + EPS)
    return (y * g.astype(jnp.float32)).astype(x.dtype)


def t5_bucket(rel):
    nb = N_BUCKETS // 2
    max_exact = nb // 2
    ret = jnp.where(rel > 0, nb, 0)
    n = jnp.abs(rel)
    nf = jnp.maximum(n, 1).astype(jnp.float32)
    large = max_exact + (jnp.log(nf / max_exact) / math.log(MAX_DISTANCE / max_exact)
                         * (nb - max_exact)).astype(jnp.int32)
    large = jnp.minimum(large, nb - 1)
    return ret + jnp.where(n < max_exact, n, large)


def windowed_gqa(q, k, v, rel_table, sink):
    bsz, seq = q.shape[:2]
    nblk = seq // BLOCK
    qb = q.reshape(bsz, nblk, BLOCK, A_KV_HEADS, A_GROUP, A_HEAD_DIM)

    def band(t):
        tp = jnp.pad(t, ((0, 0), (BLOCK, BLOCK), (0, 0), (0, 0)))
        tp = tp.reshape(bsz, nblk + 2, BLOCK, A_KV_HEADS, A_HEAD_DIM)
        return jnp.concatenate([tp[:, :-2], tp[:, 1:-1], tp[:, 2:]], axis=2)

    kb, vb = band(k), band(v)
    s = jnp.einsum('bnqhgd,bnkhd->bnhgqk', qb, kb).astype(jnp.float32) * (A_HEAD_DIM ** -0.5)
    qi = jnp.arange(BLOCK)[:, None]
    kj = jnp.arange(3 * BLOCK)[None, :]
    rel = kj - BLOCK - qi
    bias = rel_table.astype(jnp.float32)[t5_bucket(rel)]
    bias = jnp.transpose(bias, (2, 0, 1)).reshape(A_KV_HEADS, A_GROUP, BLOCK, 3 * BLOCK)
    kpos = jnp.arange(nblk)[:, None] * BLOCK + jnp.arange(3 * BLOCK)[None, :] - BLOCK
    valid = (jnp.abs(rel) <= WINDOW)[None] & ((kpos >= 0) & (kpos < seq))[:, None, :]
    s = jnp.where(valid[None, :, None, None], s + bias, NEG_INF)
    sink_col = jnp.broadcast_to(sink.astype(jnp.float32).reshape(A_KV_HEADS, A_GROUP, 1, 1),
                                s.shape[:-1] + (1,))
    p = jax.nn.softmax(jnp.concatenate([s, sink_col], axis=-1), axis=-1)[..., :-1]
    o = jnp.einsum('bnhgqk,bnkhd->bnqhgd', p.astype(v.dtype), vb)
    return o.reshape(bsz, seq, A_Q)


def gla_chunked(q, k, v, log_a, inclusive):
    bsz, nh, seq, dk = q.shape
    dv = v.shape[-1]
    nc = seq // GLA_CHUNK
    q = q.reshape(bsz, nh, nc, GLA_CHUNK, dk)
    k = k.reshape(bsz, nh, nc, GLA_CHUNK, dk)
    log_a = log_a.reshape(bsz, nh, nc, GLA_CHUNK, dk)
    v = v.reshape(bsz, nh, nc, GLA_CHUNK, dv)
    b = jnp.cumsum(log_a, axis=3)
    idx = jnp.arange(GLA_CHUNK)
    mask = (idx[:, None] >= idx[None, :]) if inclusive else (idx[:, None] > idx[None, :])
    m3 = mask[:, :, None]
    diff = b[:, :, :, :, None, :] - b[:, :, :, None, :, :]
    decay = jnp.where(m3, jnp.exp(jnp.where(m3, diff, 0.0)), 0.0)
    attn = jnp.einsum('bhnid,bhnjd,bhnijd->bhnij', q, k, decay)
    o = jnp.einsum('bhnij,bhnjd->bhnid', attn, v)
    b_last = b[:, :, :, -1, :]
    u = jnp.einsum('bhncd,bhnce->bhnde', k * jnp.exp(b_last[:, :, :, None, :] - b), v)

    def step(state, xs):
        dec, un = xs
        return dec[..., None] * state + un, state

    s0 = jnp.zeros((bsz, nh, dk, dv), jnp.float32)
    _, s_prev = lax.scan(step, s0, (jnp.moveaxis(jnp.exp(b_last), 2, 0), jnp.moveaxis(u, 2, 0)))
    s_prev = jnp.moveaxis(s_prev, 0, 2)
    o = o + jnp.einsum('bhncd,bhnde->bhnce', q * jnp.exp(b), s_prev)
    return o.reshape(bsz, nh, seq, dv)


def gla_mixer(q, k, v, g, zf, zb, w2f, b2f, w2b, b2b, gn):
    bsz, seq = q.shape[:2]
    f32 = jnp.float32

    def heads(t, d):
        return t.astype(f32).reshape(bsz, seq, B_HEADS, d).transpose(0, 2, 1, 3)

    qh = heads(q, B_KEY_DIM) * (B_KEY_DIM ** -0.5)
    kh = heads(k, B_KEY_DIM)
    vh = heads(v, B_VAL_DIM)
    la_f = heads(jax.nn.log_sigmoid(zf.astype(f32) @ w2f.astype(f32) + b2f.astype(f32)) / GATE_TAU, B_KEY_DIM)
    la_b = heads(jax.nn.log_sigmoid(zb.astype(f32) @ w2b.astype(f32) + b2b.astype(f32)) / GATE_TAU, B_KEY_DIM)
    o_f = gla_chunked(qh, kh, vh, la_f, True)
    flip = lambda t: jnp.flip(t, axis=2)
    o_b = flip(gla_chunked(flip(qh), flip(kh), flip(vh), flip(la_b), False))
    o = o_f + o_b
    o = o * lax.rsqrt(jnp.mean(jnp.square(o), axis=-1, keepdims=True) + EPS)
    o = o.transpose(0, 2, 1, 3).reshape(bsz, seq, B_V) * gn.astype(f32)
    return (o * jax.nn.silu(g.astype(f32))).astype(q.dtype)


def linear_scan(a, u, reverse):
    def combine(l, r):
        return (l[0] * r[0], r[0] * l[1] + r[1])
    _, h = lax.associative_scan(combine, (a, u), reverse=reverse, axis=1)
    return h


def rglru_mixer(xc, y, conv_w, conv_b, w_a, b_a, w_x, b_x, lam):
    bsz, seq, _ = xc.shape
    f32 = jnp.float32
    xp = jnp.pad(xc, ((0, 0), (CONV_LEFT, CONV_WIDTH - 1 - CONV_LEFT), (0, 0)))
    xconv = sum(xp[:, j:j + seq] * conv_w[j] for j in range(CONV_WIDTH)) + conv_b
    xf = xconv.astype(f32)
    xblk = xf.reshape(bsz, seq, C_BLOCKS, C_BLOCK_DIM)
    r = jax.nn.sigmoid(jnp.einsum('btgi,sgij->sbtgj', xblk, w_a.astype(f32)).reshape(2, bsz, seq, C_WIDTH)
                       + b_a.astype(f32)[:, None, None])
    i = jax.nn.sigmoid(jnp.einsum('btgi,sgij->sbtgj', xblk, w_x.astype(f32)).reshape(2, bsz, seq, C_WIDTH)
                       + b_x.astype(f32)[:, None, None])
    log_a = -LRU_C * r * jax.nn.softplus(-lam.astype(f32))[:, None, None]
    a = jnp.exp(log_a)
    u = jnp.sqrt(-jnp.expm1(2.0 * log_a)) * (i * xf[None])
    h = linear_scan(a[0], u[0], False) + linear_scan(a[1], u[1], True)
    return (h * jax.nn.gelu(y.astype(f32))).astype(xc.dtype)


def cross_attention(xn, memn, wq, wk, wv, wo):
    bsz, seq, _ = xn.shape
    mlen = memn.shape[1]
    q = (xn @ wq).reshape(bsz, seq, X_HEADS, X_HEAD_DIM)
    k = (memn @ wk).reshape(bsz, mlen, X_HEADS, X_HEAD_DIM)
    v = (memn @ wv).reshape(bsz, mlen, X_HEADS, X_HEAD_DIM)
    s = jnp.einsum('bthd,bmhd->bhtm', q, k).astype(jnp.float32) * (X_HEAD_DIM ** -0.5)
    p = jax.nn.softmax(s, axis=-1)
    o = jnp.einsum('bhtm,bmhd->bthd', p.astype(v.dtype), v).reshape(bsz, seq, D_MODEL)
    return o @ wo


def setup_inputs(seed: int = 0) -> dict:
    key = jax.random.key(seed)
    ks = jax.random.split(key, 32)
    f32 = jnp.float32
    L = DEPTH

    def nrm(k, shape, scale):
        return jax.random.normal(k, shape, f32) * scale

    def gain(k, shape):
        return 1.0 + 0.05 * jax.random.normal(k, shape, f32)

    a_init = jax.random.uniform(ks[17], (L, 2, C_WIDTH), f32, 0.9, 0.999) ** (1.0 / LRU_C)
    lru_lambda = jnp.log(a_init) - jnp.log1p(-a_init)
    return {
        'x': nrm(ks[0], (BATCH, SEQ, D_MODEL), 1.0),
        'mem': nrm(ks[1], (BATCH, MEM_LEN, D_MODEL), 1.0),
        'rel_bias': nrm(ks[2], (N_BUCKETS, A_HEADS), 0.5),
        'w_in': nrm(ks[3], (L, D_MODEL, D_IN), D_MODEL ** -0.5),
        'w_out': nrm(ks[4], (L, D_MIX, D_MODEL), D_MIX ** -0.5),
        'attn_sink': nrm(ks[5], (L, A_HEADS), 0.5),
        'gla_w2_f': nrm(ks[6], (L, GATE_RANK, B_QK), GATE_RANK ** -0.5),
        'gla_b2_f': nrm(ks[7], (L, B_QK), 0.1),
        'gla_w2_b': nrm(ks[8], (L, GATE_RANK, B_QK), GATE_RANK ** -0.5),
        'gla_b2_b': nrm(ks[9], (L, B_QK), 0.1),
        'gla_norm': gain(ks[10], (L, B_V)),
        'conv_w': nrm(ks[11], (L, CONV_WIDTH, C_WIDTH), CONV_WIDTH ** -0.5),
        'conv_b': nrm(ks[12], (L, C_WIDTH), 0.02),
        'lru_wa': nrm(ks[13], (L, 2, C_BLOCKS, C_BLOCK_DIM, C_BLOCK_DIM), C_BLOCK_DIM ** -0.5),
        'lru_ba': nrm(ks[14], (L, 2, C_WIDTH), 0.1),
        'lru_wx': nrm(ks[15], (L, 2, C_BLOCKS, C_BLOCK_DIM, C_BLOCK_DIM), C_BLOCK_DIM ** -0.5),
        'lru_bx': nrm(ks[16], (L, 2, C_WIDTH), 0.1),
        'lru_lambda': lru_lambda,
        'xq': nrm(ks[18], (L, D_MODEL, D_MODEL), D_MODEL ** -0.5),
        'xk': nrm(ks[19], (L, D_MODEL, D_MODEL), D_MODEL ** -0.5),
        'xv': nrm(ks[20], (L, D_MODEL, D_MODEL), D_MODEL ** -0.5),
        'xo': nrm(ks[21], (L, D_MODEL, D_MODEL), D_MODEL ** -0.5),
        'w_up': nrm(ks[22], (L, D_MODEL, D_FF), D_MODEL ** -0.5),
        'w_down': nrm(ks[23], (L, D_FF, D_MODEL), D_FF ** -0.5),
        'norm_mix_pre': gain(ks[24], (L, D_MODEL)),
        'norm_mix_post': gain(ks[25], (L, D_MODEL)),
        'norm_mem': gain(ks[26], (L, D_MODEL)),
        'norm_x_pre': gain(ks[27], (L, D_MODEL)),
        'norm_x_post': gain(ks[28], (L, D_MODEL)),
        'norm_ff_pre': gain(ks[29], (L, D_MODEL)),
        'norm_ff_post': gain(ks[30], (L, D_MODEL)),
    }


def reference(x, mem, rel_bias, w_in, w_out, attn_sink, gla_w2_f, gla_b2_f, gla_w2_b, gla_b2_b,
              gla_norm, conv_w, conv_b, lru_wa, lru_ba, lru_wx, lru_bx, lru_lambda,
              xq, xk, xv, xo, w_up, w_down, norm_mix_pre, norm_mix_post, norm_mem,
              norm_x_pre, norm_x_post, norm_ff_pre, norm_ff_post):
    bsz, seq, _ = x.shape
    offsets = np.cumsum(SPLIT_SIZES)[:-1].tolist()
    for l in range(DEPTH):
        h = rmsnorm(x, norm_mix_pre[l])
        aq, ak, av, bq, bk, bv, bg, zf, zb, cx, cy = jnp.split(h @ w_in[l], offsets, axis=-1)
        oa = windowed_gqa(aq.reshape(bsz, seq, A_HEADS, A_HEAD_DIM),
                          ak.reshape(bsz, seq, A_KV_HEADS, A_HEAD_DIM),
                          av.reshape(bsz, seq, A_KV_HEADS, A_HEAD_DIM),
                          rel_bias, attn_sink[l])
        ob = gla_mixer(bq, bk, bv, bg, zf, zb, gla_w2_f[l], gla_b2_f[l], gla_w2_b[l], gla_b2_b[l], gla_norm[l])
        oc = rglru_mixer(cx, cy, conv_w[l], conv_b[l], lru_wa[l], lru_ba[l], lru_wx[l], lru_bx[l], lru_lambda[l])
        mixed = jnp.concatenate([oa, ob.astype(oa.dtype), oc.astype(oa.dtype)], axis=-1) @ w_out[l]
        x = x + rmsnorm(mixed, norm_mix_post[l])
        h = rmsnorm(x, norm_x_pre[l])
        memn = rmsnorm(mem, norm_mem[l])
        x = x + rmsnorm(cross_attention(h, memn, xq[l], xk[l], xv[l], xo[l]), norm_x_post[l])
        h = rmsnorm(x, norm_ff_pre[l])
        ff = jnp.square(jax.nn.relu(h @ w_up[l])) @ w_down[l]
        x = x + rmsnorm(ff, norm_ff_post[l])
    return x
```

```python
import functools
import math

import numpy as np
import jax
import jax.numpy as jnp
from jax import lax
from jax.experimental import pallas as pl
from jax.experimental.pallas import tpu as pltpu

F32 = jnp.float32
BF16 = jnp.bfloat16

D_MODEL = 2048
DEPTH = 4
MEM_LEN = 256
A_HEAD_DIM = 128
A_HEADS = 8
A_KV_HEADS = 2
A_GROUP = 4
WINDOW = 128
BLOCK = 128
N_BUCKETS = 32
MAX_DISTANCE = 128
B_HEADS = 4
B_KEY_DIM = 64
B_VAL_DIM = 128
GATE_RANK = 16
GATE_TAU = 16.0
C_WIDTH = 512
C_BLOCKS = 4
C_BLOCK_DIM = 128
CONV_WIDTH = 4
LRU_C = 8.0
X_HEADS = 4
X_HEAD_DIM = 512
D_FF = 8192
EPS = 1e-6
NEG_INF = -1e30

A_Q = A_HEADS * A_HEAD_DIM
A_KV = A_KV_HEADS * A_HEAD_DIM
B_QK = B_HEADS * B_KEY_DIM
B_V = B_HEADS * B_VAL_DIM

LANE = 128
COL_AQ = 0
COL_AK = COL_AQ + A_Q
COL_AV = COL_AK + A_KV
COL_BQ = COL_AV + A_KV
COL_BK = COL_BQ + B_QK
COL_BV = COL_BK + B_QK
COL_BG = COL_BV + B_V
COL_CX = COL_BG + B_V
COL_CY = COL_CX + C_WIDTH
COL_Z = COL_CY + C_WIDTH
D_PROJ = COL_Z + LANE

GLA_BLOCK = 128
GLA_LEVELS = (64, 32, 16, 8, 4, 2, 1)
VMEM_LIMIT_MIB = 56


def _cparams(semantics, vmem_mib=VMEM_LIMIT_MIB):
    return pltpu.CompilerParams(dimension_semantics=semantics,
                                vmem_limit_bytes=vmem_mib * 1024 * 1024)


def _dot(a, b):
    return jnp.dot(a, b, preferred_element_type=F32)


def _dot_nt(a, b):
    return lax.dot_general(a, b, (((1,), (1,)), ((), ())), preferred_element_type=F32)


def _dot_tn(a, b):
    return lax.dot_general(a, b, (((0,), (0,)), ((), ())), preferred_element_type=F32)


def _softplus(x):
    return jnp.maximum(x, 0.0) + jnp.log1p(jnp.exp(-jnp.abs(x)))


def _rms_scale(x):
    return lax.rsqrt(jnp.mean(jnp.square(x), axis=-1, keepdims=True) + EPS)


def _norm_matmul_kernel(x_ref, g_ref, w_ref, o_ref, hn_ref):
    @pl.when(pl.program_id(1) == 0)
    def _():
        x = x_ref[...]
        hn_ref[...] = (x * _rms_scale(x) * g_ref[...]).astype(BF16)

    o_ref[...] = _dot(hn_ref[...], w_ref[...]).astype(o_ref.dtype)


def _norm_matmul(x, g, w, out_dtype, tm, tn):
    m, k = x.shape
    n = w.shape[1]
    return pl.pallas_call(
        _norm_matmul_kernel,
        grid=(m // tm, n // tn),
        in_specs=[pl.BlockSpec((tm, k), lambda i, j: (i, 0)),
                  pl.BlockSpec((1, k), lambda i, j: (0, 0)),
                  pl.BlockSpec((k, tn), lambda i, j: (0, j))],
        out_specs=pl.BlockSpec((tm, tn), lambda i, j: (i, j)),
        out_shape=jax.ShapeDtypeStruct((m, n), out_dtype),
        scratch_shapes=[pltpu.VMEM((tm, k), BF16)],
        compiler_params=_cparams(("parallel", "arbitrary")),
        name="norm_matmul",
    )(x, g.reshape(1, k), w)


def _attn_kernel(sink_ref, q_ref, kp_ref, kc_ref, kn_ref, vp_ref, vc_ref, vn_ref, bias_ref, o_ref):
    g = pl.program_id(1)
    n = pl.program_id(2)
    nblk = pl.num_programs(2)
    k = jnp.concatenate([kp_ref[...], kc_ref[...], kn_ref[...]], axis=0).astype(BF16)
    v = jnp.concatenate([vp_ref[...], vc_ref[...], vn_ref[...]], axis=0).astype(BF16)
    q = jnp.concatenate([q_ref[:, h * A_HEAD_DIM:(h + 1) * A_HEAD_DIM] for h in range(A_GROUP)],
                        axis=0).astype(BF16)
    s = _dot_nt(q, k) * (A_HEAD_DIM ** -0.5)
    s = s + bias_ref[...].reshape(A_GROUP * BLOCK, 3 * BLOCK)
    col = lax.broadcasted_iota(jnp.int32, (1, 3 * BLOCK), 1)
    in_range = jnp.logical_and(jnp.logical_or(col >= BLOCK, n > 0),
                               jnp.logical_or(col < 2 * BLOCK, n < nblk - 1))
    s = jnp.where(in_range, s, NEG_INF)
    sink = jnp.concatenate(
        [jnp.full((BLOCK, 1), sink_ref[g * A_GROUP + h], F32) for h in range(A_GROUP)], axis=0)
    m = jnp.maximum(jnp.max(s, axis=-1, keepdims=True), sink)
    p = jnp.exp(s - m)
    denom = jnp.sum(p, axis=-1, keepdims=True) + jnp.exp(sink - m)
    o = _dot(p.astype(BF16), v) / denom
    for h in range(A_GROUP):
        o_ref[:, h * A_HEAD_DIM:(h + 1) * A_HEAD_DIM] = o[h * BLOCK:(h + 1) * BLOCK].astype(o_ref.dtype)


def _windowed_attention(proj, bias_tab, sink, bsz, seq):
    nblk = seq // BLOCK
    qw = A_GROUP * A_HEAD_DIM
    kcol = COL_AK // A_HEAD_DIM
    vcol = COL_AV // A_HEAD_DIM

    def rows(b, n):
        return b * nblk + n

    def kv_spec(col0, shift):
        def imap(b, g, n):
            return (rows(b, jnp.clip(n + shift, 0, nblk - 1)), col0 + g)
        return pl.BlockSpec((BLOCK, A_HEAD_DIM), imap)

    return pl.pallas_call(
        _attn_kernel,
        grid=(bsz, A_KV_HEADS, nblk),
        in_specs=[pl.BlockSpec(memory_space=pltpu.SMEM),
                  pl.BlockSpec((BLOCK, qw), lambda b, g, n: (rows(b, n), g)),
                  kv_spec(kcol, -1), kv_spec(kcol, 0), kv_spec(kcol, 1),
                  kv_spec(vcol, -1), kv_spec(vcol, 0), kv_spec(vcol, 1),
                  pl.BlockSpec((A_GROUP, BLOCK, 3 * BLOCK), lambda b, g, n: (g, 0, 0))],
        out_specs=pl.BlockSpec((BLOCK, qw), lambda b, g, n: (rows(b, n), g)),
        out_shape=jax.ShapeDtypeStruct((bsz * seq, A_Q), BF16),
        compiler_params=_cparams(("parallel", "parallel", "arbitrary")),
        name="windowed_attention",
    )(sink, proj, proj, proj, proj, proj, proj, proj, bias_tab)


def _t5_bucket(rel):
    nb = N_BUCKETS // 2
    max_exact = nb // 2
    ret = jnp.where(rel > 0, nb, 0)
    n = jnp.abs(rel)
    nf = jnp.maximum(n, 1).astype(F32)
    large = max_exact + (jnp.log(nf / max_exact) / math.log(MAX_DISTANCE / max_exact)
                         * (nb - max_exact)).astype(jnp.int32)
    large = jnp.minimum(large, nb - 1)
    return ret + jnp.where(n < max_exact, n, large)


def _attention_bias_table(rel_bias):
    qi = jnp.arange(BLOCK)[:, None]
    kj = jnp.arange(3 * BLOCK)[None, :]
    rel = kj - BLOCK - qi
    bias = jnp.transpose(rel_bias.astype(F32)[_t5_bucket(rel)], (2, 0, 1))
    return jnp.where((jnp.abs(rel) <= WINDOW)[None], bias, NEG_INF)


def _gla_constants():
    c = GLA_BLOCK
    idx = np.arange(c)
    i = idx[:, None]
    u = idx[None, :]
    mats = [u <= i, u > i]
    masks = [np.eye(c, dtype=bool)]
    for s in GLA_LEVELS:
        upper = (i // s) % 2 == 1
        pivot = (i // (2 * s)) * (2 * s) + s - 1
        mats.append(upper & (u > pivot) & (u <= i))
        mats.append((~upper) & (u > i) & (u <= pivot))
        masks.append((i // (2 * s) == u // (2 * s)) & upper & ((u // s) % 2 == 0))
    fwd_d = np.concatenate(mats, axis=0)
    bwd_d = np.concatenate([m[::-1, ::-1] for m in mats], axis=0)
    fwd_m = np.stack(masks)
    bwd_m = np.stack([m[::-1, ::-1] for m in masks])
    d = np.stack([fwd_d, bwd_d]).astype(np.float32)
    m = np.stack([fwd_m, bwd_m]).astype(np.float32)
    return jnp.asarray(d, dtype=BF16), jnp.asarray(m, dtype=F32)


def _gla_block(direction, rows, q_ref, k_ref, v_ref, z_ref, w2_ref, b2_ref, d_ref, m_ref, st_ref):
    c = GLA_BLOCK
    q = q_ref[rows, :] * (B_KEY_DIM ** -0.5)
    k = k_ref[rows, :]
    v = v_ref[rows, :].astype(BF16)
    gate_cols = slice(direction * B_QK, (direction + 1) * B_QK)
    pre = _dot(z_ref[rows, :].astype(BF16), w2_ref[:, gate_cols]) + b2_ref[:, gate_cols]
    log_a = -_softplus(-pre) * (1.0 / GATE_TAU)
    hi = log_a.astype(BF16)
    rem = log_a - hi.astype(F32)
    mid = rem.astype(BF16)
    lo = (rem - mid.astype(F32)).astype(BF16)
    d = d_ref[direction]
    e = _dot(d, hi) + _dot(d, mid) + _dot(d, lo)
    e_q = e[0:c]
    e_k = e[c:2 * c]
    total = e_q[c - 1:c] if direction == 0 else e_q[0:1]
    q_inter = (q * jnp.exp(e_q)).astype(BF16)
    k_inter = (k * jnp.exp(e_k)).astype(BF16)
    state_decay = jnp.exp(total)
    q_lv = [(q * jnp.exp(e[(2 + 2 * l) * c:(3 + 2 * l) * c])).astype(BF16) for l in range(len(GLA_LEVELS))]
    k_lv = [(k * jnp.exp(e[(3 + 2 * l) * c:(4 + 2 * l) * c])).astype(BF16) for l in range(len(GLA_LEVELS))]
    q_b = q.astype(BF16)
    k_b = k.astype(BF16)
    outs = []
    for h in range(B_HEADS):
        hs = slice(h * B_KEY_DIM, (h + 1) * B_KEY_DIM)
        vs = slice(h * B_VAL_DIM, (h + 1) * B_VAL_DIM)
        state_t = st_ref[direction, h]
        o = _dot_nt(q_inter[:, hs], state_t.astype(BF16))
        a = None
        if direction == 0:
            a = m_ref[0, 0] * _dot_nt(q_b[:, hs], k_b[:, hs])
        for l in range(len(GLA_LEVELS)):
            p = m_ref[direction, l + 1] * _dot_nt(q_lv[l][:, hs], k_lv[l][:, hs])
            a = p if a is None else a + p
        o = o + _dot(a.astype(BF16), v[:, vs])
        st_ref[direction, h] = state_t * state_decay[:, hs] + _dot_tn(v[:, vs], k_inter[:, hs])
        outs.append(o)
    return jnp.concatenate(outs, axis=1)


def _gla_kernel(q_ref, k_ref, v_ref, g_ref, z_ref, w2_ref, b2_ref, gn_ref, d_ref, m_ref,
                o_ref, of_ref, st_ref):
    nblk = q_ref.shape[0] // GLA_BLOCK
    st_ref[...] = jnp.zeros(st_ref.shape, F32)
    block = functools.partial(_gla_block, q_ref=q_ref, k_ref=k_ref, v_ref=v_ref, z_ref=z_ref,
                              w2_ref=w2_ref, b2_ref=b2_ref, d_ref=d_ref, m_ref=m_ref, st_ref=st_ref)

    def forward(n, carry):
        rows = pl.ds(pl.multiple_of(n * GLA_BLOCK, GLA_BLOCK), GLA_BLOCK)
        of_ref[rows, :] = block(0, rows)
        return carry

    lax.fori_loop(0, nblk, forward, 0)

    def backward(step, carry):
        n = nblk - 1 - step
        rows = pl.ds(pl.multiple_of(n * GLA_BLOCK, GLA_BLOCK), GLA_BLOCK)
        o = of_ref[rows, :] + block(1, rows)
        normed = []
        for h in range(B_HEADS):
            oh = o[:, h * B_VAL_DIM:(h + 1) * B_VAL_DIM]
            normed.append(oh * _rms_scale(oh))
        g = g_ref[rows, :]
        out = jnp.concatenate(normed, axis=1) * gn_ref[...] * (g * jax.nn.sigmoid(g))
        o_ref[rows, :] = out.astype(o_ref.dtype)
        return carry

    lax.fori_loop(0, nblk, backward, 0)


def _gla(proj, w2, b2, gn, d_mats, masks, bsz, seq):
    def col(width, start):
        return pl.BlockSpec((seq, width), lambda b: (b, start // width))

    def whole(a):
        return pl.BlockSpec(a.shape, lambda b: (0,) * a.ndim)

    return pl.pallas_call(
        _gla_kernel,
        grid=(bsz,),
        in_specs=[col(B_QK, COL_BQ), col(B_QK, COL_BK), col(B_V, COL_BV), col(B_V, COL_BG),
                  col(LANE, COL_Z), whole(w2), whole(b2), whole(gn), whole(d_mats), whole(masks)],
        out_specs=pl.BlockSpec((seq, B_V), lambda b: (b, 0)),
        out_shape=jax.ShapeDtypeStruct((bsz * seq, B_V), BF16),
        scratch_shapes=[pltpu.VMEM((seq, B_V), F32),
                        pltpu.VMEM((2, B_HEADS, B_VAL_DIM, B_KEY_DIM), F32)],
        compiler_params=_cparams(("parallel",)),
        name="gla",
    )(proj, proj, proj, proj, proj, w2, b2, gn, d_mats, masks)


LRU_CHUNK = 256
SUBLANES = 8


def _lru_kernel(x_ref, y_ref, cw_ref, cb_ref, wa_ref, wx_ref, ba_ref, bx_ref, lam_ref, o_ref,
                xp_ref, af_ref, hf_ref, ab_ref, hb_ref):
    seq = x_ref.shape[0]
    ch = x_ref.shape[1]
    nchunk = seq // LRU_CHUNK
    groups = LRU_CHUNK // SUBLANES
    halo = SUBLANES
    xp_ref[0:halo, :] = jnp.zeros((halo, ch), F32)
    xp_ref[seq + halo:seq + 2 * halo, :] = jnp.zeros((halo, ch), F32)
    xp_ref[halo:seq + halo, :] = x_ref[...]
    decay_rate = _softplus(-lam_ref[...])
    sub = lax.broadcasted_iota(jnp.int32, (groups, SUBLANES, ch), 1)
    scan_refs = ((af_ref, hf_ref), (ab_ref, hb_ref))

    def gates_and_local_scan(c, carry):
        r0 = pl.multiple_of(c * LRU_CHUNK, LRU_CHUNK)
        ext = LRU_CHUNK + 2 * halo
        xe = xp_ref[pl.ds(r0, ext), :]
        xc = (cw_ref[0:1, :] * pltpu.roll(xe, 2, 0) + cw_ref[1:2, :] * pltpu.roll(xe, 1, 0)
              + cw_ref[2:3, :] * xe + cw_ref[3:4, :] * pltpu.roll(xe, ext - 1, 0))
        xc = xc[halo:halo + LRU_CHUNK] + cb_ref[...]
        xb = xc.astype(BF16)
        for d in range(2):
            r = jax.nn.sigmoid(_dot(xb, wa_ref[d, 0]) + ba_ref[d:d + 1, :])
            i = jax.nn.sigmoid(_dot(xb, wx_ref[d, 0]) + bx_ref[d:d + 1, :])
            log_a = -LRU_C * r * decay_rate[d:d + 1, :]
            a = jnp.exp(log_a).reshape(groups, SUBLANES, ch)
            t = jnp.tanh(log_a)
            u = (jnp.sqrt(-2.0 * t / (1.0 - t)) * (i * xc)).reshape(groups, SUBLANES, ch)
            for s in (1, 2, 4):
                shift = s if d == 0 else SUBLANES - s
                keep = (sub >= s) if d == 0 else (sub < SUBLANES - s)
                a_sh = pltpu.roll(a, shift, 1)
                u_sh = pltpu.roll(u, shift, 1)
                u = jnp.where(keep, a * u_sh + u, u)
                a = jnp.where(keep, a * a_sh, a)
            a_ref, h_ref = scan_refs[d]
            a_ref[pl.ds(r0, LRU_CHUNK), :] = a.reshape(LRU_CHUNK, ch)
            h_ref[pl.ds(r0, LRU_CHUNK), :] = u.reshape(LRU_CHUNK, ch)
        return carry

    lax.fori_loop(0, nchunk, gates_and_local_scan, 0)

    ngroups = seq // SUBLANES

    def carry_groups(gi, carry):
        cf, cb = carry
        rf = pl.ds(pl.multiple_of(gi * SUBLANES, SUBLANES), SUBLANES)
        hf = hf_ref[rf, :] + af_ref[rf, :] * cf
        hf_ref[rf, :] = hf
        rb = pl.ds(pl.multiple_of((ngroups - 1 - gi) * SUBLANES, SUBLANES), SUBLANES)
        hb = hb_ref[rb, :] + ab_ref[rb, :] * cb
        hb_ref[rb, :] = hb
        return hf[SUBLANES - 1:SUBLANES, :], hb[0:1, :]

    zero = jnp.zeros((1, ch), F32)
    lax.fori_loop(0, ngroups, carry_groups, (zero, zero), unroll=8)

    def gate_out(c, carry):
        rows = pl.ds(pl.multiple_of(c * LRU_CHUNK, LRU_CHUNK), LRU_CHUNK)
        h = hf_ref[rows, :] + hb_ref[rows, :]
        o_ref[rows, :] = (h * jax.nn.gelu(y_ref[rows, :])).astype(o_ref.dtype)
        return carry

    lax.fori_loop(0, nchunk, gate_out, 0)


def _rglru(proj, conv_w, conv_b, w_a, w_x, b_a, b_x, lam, bsz, seq):
    ch = C_BLOCK_DIM

    def col(start):
        return pl.BlockSpec((seq, ch), lambda b, c: (b, start // ch + c))

    def per_block(rows):
        return pl.BlockSpec((rows, ch), lambda b, c: (0, c))

    gate_w = pl.BlockSpec((2, 1, ch, ch), lambda b, c: (0, c, 0, 0))
    return pl.pallas_call(
        _lru_kernel,
        grid=(bsz, C_BLOCKS),
        in_specs=[col(COL_CX), col(COL_CY), per_block(CONV_WIDTH), per_block(1), gate_w, gate_w,
                  per_block(2), per_block(2), per_block(2)],
        out_specs=pl.BlockSpec((seq, ch), lambda b, c: (b, c)),
        out_shape=jax.ShapeDtypeStruct((bsz * seq, C_WIDTH), BF16),
        scratch_shapes=[pltpu.VMEM((seq + 2 * SUBLANES, ch), F32)] + [pltpu.VMEM((seq, ch), F32)] * 4,
        compiler_params=_cparams(("parallel", "parallel")),
        name="rglru",
    )(proj, proj, conv_w, conv_b.reshape(1, C_WIDTH), w_a, w_x, b_a, b_x, lam)


def _out_proj_kernel(oa_ref, ob_ref, oc_ref, w_ref, g_ref, x_ref, o_ref):
    mixed = (_dot(oa_ref[...], w_ref[0:A_Q, :]) + _dot(ob_ref[...], w_ref[A_Q:A_Q + B_V, :])
             + _dot(oc_ref[...], w_ref[A_Q + B_V:A_Q + B_V + C_WIDTH, :]))
    o_ref[...] = x_ref[...] + mixed * _rms_scale(mixed) * g_ref[...]


def _out_proj(oa, ob, oc, w, g, x, tm):
    m, d = x.shape

    def rows(width):
        return pl.BlockSpec((tm, width), lambda i: (i, 0))

    return pl.pallas_call(
        _out_proj_kernel,
        grid=(m // tm,),
        in_specs=[rows(A_Q), rows(B_V), rows(C_WIDTH),
                  pl.BlockSpec(w.shape, lambda i: (0, 0)), pl.BlockSpec((1, d), lambda i: (0, 0)),
                  rows(d)],
        out_specs=rows(d),
        out_shape=jax.ShapeDtypeStruct((m, d), F32),
        compiler_params=_cparams(("parallel",)),
        name="out_proj",
    )(oa, ob, oc, w, g.reshape(1, d), x)


def _cross_attn_kernel(q_ref, kv_ref, wo_ref, g_ref, x_ref, o_ref):
    acc = None
    for h in range(X_HEADS):
        hs = slice(h * X_HEAD_DIM, (h + 1) * X_HEAD_DIM)
        vs = slice(D_MODEL + h * X_HEAD_DIM, D_MODEL + (h + 1) * X_HEAD_DIM)
        s = _dot_nt(q_ref[:, hs], kv_ref[:, hs]) * (X_HEAD_DIM ** -0.5)
        m = jnp.max(s, axis=-1, keepdims=True)
        p = jnp.exp(s - m)
        o = _dot(p.astype(BF16), kv_ref[:, vs]) / jnp.sum(p, axis=-1, keepdims=True)
        part = _dot(o.astype(BF16), wo_ref[hs, :])
        acc = part if acc is None else acc + part
    o_ref[...] = x_ref[...] + acc * _rms_scale(acc) * g_ref[...]


def _cross_attention(q, kv, wo, g, x, bsz, seq, tq):
    d = D_MODEL
    per_row = seq // tq
    return pl.pallas_call(
        _cross_attn_kernel,
        grid=(bsz, per_row),
        in_specs=[pl.BlockSpec((tq, d), lambda b, i: (b * per_row + i, 0)),
                  pl.BlockSpec((MEM_LEN, 2 * d), lambda b, i: (b, 0)),
                  pl.BlockSpec((d, d), lambda b, i: (0, 0)),
                  pl.BlockSpec((1, d), lambda b, i: (0, 0)),
                  pl.BlockSpec((tq, d), lambda b, i: (b * per_row + i, 0))],
        out_specs=pl.BlockSpec((tq, d), lambda b, i: (b * per_row + i, 0)),
        out_shape=jax.ShapeDtypeStruct((bsz * seq, d), F32),
        compiler_params=_cparams(("parallel", "parallel")),
        name="cross_attention",
    )(q, kv, wo, g.reshape(1, d), x)


def _ffn_kernel(x_ref, gpre_ref, wu_ref, wd_ref, gpost_ref, o_ref, hn_ref, acc_ref):
    f = pl.program_id(1)

    @pl.when(f == 0)
    def _():
        x = x_ref[...]
        hn_ref[...] = (x * _rms_scale(x) * gpre_ref[...]).astype(BF16)
        acc_ref[...] = jnp.zeros(acc_ref.shape, F32)

    up = _dot(hn_ref[...], wu_ref[...])
    act = jnp.square(jnp.maximum(up, 0.0)).astype(BF16)
    acc_ref[...] += _dot(act, wd_ref[...])

    @pl.when(f == pl.num_programs(1) - 1)
    def _():
        ff = acc_ref[...]
        o_ref[...] = x_ref[...] + ff * _rms_scale(ff) * gpost_ref[...]


def _ffn(x, gpre, w_up, w_down, gpost, tm, tf):
    m, d = x.shape
    dff = w_up.shape[1]
    return pl.pallas_call(
        _ffn_kernel,
        grid=(m // tm, dff // tf),
        in_specs=[pl.BlockSpec((tm, d), lambda i, f: (i, 0)),
                  pl.BlockSpec((1, d), lambda i, f: (0, 0)),
                  pl.BlockSpec((d, tf), lambda i, f: (0, f)),
                  pl.BlockSpec((tf, d), lambda i, f: (f, 0)),
                  pl.BlockSpec((1, d), lambda i, f: (0, 0))],
        out_specs=pl.BlockSpec((tm, d), lambda i, f: (i, 0)),
        out_shape=jax.ShapeDtypeStruct((m, d), F32),
        scratch_shapes=[pltpu.VMEM((tm, d), BF16), pltpu.VMEM((tm, d), F32)],
        compiler_params=_cparams(("parallel", "arbitrary")),
        name="ffn",
    )(x, gpre.reshape(1, d), w_up, w_down, gpost.reshape(1, d))


def _reorder_w_in(w):
    z0 = A_Q + 2 * A_KV + 2 * B_QK + 2 * B_V
    z1 = z0 + 2 * GATE_RANK
    pad = jnp.zeros((w.shape[0], LANE - 2 * GATE_RANK), w.dtype)
    return jnp.concatenate([w[:, :z0], w[:, z1:], w[:, z0:z1], pad], axis=1).astype(BF16)


def _gate_expansion(w2_f, w2_b):
    w = jnp.zeros((LANE, 2 * B_QK), F32)
    w = w.at[0:GATE_RANK, 0:B_QK].set(w2_f)
    w = w.at[GATE_RANK:2 * GATE_RANK, B_QK:].set(w2_b)
    return w.astype(BF16)


def kernel(x, mem, rel_bias, w_in, w_out, attn_sink, gla_w2_f, gla_b2_f, gla_w2_b, gla_b2_b, gla_norm, conv_w, conv_b, lru_wa, lru_ba, lru_wx, lru_bx, lru_lambda, xq, xk, xv, xo, w_up, w_down, norm_mix_pre, norm_mix_post, norm_mem, norm_x_pre, norm_x_post, norm_ff_pre, norm_ff_post):
    bsz, seq, d = x.shape
    assert d == D_MODEL and seq % LRU_CHUNK == 0 and seq % 512 == 0
    xs = x.reshape(bsz * seq, d)
    mems = mem.reshape(bsz * mem.shape[1], d)
    bias_tab = _attention_bias_table(rel_bias)
    d_mats, masks = _gla_constants()
    for l in range(DEPTH):
        proj = _norm_matmul(xs, norm_mix_pre[l], _reorder_w_in(w_in[l]), F32, tm=512, tn=1408)
        oa = _windowed_attention(proj, bias_tab, attn_sink[l], bsz, seq)
        ob = _gla(proj, _gate_expansion(gla_w2_f[l], gla_w2_b[l]),
                  jnp.concatenate([gla_b2_f[l], gla_b2_b[l]]).reshape(1, 2 * B_QK),
                  gla_norm[l].reshape(1, B_V), d_mats, masks, bsz, seq)
        oc = _rglru(proj, conv_w[l], conv_b[l], lru_wa[l].astype(BF16), lru_wx[l].astype(BF16),
                    lru_ba[l], lru_bx[l], lru_lambda[l], bsz, seq)
        xs = _out_proj(oa, ob, oc, w_out[l].astype(BF16), norm_mix_post[l], xs, tm=512)

        q = _norm_matmul(xs, norm_x_pre[l], xq[l].astype(BF16), BF16, tm=512, tn=1024)
        w_kv = jnp.concatenate([xk[l], xv[l]], axis=1).astype(BF16)
        kv = _norm_matmul(mems, norm_mem[l], w_kv, BF16, tm=512, tn=1024)
        xs = _cross_attention(q, kv, xo[l].astype(BF16), norm_x_post[l], xs, bsz, seq, tq=512)

        xs = _ffn(xs, norm_ff_pre[l], w_up[l].astype(BF16), w_down[l].astype(BF16),
                  norm_ff_post[l], tm=512, tf=1024)
    return xs.reshape(bsz, seq, d)
```

```python
import functools
import math

import numpy as np
import jax
import jax.numpy as jnp
from jax import lax
from jax.experimental import pallas as pl
from jax.experimental.pallas import tpu as pltpu

F32 = jnp.float32
BF16 = jnp.bfloat16

D_MODEL = 2048
DEPTH = 4
MEM_LEN = 256
A_HEAD_DIM = 128
A_HEADS = 8
A_KV_HEADS = 2
A_GROUP = 4
WINDOW = 128
BLOCK = 128
N_BUCKETS = 32
MAX_DISTANCE = 128
B_HEADS = 4
B_KEY_DIM = 64
B_VAL_DIM = 128
GATE_RANK = 16
GATE_TAU = 16.0
C_WIDTH = 512
C_BLOCKS = 4
C_BLOCK_DIM = 128
CONV_WIDTH = 4
LRU_C = 8.0
X_HEADS = 4
X_HEAD_DIM = 512
D_FF = 8192
EPS = 1e-6
NEG_INF = -1e30

A_Q = A_HEADS * A_HEAD_DIM
A_KV = A_KV_HEADS * A_HEAD_DIM
B_QK = B_HEADS * B_KEY_DIM
B_V = B_HEADS * B_VAL_DIM

LANE = 128
COL_AQ = 0
COL_AK = COL_AQ + A_Q
COL_AV = COL_AK + A_KV
COL_BQ = COL_AV + A_KV
COL_BK = COL_BQ + B_QK
COL_BV = COL_BK + B_QK
COL_BG = COL_BV + B_V
COL_CX = COL_BG + B_V
COL_CY = COL_CX + C_WIDTH
COL_Z = COL_CY + C_WIDTH
D_PROJ = COL_Z + LANE

GLA_BLOCK = 128
GLA_LEVELS = (1, 2, 4, 8, 16, 32, 64)
GLA_PAIRS = B_HEADS // 2
VMEM_LIMIT_MIB = 56


def _cparams(semantics, vmem_mib=VMEM_LIMIT_MIB):
    return pltpu.CompilerParams(dimension_semantics=semantics,
                                vmem_limit_bytes=vmem_mib * 1024 * 1024)


def _dot(a, b):
    return jnp.dot(a, b, preferred_element_type=F32)


def _dot_nt(a, b):
    return lax.dot_general(a, b, (((1,), (1,)), ((), ())), preferred_element_type=F32)


def _dot_tn(a, b):
    return lax.dot_general(a, b, (((0,), (0,)), ((), ())), preferred_element_type=F32)


def _softplus(x):
    return jnp.maximum(x, 0.0) + jnp.log(1.0 + jnp.exp(-jnp.abs(x)))


def _rms_scale(x):
    return lax.rsqrt(jnp.mean(jnp.square(x), axis=-1, keepdims=True) + EPS)


def _norm_matmul_kernel(x_ref, g_ref, w_ref, o_ref, hn_ref):
    @pl.when(pl.program_id(1) == 0)
    def _():
        x = x_ref[...]
        hn_ref[...] = (x * _rms_scale(x) * g_ref[...]).astype(BF16)

    o_ref[...] = _dot(hn_ref[...], w_ref[...]).astype(o_ref.dtype)


def _norm_matmul(x, g, w, layer, out_dtype, tm, tn):
    m, k = x.shape
    n = w.shape[2]
    return pl.pallas_call(
        _norm_matmul_kernel,
        grid=(m // tm, n // tn),
        in_specs=[pl.BlockSpec((tm, k), lambda i, j: (i, 0)),
                  pl.BlockSpec((1, k), lambda i, j: (0, 0)),
                  pl.BlockSpec((None, k, tn), lambda i, j: (layer, 0, j))],
        out_specs=pl.BlockSpec((tm, tn), lambda i, j: (i, j)),
        out_shape=jax.ShapeDtypeStruct((m, n), out_dtype),
        scratch_shapes=[pltpu.VMEM((tm, k), BF16)],
        compiler_params=_cparams(("parallel", "arbitrary")),
        name="norm_matmul",
    )(x, g.reshape(1, k), w)


def _attn_kernel(sink_ref, q_ref, k_ref, v_ref, bias_ref, o_ref, kb_ref, vt_ref):
    g = pl.program_id(1)
    seq = q_ref.shape[0]
    nblk = seq // BLOCK
    kb_ref[...] = k_ref[...].astype(BF16)
    for n in range(nblk):
        vt_ref[n] = v_ref[n * BLOCK:(n + 1) * BLOCK, :].T.astype(BF16)
    sink = jnp.concatenate(
        [jnp.full((1, BLOCK), sink_ref[g * A_GROUP + h], F32) for h in range(A_GROUP)], axis=1)

    def block(n, first, last):
        t0 = 1 if first else 0
        t1 = 2 if last else 3
        nk = (t1 - t0) * BLOCK
        r0 = pl.multiple_of(n * BLOCK, BLOCK)
        k = kb_ref[pl.ds(pl.multiple_of(r0 + (t0 - 1) * BLOCK, BLOCK), nk), :]
        q = jnp.concatenate([q_ref[pl.ds(r0, BLOCK), h * A_HEAD_DIM:(h + 1) * A_HEAD_DIM]
                             for h in range(A_GROUP)], axis=0).astype(BF16)
        s = _dot_nt(k, q) * (A_HEAD_DIM ** -0.5) + bias_ref[t0 * BLOCK:t1 * BLOCK, :]
        m = jnp.maximum(jnp.max(s, axis=0, keepdims=True), sink)
        p = jnp.exp(s - m).astype(BF16)
        v_t = jnp.concatenate([vt_ref[n + t - 1] for t in range(t0, t1)], axis=1)
        ones = jnp.ones((2 * SUBLANES, nk), BF16)
        o = _dot(jnp.concatenate([v_t, ones], axis=0), p)
        denom = o[A_HEAD_DIM:A_HEAD_DIM + 1, :] + jnp.exp(sink - m)
        o = o[0:A_HEAD_DIM, :] / denom
        for h in range(A_GROUP):
            o_ref[pl.ds(r0, BLOCK), h * A_HEAD_DIM:(h + 1) * A_HEAD_DIM] = (
                o[:, h * BLOCK:(h + 1) * BLOCK].T.astype(o_ref.dtype))

    def body(n, carry):
        block(n, False, False)
        return carry

    block(0, True, False)
    lax.fori_loop(1, nblk - 1, body, 0, unroll=2)
    block(nblk - 1, False, True)


def _windowed_attention(proj, bias_tab, sink, bsz, seq):
    qw = A_GROUP * A_HEAD_DIM
    nblk = seq // BLOCK

    def kv_spec(col0):
        return pl.BlockSpec((seq, A_HEAD_DIM), lambda b, g: (b, col0 // A_HEAD_DIM + g))

    return pl.pallas_call(
        _attn_kernel,
        grid=(bsz, A_KV_HEADS),
        in_specs=[pl.BlockSpec(memory_space=pltpu.SMEM),
                  pl.BlockSpec((seq, qw), lambda b, g: (b, g)),
                  kv_spec(COL_AK), kv_spec(COL_AV),
                  pl.BlockSpec((None, 3 * BLOCK, qw), lambda b, g: (g, 0, 0))],
        out_specs=pl.BlockSpec((seq, qw), lambda b, g: (b, g)),
        out_shape=jax.ShapeDtypeStruct((bsz * seq, A_Q), BF16),
        scratch_shapes=[pltpu.VMEM((seq, A_HEAD_DIM), BF16),
                        pltpu.VMEM((nblk, A_HEAD_DIM, BLOCK), BF16)],
        compiler_params=_cparams(("parallel", "parallel")),
        name="windowed_attention",
    )(sink, proj, proj, proj, bias_tab)


def _t5_bucket(rel):
    nb = N_BUCKETS // 2
    max_exact = nb // 2
    ret = jnp.where(rel > 0, nb, 0)
    n = jnp.abs(rel)
    nf = jnp.maximum(n, 1).astype(F32)
    large = max_exact + (jnp.log(nf / max_exact) / math.log(MAX_DISTANCE / max_exact)
                         * (nb - max_exact)).astype(jnp.int32)
    large = jnp.minimum(large, nb - 1)
    return ret + jnp.where(n < max_exact, n, large)


def _attention_bias_table(rel_bias):
    kj = jnp.arange(3 * BLOCK)[:, None]
    qi = jnp.arange(BLOCK)[None, :]
    rel = kj - BLOCK - qi
    onehot = jax.nn.one_hot(_t5_bucket(rel), N_BUCKETS, dtype=F32)
    bias = jnp.einsum("kqn,nh->hkq", onehot, rel_bias.astype(F32), precision=lax.Precision.HIGHEST)
    bias = jnp.where((jnp.abs(rel) <= WINDOW)[None], bias, NEG_INF)
    bias = bias.reshape(A_KV_HEADS, A_GROUP, 3 * BLOCK, BLOCK)
    return jnp.transpose(bias, (0, 2, 1, 3)).reshape(A_KV_HEADS, 3 * BLOCK, A_GROUP * BLOCK)


def _gla_masks():
    c = GLA_BLOCK
    i = np.arange(c)[:, None]
    j = np.arange(c)[None, :]
    fwd = [i == j]
    for s in GLA_LEVELS:
        fwd.append((i // (2 * s) == j // (2 * s)) & ((i // s) % 2 == 1) & ((j // s) % 2 == 0))
    fwd = np.stack(fwd)
    return jnp.asarray(np.stack([fwd, fwd.transpose(0, 2, 1)]).astype(np.float32))


def _gla_block(direction, rows, q_ref, k_ref, v_ref, z_ref, w2_ref, b2_ref, m_ref, st_ref):
    c = GLA_BLOCK
    q = q_ref[rows, :] * (B_KEY_DIM ** -0.5)
    k = k_ref[rows, :]
    v = v_ref[rows, :].astype(BF16)
    gate_cols = slice(direction * B_QK, (direction + 1) * B_QK)
    pre = _dot(z_ref[rows, :].astype(BF16), w2_ref[:, gate_cols]) + b2_ref[:, gate_cols]
    log_a = -_softplus(-pre) * (1.0 / GATE_TAU)

    row = lax.broadcasted_iota(jnp.int32, (c, B_QK), 0)
    first_head = lax.broadcasted_iota(jnp.int32, (c, LANE), 1) < B_KEY_DIM
    pair = [slice(p * LANE, (p + 1) * LANE) for p in range(GLA_PAIRS)]
    k_first = [jnp.where(first_head, k[:, ps], 0.0) for ps in pair]
    k_second = [jnp.where(first_head, 0.0, k[:, ps]) for ps in pair]

    def exponents(p_s, t_s):
        if direction == 0:
            return p_s, t_s - p_s
        return t_s - p_s + log_a, p_s - log_a

    def scaled(e_q, e_k, p):
        w_q = jnp.exp(e_q[:, pair[p]])
        w_k = jnp.exp(e_k[:, pair[p]])
        q_t = (q[:, pair[p]] * w_q).astype(BF16)
        k_t = jnp.concatenate([(k_first[p] * w_k).astype(BF16), (k_second[p] * w_k).astype(BF16)], axis=0)
        return q_t, k_t

    attn = [None] * GLA_PAIRS

    def add_level(level, q_t, k_t, p):
        mask = m_ref[direction, level]
        contrib = jnp.concatenate([mask, mask], axis=1) * _dot_nt(q_t, k_t)
        attn[p] = contrib if attn[p] is None else attn[p] + contrib

    if direction == 0:
        for p in range(GLA_PAIRS):
            k_t = jnp.concatenate([k_first[p].astype(BF16), k_second[p].astype(BF16)], axis=0)
            add_level(0, q[:, pair[p]].astype(BF16), k_t, p)
    p_s = log_a
    t_s = log_a
    for li, s in enumerate(GLA_LEVELS):
        e_q, e_k = exponents(p_s, t_s)
        for p in range(GLA_PAIRS):
            q_t, k_t = scaled(e_q, e_k, p)
            add_level(li + 1, q_t, k_t, p)
        upper = (row & s) != 0
        t_before = pltpu.roll(t_s, s, 0)
        t_after = pltpu.roll(t_s, c - s, 0)
        p_s = p_s + jnp.where(upper, t_before, 0.0)
        t_s = t_s + jnp.where(upper, t_before, t_after)

    e_q, e_k = exponents(p_s, t_s)
    state_decay = jnp.exp(t_s[0:1, :])
    own_block = lax.broadcasted_iota(jnp.int32, (2 * B_VAL_DIM, LANE), 0) < B_VAL_DIM
    own_block = own_block == (lax.broadcasted_iota(jnp.int32, (2 * B_VAL_DIM, LANE), 1) < B_KEY_DIM)
    zero_v = jnp.zeros((c, B_VAL_DIM), BF16)
    outs = []
    for p in range(GLA_PAIRS):
        v_pair = v[:, 2 * p * B_VAL_DIM:(2 * p + 2) * B_VAL_DIM]
        v_diag = jnp.concatenate(
            [jnp.concatenate([v_pair[:, :B_VAL_DIM], zero_v], axis=1),
             jnp.concatenate([zero_v, v_pair[:, B_VAL_DIM:]], axis=1)], axis=0)
        state_t = st_ref[direction, p]
        q_t = (q[:, pair[p]] * jnp.exp(e_q[:, pair[p]])).astype(BF16)
        k_t = (k[:, pair[p]] * jnp.exp(e_k[:, pair[p]])).astype(BF16)
        o = _dot_nt(q_t, state_t.astype(BF16)) + _dot(attn[p].astype(BF16), v_diag)
        update = jnp.where(own_block, _dot_tn(v_pair, k_t), 0.0)
        st_ref[direction, p] = state_t * state_decay[:, pair[p]] + update
        outs.append(o)
    return jnp.concatenate(outs, axis=1)


def _gla_kernel(q_ref, k_ref, v_ref, g_ref, z_ref, w2_ref, b2_ref, gn_ref, m_ref,
                o_ref, of_ref, st_ref):
    nblk = q_ref.shape[0] // GLA_BLOCK
    st_ref[...] = jnp.zeros(st_ref.shape, F32)
    block = functools.partial(_gla_block, q_ref=q_ref, k_ref=k_ref, v_ref=v_ref, z_ref=z_ref,
                              w2_ref=w2_ref, b2_ref=b2_ref, m_ref=m_ref, st_ref=st_ref)

    def forward(n, carry):
        rows = pl.ds(pl.multiple_of(n * GLA_BLOCK, GLA_BLOCK), GLA_BLOCK)
        of_ref[rows, :] = block(0, rows)
        return carry

    lax.fori_loop(0, nblk, forward, 0)

    def backward(step, carry):
        n = nblk - 1 - step
        rows = pl.ds(pl.multiple_of(n * GLA_BLOCK, GLA_BLOCK), GLA_BLOCK)
        o = of_ref[rows, :] + block(1, rows)
        normed = []
        for h in range(B_HEADS):
            oh = o[:, h * B_VAL_DIM:(h + 1) * B_VAL_DIM]
            normed.append(oh * _rms_scale(oh))
        g = g_ref[rows, :]
        out = jnp.concatenate(normed, axis=1) * gn_ref[...] * (g * jax.nn.sigmoid(g))
        o_ref[rows, :] = out.astype(o_ref.dtype)
        return carry

    lax.fori_loop(0, nblk, backward, 0)


def _gla(proj, w2, layer, b2, gn, masks, bsz, seq):
    def col(width, start):
        return pl.BlockSpec((seq, width), lambda b: (b, start // width))

    def whole(a):
        return pl.BlockSpec(a.shape, lambda b: (0,) * a.ndim)

    return pl.pallas_call(
        _gla_kernel,
        grid=(bsz,),
        in_specs=[col(B_QK, COL_BQ), col(B_QK, COL_BK), col(B_V, COL_BV), col(B_V, COL_BG),
                  col(LANE, COL_Z), pl.BlockSpec((None,) + w2.shape[1:], lambda b: (layer, 0, 0)),
                  whole(b2), whole(gn), whole(masks)],
        out_specs=pl.BlockSpec((seq, B_V), lambda b: (b, 0)),
        out_shape=jax.ShapeDtypeStruct((bsz * seq, B_V), BF16),
        scratch_shapes=[pltpu.VMEM((seq, B_V), F32),
                        pltpu.VMEM((2, GLA_PAIRS, 2 * B_VAL_DIM, LANE), F32)],
        compiler_params=_cparams(("parallel",)),
        name="gla",
    )(proj, proj, proj, proj, proj, w2, b2, gn, masks)


LRU_CHUNK = 256
SUBLANES = 8


def _lru_kernel(x_ref, y_ref, cw_ref, cb_ref, wa_ref, wx_ref, ba_ref, bx_ref, lam_ref, o_ref,
                xp_ref, af_ref, hf_ref, ab_ref, hb_ref):
    seq = x_ref.shape[0]
    ch = x_ref.shape[1]
    nchunk = seq // LRU_CHUNK
    groups = LRU_CHUNK // SUBLANES
    halo = SUBLANES
    xp_ref[0:halo, :] = jnp.zeros((halo, ch), F32)
    xp_ref[seq + halo:seq + 2 * halo, :] = jnp.zeros((halo, ch), F32)
    xp_ref[halo:seq + halo, :] = x_ref[...]
    decay_rate = _softplus(-lam_ref[...])
    sub = lax.broadcasted_iota(jnp.int32, (groups, SUBLANES, ch), 1)
    scan_refs = ((af_ref, hf_ref), (ab_ref, hb_ref))

    def gates_and_local_scan(c, carry):
        r0 = pl.multiple_of(c * LRU_CHUNK, LRU_CHUNK)
        ext = LRU_CHUNK + 2 * halo
        xe = xp_ref[pl.ds(r0, ext), :]
        xc = (cw_ref[0:1, :] * pltpu.roll(xe, 2, 0) + cw_ref[1:2, :] * pltpu.roll(xe, 1, 0)
              + cw_ref[2:3, :] * xe + cw_ref[3:4, :] * pltpu.roll(xe, ext - 1, 0))
        xc = xc[halo:halo + LRU_CHUNK] + cb_ref[...]
        xb = xc.astype(BF16)
        for d in range(2):
            r = jax.nn.sigmoid(_dot(xb, wa_ref[d, 0]) + ba_ref[d:d + 1, :])
            i = jax.nn.sigmoid(_dot(xb, wx_ref[d, 0]) + bx_ref[d:d + 1, :])
            log_a = -LRU_C * r * decay_rate[d:d + 1, :]
            a = jnp.exp(log_a).reshape(groups, SUBLANES, ch)
            t = jnp.tanh(log_a)
            u = (jnp.sqrt(-2.0 * t / (1.0 - t)) * (i * xc)).reshape(groups, SUBLANES, ch)
            for s in (1, 2, 4):
                shift = s if d == 0 else SUBLANES - s
                keep = (sub >= s) if d == 0 else (sub < SUBLANES - s)
                a_sh = pltpu.roll(a, shift, 1)
                u_sh = pltpu.roll(u, shift, 1)
                u = jnp.where(keep, a * u_sh + u, u)
                a = jnp.where(keep, a * a_sh, a)
            a_ref, h_ref = scan_refs[d]
            a_ref[pl.ds(r0, LRU_CHUNK), :] = a.reshape(LRU_CHUNK, ch)
            h_ref[pl.ds(r0, LRU_CHUNK), :] = u.reshape(LRU_CHUNK, ch)
        return carry

    lax.fori_loop(0, nchunk, gates_and_local_scan, 0)

    ngroups = seq // SUBLANES

    def carry_groups(gi, carry):
        cf, cb = carry
        rf = pl.ds(pl.multiple_of(gi * SUBLANES, SUBLANES), SUBLANES)
        hf = hf_ref[rf, :] + af_ref[rf, :] * cf
        hf_ref[rf, :] = hf
        rb = pl.ds(pl.multiple_of((ngroups - 1 - gi) * SUBLANES, SUBLANES), SUBLANES)
        hb = hb_ref[rb, :] + ab_ref[rb, :] * cb
        hb_ref[rb, :] = hb
        return hf[SUBLANES - 1:SUBLANES, :], hb[0:1, :]

    zero = jnp.zeros((1, ch), F32)
    lax.fori_loop(0, ngroups, carry_groups, (zero, zero), unroll=8)

    def gate_out(c, carry):
        rows = pl.ds(pl.multiple_of(c * LRU_CHUNK, LRU_CHUNK), LRU_CHUNK)
        h = hf_ref[rows, :] + hb_ref[rows, :]
        o_ref[rows, :] = (h * jax.nn.gelu(y_ref[rows, :])).astype(o_ref.dtype)
        return carry

    lax.fori_loop(0, nchunk, gate_out, 0)


def _rglru(proj, conv_w, conv_b, w_a, w_x, layer, b_a, b_x, lam, bsz, seq):
    ch = C_BLOCK_DIM

    def col(start):
        return pl.BlockSpec((seq, ch), lambda b, c: (b, start // ch + c))

    def per_block(rows):
        return pl.BlockSpec((rows, ch), lambda b, c: (0, c))

    gate_w = pl.BlockSpec((None, 2, 1, ch, ch), lambda b, c: (layer, 0, c, 0, 0))
    return pl.pallas_call(
        _lru_kernel,
        grid=(bsz, C_BLOCKS),
        in_specs=[col(COL_CX), col(COL_CY), per_block(CONV_WIDTH), per_block(1), gate_w, gate_w,
                  per_block(2), per_block(2), per_block(2)],
        out_specs=pl.BlockSpec((seq, ch), lambda b, c: (b, c)),
        out_shape=jax.ShapeDtypeStruct((bsz * seq, C_WIDTH), BF16),
        scratch_shapes=[pltpu.VMEM((seq + 2 * SUBLANES, ch), F32)] + [pltpu.VMEM((seq, ch), F32)] * 4,
        compiler_params=_cparams(("parallel", "parallel")),
        name="rglru",
    )(proj, proj, conv_w, conv_b.reshape(1, C_WIDTH), w_a, w_x, b_a, b_x, lam)


def _out_proj_kernel(oa_ref, ob_ref, oc_ref, w_ref, g_ref, x_ref, o_ref):
    mixed = (_dot(oa_ref[...], w_ref[0:A_Q, :]) + _dot(ob_ref[...], w_ref[A_Q:A_Q + B_V, :])
             + _dot(oc_ref[...], w_ref[A_Q + B_V:A_Q + B_V + C_WIDTH, :]))
    o_ref[...] = x_ref[...] + mixed * _rms_scale(mixed) * g_ref[...]


def _out_proj(oa, ob, oc, w, layer, g, x, tm):
    m, d = x.shape

    def rows(width):
        return pl.BlockSpec((tm, width), lambda i: (i, 0))

    return pl.pallas_call(
        _out_proj_kernel,
        grid=(m // tm,),
        in_specs=[rows(A_Q), rows(B_V), rows(C_WIDTH),
                  pl.BlockSpec((None,) + w.shape[1:], lambda i: (layer, 0, 0)),
                  pl.BlockSpec((1, d), lambda i: (0, 0)),
                  rows(d)],
        out_specs=rows(d),
        out_shape=jax.ShapeDtypeStruct((m, d), F32),
        compiler_params=_cparams(("parallel",)),
        name="out_proj",
    )(oa, ob, oc, w, g.reshape(1, d), x)


def _cross_attn_kernel(q_ref, k_ref, v_ref, wo_ref, g_ref, x_ref, o_ref):
    acc = None
    for h in range(X_HEADS):
        hs = slice(h * X_HEAD_DIM, (h + 1) * X_HEAD_DIM)
        s = _dot_nt(q_ref[:, hs], k_ref[:, hs]) * (X_HEAD_DIM ** -0.5)
        m = jnp.max(s, axis=-1, keepdims=True)
        p = jnp.exp(s - m)
        o = _dot(p.astype(BF16), v_ref[:, hs]) / jnp.sum(p, axis=-1, keepdims=True)
        part = _dot(o.astype(BF16), wo_ref[hs, :])
        acc = part if acc is None else acc + part
    o_ref[...] = x_ref[...] + acc * _rms_scale(acc) * g_ref[...]


def _cross_attention(q, k, v, wo, layer, g, x, bsz, seq, tq):
    d = D_MODEL
    per_row = seq // tq
    mem_spec = pl.BlockSpec((MEM_LEN, d), lambda b, i: (b, 0))
    return pl.pallas_call(
        _cross_attn_kernel,
        grid=(bsz, per_row),
        in_specs=[pl.BlockSpec((tq, d), lambda b, i: (b * per_row + i, 0)),
                  mem_spec, mem_spec,
                  pl.BlockSpec((None, d, d), lambda b, i: (layer, 0, 0)),
                  pl.BlockSpec((1, d), lambda b, i: (0, 0)),
                  pl.BlockSpec((tq, d), lambda b, i: (b * per_row + i, 0))],
        out_specs=pl.BlockSpec((tq, d), lambda b, i: (b * per_row + i, 0)),
        out_shape=jax.ShapeDtypeStruct((bsz * seq, d), F32),
        compiler_params=_cparams(("parallel", "parallel")),
        name="cross_attention",
    )(q, k, v, wo, g.reshape(1, d), x)


def _ffn_kernel(x_ref, gpre_ref, wu_ref, wd_ref, gpost_ref, o_ref, hn_ref, acc_ref):
    f = pl.program_id(1)

    @pl.when(f == 0)
    def _():
        x = x_ref[...]
        hn_ref[...] = (x * _rms_scale(x) * gpre_ref[...]).astype(BF16)
        acc_ref[...] = jnp.zeros(acc_ref.shape, F32)

    up = _dot(hn_ref[...], wu_ref[...])
    act = jnp.square(jnp.maximum(up, 0.0)).astype(BF16)
    acc_ref[...] += _dot(act, wd_ref[...])

    @pl.when(f == pl.num_programs(1) - 1)
    def _():
        ff = acc_ref[...]
        o_ref[...] = x_ref[...] + ff * _rms_scale(ff) * gpost_ref[...]


def _ffn(x, gpre, w_up, w_down, layer, gpost, tm, tf):
    m, d = x.shape
    dff = w_up.shape[2]
    return pl.pallas_call(
        _ffn_kernel,
        grid=(m // tm, dff // tf),
        in_specs=[pl.BlockSpec((tm, d), lambda i, f: (i, 0)),
                  pl.BlockSpec((1, d), lambda i, f: (0, 0)),
                  pl.BlockSpec((None, d, tf), lambda i, f: (layer, 0, f)),
                  pl.BlockSpec((None, tf, d), lambda i, f: (layer, f, 0)),
                  pl.BlockSpec((1, d), lambda i, f: (0, 0))],
        out_specs=pl.BlockSpec((tm, d), lambda i, f: (i, 0)),
        out_shape=jax.ShapeDtypeStruct((m, d), F32),
        scratch_shapes=[pltpu.VMEM((tm, d), BF16), pltpu.VMEM((tm, d), F32)],
        compiler_params=_cparams(("parallel", "arbitrary")),
        name="ffn",
    )(x, gpre.reshape(1, d), w_up, w_down, gpost.reshape(1, d))


def _reorder_w_in(w):
    z0 = A_Q + 2 * A_KV + 2 * B_QK + 2 * B_V
    z1 = z0 + 2 * GATE_RANK
    pad = jnp.zeros(w.shape[:-1] + (LANE - 2 * GATE_RANK,), BF16)
    return jnp.concatenate([w[..., :z0].astype(BF16), w[..., z1:].astype(BF16),
                            w[..., z0:z1].astype(BF16), pad], axis=-1)


def _gate_expansion(w2_f, w2_b):
    w = jnp.zeros((w2_f.shape[0], LANE, 2 * B_QK), F32)
    w = w.at[:, 0:GATE_RANK, 0:B_QK].set(w2_f)
    w = w.at[:, GATE_RANK:2 * GATE_RANK, B_QK:].set(w2_b)
    return w.astype(BF16)


def kernel(x, mem, rel_bias, w_in, w_out, attn_sink, gla_w2_f, gla_b2_f, gla_w2_b, gla_b2_b, gla_norm, conv_w, conv_b, lru_wa, lru_ba, lru_wx, lru_bx, lru_lambda, xq, xk, xv, xo, w_up, w_down, norm_mix_pre, norm_mix_post, norm_mem, norm_x_pre, norm_x_post, norm_ff_pre, norm_ff_post):
    bsz, seq, d = x.shape
    assert d == D_MODEL and seq % LRU_CHUNK == 0 and seq % 512 == 0
    xs = x.reshape(bsz * seq, d)
    mems = mem.reshape(bsz * mem.shape[1], d)
    bias_tab = _attention_bias_table(rel_bias)
    masks = _gla_masks()
    w_in_b = _reorder_w_in(w_in)
    w2_b = _gate_expansion(gla_w2_f, gla_w2_b)
    b2 = jnp.concatenate([gla_b2_f, gla_b2_b], axis=-1)
    bf = lambda w: w.astype(BF16)
    w_out_b, wa_b, wx_b = bf(w_out), bf(lru_wa), bf(lru_wx)
    xq_b, xk_b, xv_b, xo_b = bf(xq), bf(xk), bf(xv), bf(xo)
    w_up_b, w_down_b = bf(w_up), bf(w_down)
    for l in range(DEPTH):
        proj = _norm_matmul(xs, norm_mix_pre[l], w_in_b, l, F32, tm=512, tn=1408)
        oa = _windowed_attention(proj, bias_tab, attn_sink[l], bsz, seq)
        ob = _gla(proj, w2_b, l, b2[l].reshape(1, 2 * B_QK), gla_norm[l].reshape(1, B_V), masks, bsz, seq)
        oc = _rglru(proj, conv_w[l], conv_b[l], wa_b, wx_b, l, lru_ba[l], lru_bx[l], lru_lambda[l],
                    bsz, seq)
        xs = _out_proj(oa, ob, oc, w_out_b, l, norm_mix_post[l], xs, tm=512)

        q = _norm_matmul(xs, norm_x_pre[l], xq_b, l, BF16, tm=512, tn=1024)
        k = _norm_matmul(mems, norm_mem[l], xk_b, l, BF16, tm=512, tn=1024)
        v = _norm_matmul(mems, norm_mem[l], xv_b, l, BF16, tm=512, tn=1024)
        xs = _cross_attention(q, k, v, xo_b, l, norm_x_post[l], xs, bsz, seq, tq=512)

        xs = _ffn(xs, norm_ff_pre[l], w_up_b, w_down_b, l, norm_ff_post[l], tm=512, tf=1024)
    return xs.reshape(bsz, seq, d)
```

```python
import functools
import math

import numpy as np
import jax
import jax.numpy as jnp
from jax import lax
from jax.experimental import pallas as pl
from jax.experimental.pallas import tpu as pltpu

F32 = jnp.float32
BF16 = jnp.bfloat16

D_MODEL = 2048
DEPTH = 4
MEM_LEN = 256
A_HEAD_DIM = 128
A_HEADS = 8
A_KV_HEADS = 2
A_GROUP = 4
WINDOW = 128
BLOCK = 128
N_BUCKETS = 32
MAX_DISTANCE = 128
B_HEADS = 4
B_KEY_DIM = 64
B_VAL_DIM = 128
GATE_RANK = 16
GATE_TAU = 16.0
C_WIDTH = 512
C_BLOCKS = 4
C_BLOCK_DIM = 128
CONV_WIDTH = 4
LRU_C = 8.0
X_HEADS = 4
X_HEAD_DIM = 512
D_FF = 8192
EPS = 1e-6
NEG_INF = -1e30

A_Q = A_HEADS * A_HEAD_DIM
A_KV = A_KV_HEADS * A_HEAD_DIM
B_QK = B_HEADS * B_KEY_DIM
B_V = B_HEADS * B_VAL_DIM

LANE = 128
COL_AQ = 0
COL_AK = COL_AQ + A_Q
COL_AV = COL_AK + A_KV
COL_BQ = COL_AV + A_KV
COL_BK = COL_BQ + B_QK
COL_BV = COL_BK + B_QK
COL_BG = COL_BV + B_V
COL_CX = COL_BG + B_V
COL_CY = COL_CX + C_WIDTH
COL_Z = COL_CY + C_WIDTH
D_PROJ = COL_Z + LANE

GLA_BLOCK = 128
GLA_LEVELS = (1, 2, 4, 8, 16, 32, 64)
GLA_PAIRS = B_HEADS // 2
VMEM_LIMIT_MIB = 56


def _cparams(semantics, vmem_mib=VMEM_LIMIT_MIB):
    return pltpu.CompilerParams(dimension_semantics=semantics,
                                vmem_limit_bytes=vmem_mib * 1024 * 1024)


def _dot(a, b):
    return jnp.dot(a, b, preferred_element_type=F32)


def _dot_nt(a, b):
    return lax.dot_general(a, b, (((1,), (1,)), ((), ())), preferred_element_type=F32)


def _dot_tn(a, b):
    return lax.dot_general(a, b, (((0,), (0,)), ((), ())), preferred_element_type=F32)


def _softplus(x):
    return jnp.maximum(x, 0.0) + jnp.log(1.0 + jnp.exp(-jnp.abs(x)))


def _rms_scale(x):
    return lax.rsqrt(jnp.mean(jnp.square(x), axis=-1, keepdims=True) + EPS)


def _norm_matmul_kernel(x_ref, g_ref, w_ref, o_ref, hn_ref):
    @pl.when(pl.program_id(1) == 0)
    def _():
        x = x_ref[...]
        hn_ref[...] = (x * _rms_scale(x) * g_ref[...]).astype(BF16)

    o_ref[...] = _dot(hn_ref[...], w_ref[...]).astype(o_ref.dtype)


def _norm_matmul(x, g, w, layer, out_dtype, tm, tn):
    m, k = x.shape
    n = w.shape[2]
    return pl.pallas_call(
        _norm_matmul_kernel,
        grid=(m // tm, n // tn),
        in_specs=[pl.BlockSpec((tm, k), lambda i, j: (i, 0)),
                  pl.BlockSpec((1, k), lambda i, j: (0, 0)),
                  pl.BlockSpec((None, k, tn), lambda i, j: (layer, 0, j))],
        out_specs=pl.BlockSpec((tm, tn), lambda i, j: (i, j)),
        out_shape=jax.ShapeDtypeStruct((m, n), out_dtype),
        scratch_shapes=[pltpu.VMEM((tm, k), BF16)],
        compiler_params=_cparams(("parallel", "arbitrary")),
        name="norm_matmul",
    )(x, g.reshape(1, k), w)


def _resident(shape, index_map):
    return pl.BlockSpec(shape, index_map, pipeline_mode=pl.Buffered(1))


def _in_proj_kernel(x_ref, g_ref, wm_ref, wc_ref, wz_ref, o_ref):
    x = x_ref[...]
    hn = (x * _rms_scale(x) * g_ref[...]).astype(BF16)
    o_ref[:, 0:COL_CX] = _dot(hn, wm_ref[...])
    o_ref[:, COL_CX:COL_Z] = _dot(hn, wc_ref[...])
    o_ref[:, COL_Z:D_PROJ] = _dot(hn, wz_ref[...])


def _in_proj(x, g, w_main, w_c, w_z, layer, tm):
    m, k = x.shape

    def weight(w):
        return _resident((None, k, w.shape[2]), lambda i: (layer, 0, 0))

    return pl.pallas_call(
        _in_proj_kernel,
        grid=(m // tm,),
        in_specs=[pl.BlockSpec((tm, k), lambda i: (i, 0)), pl.BlockSpec((1, k), lambda i: (0, 0)),
                  weight(w_main), weight(w_c), weight(w_z)],
        out_specs=pl.BlockSpec((tm, D_PROJ), lambda i: (i, 0)),
        out_shape=jax.ShapeDtypeStruct((m, D_PROJ), F32),
        compiler_params=_cparams(("parallel",)),
        name="in_proj",
    )(x, g.reshape(1, k), w_main, w_c, w_z)


def _attn_kernel(sink_ref, q_ref, k_ref, v_ref, bias_ref, o_ref, kb_ref, vt_ref):
    g = pl.program_id(1)
    seq = q_ref.shape[0]
    nblk = seq // BLOCK
    kb_ref[...] = k_ref[...].astype(BF16)
    for n in range(nblk):
        vt_ref[n] = v_ref[n * BLOCK:(n + 1) * BLOCK, :].T.astype(BF16)
    sink = jnp.concatenate(
        [jnp.full((1, BLOCK), sink_ref[g * A_GROUP + h], F32) for h in range(A_GROUP)], axis=1)

    def block(n, first, last):
        t0 = 1 if first else 0
        t1 = 2 if last else 3
        nk = (t1 - t0) * BLOCK
        r0 = pl.multiple_of(n * BLOCK, BLOCK)
        k = kb_ref[pl.ds(pl.multiple_of(r0 + (t0 - 1) * BLOCK, BLOCK), nk), :]
        q = jnp.concatenate([q_ref[pl.ds(r0, BLOCK), h * A_HEAD_DIM:(h + 1) * A_HEAD_DIM]
                             for h in range(A_GROUP)], axis=0).astype(BF16)
        s = _dot_nt(k, q) * (A_HEAD_DIM ** -0.5) + bias_ref[t0 * BLOCK:t1 * BLOCK, :]
        m = jnp.maximum(jnp.max(s, axis=0, keepdims=True), sink)
        p = jnp.exp(s - m).astype(BF16)
        v_t = jnp.concatenate([vt_ref[n + t - 1] for t in range(t0, t1)], axis=1)
        ones = jnp.ones((2 * SUBLANES, nk), BF16)
        o = _dot(jnp.concatenate([v_t, ones], axis=0), p)
        denom = o[A_HEAD_DIM:A_HEAD_DIM + 1, :] + jnp.exp(sink - m)
        o = o[0:A_HEAD_DIM, :] / denom
        for h in range(A_GROUP):
            o_ref[pl.ds(r0, BLOCK), h * A_HEAD_DIM:(h + 1) * A_HEAD_DIM] = (
                o[:, h * BLOCK:(h + 1) * BLOCK].T.astype(o_ref.dtype))

    def body(n, carry):
        block(n, False, False)
        return carry

    block(0, True, False)
    lax.fori_loop(1, nblk - 1, body, 0, unroll=2)
    block(nblk - 1, False, True)


def _windowed_attention(proj, bias_tab, sink, bsz, seq):
    qw = A_GROUP * A_HEAD_DIM
    nblk = seq // BLOCK

    def kv_spec(col0):
        return pl.BlockSpec((seq, A_HEAD_DIM), lambda b, g: (b, col0 // A_HEAD_DIM + g))

    return pl.pallas_call(
        _attn_kernel,
        grid=(bsz, A_KV_HEADS),
        in_specs=[pl.BlockSpec(memory_space=pltpu.SMEM),
                  pl.BlockSpec((seq, qw), lambda b, g: (b, g)),
                  kv_spec(COL_AK), kv_spec(COL_AV),
                  pl.BlockSpec((None, 3 * BLOCK, qw), lambda b, g: (g, 0, 0))],
        out_specs=pl.BlockSpec((seq, qw), lambda b, g: (b, g)),
        out_shape=jax.ShapeDtypeStruct((bsz * seq, A_Q), BF16),
        scratch_shapes=[pltpu.VMEM((seq, A_HEAD_DIM), BF16),
                        pltpu.VMEM((nblk, A_HEAD_DIM, BLOCK), BF16)],
        compiler_params=_cparams(("parallel", "parallel")),
        name="windowed_attention",
    )(sink, proj, proj, proj, bias_tab)


def _t5_bucket(rel):
    nb = N_BUCKETS // 2
    max_exact = nb // 2
    ret = jnp.where(rel > 0, nb, 0)
    n = jnp.abs(rel)
    nf = jnp.maximum(n, 1).astype(F32)
    large = max_exact + (jnp.log(nf / max_exact) / math.log(MAX_DISTANCE / max_exact)
                         * (nb - max_exact)).astype(jnp.int32)
    large = jnp.minimum(large, nb - 1)
    return ret + jnp.where(n < max_exact, n, large)


def _attention_bias_table(rel_bias):
    kj = jnp.arange(3 * BLOCK)[:, None]
    qi = jnp.arange(BLOCK)[None, :]
    rel = kj - BLOCK - qi
    onehot = jax.nn.one_hot(_t5_bucket(rel), N_BUCKETS, dtype=F32)
    bias = jnp.einsum("kqn,nh->hkq", onehot, rel_bias.astype(F32), precision=lax.Precision.HIGHEST)
    bias = jnp.where((jnp.abs(rel) <= WINDOW)[None], bias, NEG_INF)
    bias = bias.reshape(A_KV_HEADS, A_GROUP, 3 * BLOCK, BLOCK)
    return jnp.transpose(bias, (0, 2, 1, 3)).reshape(A_KV_HEADS, 3 * BLOCK, A_GROUP * BLOCK)


def _gla_masks():
    c = GLA_BLOCK
    i = np.arange(c)[:, None]
    j = np.arange(c)[None, :]
    fwd = [i == j]
    for s in GLA_LEVELS:
        fwd.append((i // (2 * s) == j // (2 * s)) & ((i // s) % 2 == 1) & ((j // s) % 2 == 0))
    fwd = np.stack(fwd)
    return jnp.asarray(np.stack([fwd, fwd.transpose(0, 2, 1)]).astype(np.float32))


def _gla_block(direction, rows, q_ref, k_ref, v_ref, z_ref, w2_ref, b2_ref, m_ref, st_ref):
    c = GLA_BLOCK
    q = q_ref[rows, :] * (B_KEY_DIM ** -0.5)
    k = k_ref[rows, :]
    v = v_ref[rows, :].astype(BF16)
    gate_cols = slice(direction * B_QK, (direction + 1) * B_QK)
    pre = _dot(z_ref[rows, :].astype(BF16), w2_ref[:, gate_cols]) + b2_ref[:, gate_cols]
    log_a = -_softplus(-pre) * (1.0 / GATE_TAU)

    row = lax.broadcasted_iota(jnp.int32, (c, B_QK), 0)
    first_head = lax.broadcasted_iota(jnp.int32, (c, LANE), 1) < B_KEY_DIM
    pair = [slice(p * LANE, (p + 1) * LANE) for p in range(GLA_PAIRS)]
    k_first = [jnp.where(first_head, k[:, ps], 0.0) for ps in pair]
    k_second = [jnp.where(first_head, 0.0, k[:, ps]) for ps in pair]

    def exponents(p_s, t_s):
        if direction == 0:
            return p_s, t_s - p_s
        return t_s - p_s + log_a, p_s - log_a

    def scaled(e_q, e_k, p):
        w_q = jnp.exp(e_q[:, pair[p]])
        w_k = jnp.exp(e_k[:, pair[p]])
        q_t = (q[:, pair[p]] * w_q).astype(BF16)
        k_t = jnp.concatenate([(k_first[p] * w_k).astype(BF16), (k_second[p] * w_k).astype(BF16)], axis=0)
        return q_t, k_t

    attn = [None] * GLA_PAIRS

    def add_level(level, q_t, k_t, p):
        mask = m_ref[direction, level]
        contrib = jnp.concatenate([mask, mask], axis=1) * _dot_nt(q_t, k_t)
        attn[p] = contrib if attn[p] is None else attn[p] + contrib

    if direction == 0:
        for p in range(GLA_PAIRS):
            k_t = jnp.concatenate([k_first[p].astype(BF16), k_second[p].astype(BF16)], axis=0)
            add_level(0, q[:, pair[p]].astype(BF16), k_t, p)
    p_s = log_a
    t_s = log_a
    for li, s in enumerate(GLA_LEVELS):
        e_q, e_k = exponents(p_s, t_s)
        for p in range(GLA_PAIRS):
            q_t, k_t = scaled(e_q, e_k, p)
            add_level(li + 1, q_t, k_t, p)
        upper = (row & s) != 0
        t_before = pltpu.roll(t_s, s, 0)
        t_after = pltpu.roll(t_s, c - s, 0)
        p_s = p_s + jnp.where(upper, t_before, 0.0)
        t_s = t_s + jnp.where(upper, t_before, t_after)

    e_q, e_k = exponents(p_s, t_s)
    state_decay = jnp.exp(t_s[0:1, :])
    own_block = lax.broadcasted_iota(jnp.int32, (2 * B_VAL_DIM, LANE), 0) < B_VAL_DIM
    own_block = own_block == (lax.broadcasted_iota(jnp.int32, (2 * B_VAL_DIM, LANE), 1) < B_KEY_DIM)
    zero_v = jnp.zeros((c, B_VAL_DIM), BF16)
    outs = []
    for p in range(GLA_PAIRS):
        v_pair = v[:, 2 * p * B_VAL_DIM:(2 * p + 2) * B_VAL_DIM]
        v_diag = jnp.concatenate(
            [jnp.concatenate([v_pair[:, :B_VAL_DIM], zero_v], axis=1),
             jnp.concatenate([zero_v, v_pair[:, B_VAL_DIM:]], axis=1)], axis=0)
        state_t = st_ref[direction, p]
        q_t = (q[:, pair[p]] * jnp.exp(e_q[:, pair[p]])).astype(BF16)
        k_t = (k[:, pair[p]] * jnp.exp(e_k[:, pair[p]])).astype(BF16)
        o = _dot_nt(q_t, state_t.astype(BF16)) + _dot(attn[p].astype(BF16), v_diag)
        update = jnp.where(own_block, _dot_tn(v_pair, k_t), 0.0)
        st_ref[direction, p] = state_t * state_decay[:, pair[p]] + update
        outs.append(o)
    return jnp.concatenate(outs, axis=1)


def _gla_kernel(q_ref, k_ref, v_ref, g_ref, z_ref, w2_ref, b2_ref, gn_ref, m_ref,
                o_ref, of_ref, st_ref):
    nblk = q_ref.shape[0] // GLA_BLOCK
    st_ref[...] = jnp.zeros(st_ref.shape, F32)
    block = functools.partial(_gla_block, q_ref=q_ref, k_ref=k_ref, v_ref=v_ref, z_ref=z_ref,
                              w2_ref=w2_ref, b2_ref=b2_ref, m_ref=m_ref, st_ref=st_ref)

    def forward(n, carry):
        rows = pl.ds(pl.multiple_of(n * GLA_BLOCK, GLA_BLOCK), GLA_BLOCK)
        of_ref[rows, :] = block(0, rows)
        return carry

    lax.fori_loop(0, nblk, forward, 0)

    def backward(step, carry):
        n = nblk - 1 - step
        rows = pl.ds(pl.multiple_of(n * GLA_BLOCK, GLA_BLOCK), GLA_BLOCK)
        o = of_ref[rows, :] + block(1, rows)
        normed = []
        for h in range(B_HEADS):
            oh = o[:, h * B_VAL_DIM:(h + 1) * B_VAL_DIM]
            normed.append(oh * _rms_scale(oh))
        g = g_ref[rows, :]
        out = jnp.concatenate(normed, axis=1) * gn_ref[...] * (g * jax.nn.sigmoid(g))
        o_ref[rows, :] = out.astype(o_ref.dtype)
        return carry

    lax.fori_loop(0, nblk, backward, 0)


def _gla(proj, w2, layer, b2, gn, masks, bsz, seq):
    def col(width, start):
        return pl.BlockSpec((seq, width), lambda b: (b, start // width))

    def whole(a):
        return pl.BlockSpec(a.shape, lambda b: (0,) * a.ndim)

    return pl.pallas_call(
        _gla_kernel,
        grid=(bsz,),
        in_specs=[col(B_QK, COL_BQ), col(B_QK, COL_BK), col(B_V, COL_BV), col(B_V, COL_BG),
                  col(LANE, COL_Z), pl.BlockSpec((None,) + w2.shape[1:], lambda b: (layer, 0, 0)),
                  whole(b2), whole(gn), whole(masks)],
        out_specs=pl.BlockSpec((seq, B_V), lambda b: (b, 0)),
        out_shape=jax.ShapeDtypeStruct((bsz * seq, B_V), BF16),
        scratch_shapes=[pltpu.VMEM((seq, B_V), F32),
                        pltpu.VMEM((2, GLA_PAIRS, 2 * B_VAL_DIM, LANE), F32)],
        compiler_params=_cparams(("parallel",)),
        name="gla",
    )(proj, proj, proj, proj, proj, w2, b2, gn, masks)


LRU_CHUNK = 256
SUBLANES = 8


def _lru_kernel(x_ref, y_ref, cw_ref, cb_ref, wa_ref, wx_ref, ba_ref, bx_ref, lam_ref, o_ref,
                xp_ref, af_ref, hf_ref, ab_ref, hb_ref):
    seq = x_ref.shape[0]
    ch = x_ref.shape[1]
    nchunk = seq // LRU_CHUNK
    groups = LRU_CHUNK // SUBLANES
    halo = SUBLANES
    xp_ref[0:halo, :] = jnp.zeros((halo, ch), F32)
    xp_ref[seq + halo:seq + 2 * halo, :] = jnp.zeros((halo, ch), F32)
    xp_ref[halo:seq + halo, :] = x_ref[...]
    decay_rate = _softplus(-lam_ref[...])
    sub = lax.broadcasted_iota(jnp.int32, (groups, SUBLANES, ch), 1)
    scan_refs = ((af_ref, hf_ref), (ab_ref, hb_ref))

    def gates_and_local_scan(c, carry):
        r0 = pl.multiple_of(c * LRU_CHUNK, LRU_CHUNK)
        ext = LRU_CHUNK + 2 * halo
        xe = xp_ref[pl.ds(r0, ext), :]
        xc = (cw_ref[0:1, :] * pltpu.roll(xe, 2, 0) + cw_ref[1:2, :] * pltpu.roll(xe, 1, 0)
              + cw_ref[2:3, :] * xe + cw_ref[3:4, :] * pltpu.roll(xe, ext - 1, 0))
        xc = xc[halo:halo + LRU_CHUNK] + cb_ref[...]
        xb = xc.astype(BF16)
        for d in range(2):
            r = jax.nn.sigmoid(_dot(xb, wa_ref[d, 0]) + ba_ref[d:d + 1, :])
            i = jax.nn.sigmoid(_dot(xb, wx_ref[d, 0]) + bx_ref[d:d + 1, :])
            log_a = -LRU_C * r * decay_rate[d:d + 1, :]
            a = jnp.exp(log_a).reshape(groups, SUBLANES, ch)
            t = jnp.tanh(log_a)
            u = (jnp.sqrt(-2.0 * t / (1.0 - t)) * (i * xc)).reshape(groups, SUBLANES, ch)
            for s in (1, 2, 4):
                shift = s if d == 0 else SUBLANES - s
                keep = (sub >= s) if d == 0 else (sub < SUBLANES - s)
                a_sh = pltpu.roll(a, shift, 1)
                u_sh = pltpu.roll(u, shift, 1)
                u = jnp.where(keep, a * u_sh + u, u)
                a = jnp.where(keep, a * a_sh, a)
            a_ref, h_ref = scan_refs[d]
            a_ref[pl.ds(r0, LRU_CHUNK), :] = a.reshape(LRU_CHUNK, ch)
            h_ref[pl.ds(r0, LRU_CHUNK), :] = u.reshape(LRU_CHUNK, ch)
        return carry

    lax.fori_loop(0, nchunk, gates_and_local_scan, 0)

    ngroups = seq // SUBLANES

    def carry_groups(gi, carry):
        cf, cb = carry
        rf = pl.ds(pl.multiple_of(gi * SUBLANES, SUBLANES), SUBLANES)
        hf = hf_ref[rf, :] + af_ref[rf, :] * cf
        hf_ref[rf, :] = hf
        rb = pl.ds(pl.multiple_of((ngroups - 1 - gi) * SUBLANES, SUBLANES), SUBLANES)
        hb = hb_ref[rb, :] + ab_ref[rb, :] * cb
        hb_ref[rb, :] = hb
        return hf[SUBLANES - 1:SUBLANES, :], hb[0:1, :]

    zero = jnp.zeros((1, ch), F32)
    lax.fori_loop(0, ngroups, carry_groups, (zero, zero), unroll=8)

    def gate_out(c, carry):
        rows = pl.ds(pl.multiple_of(c * LRU_CHUNK, LRU_CHUNK), LRU_CHUNK)
        h = hf_ref[rows, :] + hb_ref[rows, :]
        o_ref[rows, :] = (h * jax.nn.gelu(y_ref[rows, :])).astype(o_ref.dtype)
        return carry

    lax.fori_loop(0, nchunk, gate_out, 0)


def _rglru(proj, conv_w, conv_b, w_a, w_x, layer, b_a, b_x, lam, bsz, seq):
    ch = C_BLOCK_DIM

    def col(start):
        return pl.BlockSpec((seq, ch), lambda b, c: (b, start // ch + c))

    def per_block(rows):
        return pl.BlockSpec((rows, ch), lambda b, c: (0, c))

    gate_w = pl.BlockSpec((None, 2, 1, ch, ch), lambda b, c: (layer, 0, c, 0, 0))
    return pl.pallas_call(
        _lru_kernel,
        grid=(bsz, C_BLOCKS),
        in_specs=[col(COL_CX), col(COL_CY), per_block(CONV_WIDTH), per_block(1), gate_w, gate_w,
                  per_block(2), per_block(2), per_block(2)],
        out_specs=pl.BlockSpec((seq, ch), lambda b, c: (b, c)),
        out_shape=jax.ShapeDtypeStruct((bsz * seq, C_WIDTH), BF16),
        scratch_shapes=[pltpu.VMEM((seq + 2 * SUBLANES, ch), F32)] + [pltpu.VMEM((seq, ch), F32)] * 4,
        compiler_params=_cparams(("parallel", "parallel")),
        name="rglru",
    )(proj, proj, conv_w, conv_b.reshape(1, C_WIDTH), w_a, w_x, b_a, b_x, lam)


def _out_proj_kernel(oa_ref, ob_ref, oc_ref, w_ref, g_ref, x_ref, o_ref):
    mixed = (_dot(oa_ref[...], w_ref[0:A_Q, :]) + _dot(ob_ref[...], w_ref[A_Q:A_Q + B_V, :])
             + _dot(oc_ref[...], w_ref[A_Q + B_V:A_Q + B_V + C_WIDTH, :]))
    o_ref[...] = x_ref[...] + mixed * _rms_scale(mixed) * g_ref[...]


def _out_proj(oa, ob, oc, w, layer, g, x, tm):
    m, d = x.shape

    def rows(width):
        return pl.BlockSpec((tm, width), lambda i: (i, 0))

    return pl.pallas_call(
        _out_proj_kernel,
        grid=(m // tm,),
        in_specs=[rows(A_Q), rows(B_V), rows(C_WIDTH),
                  pl.BlockSpec((None,) + w.shape[1:], lambda i: (layer, 0, 0)),
                  pl.BlockSpec((1, d), lambda i: (0, 0)),
                  rows(d)],
        out_specs=rows(d),
        out_shape=jax.ShapeDtypeStruct((m, d), F32),
        compiler_params=_cparams(("parallel",)),
        name="out_proj",
    )(oa, ob, oc, w, g.reshape(1, d), x)


def _cross_attn_kernel(x_ref, gpre_ref, wq_ref, k_ref, v_ref, wo_ref, gpost_ref, o_ref):
    x = x_ref[...]
    hn = (x * _rms_scale(x) * gpre_ref[...]).astype(BF16)
    q = _dot(hn, wq_ref[...]).astype(BF16)
    acc = None
    for h in range(X_HEADS):
        hs = slice(h * X_HEAD_DIM, (h + 1) * X_HEAD_DIM)
        s = _dot_nt(q[:, hs], k_ref[:, hs]) * (X_HEAD_DIM ** -0.5)
        m = jnp.max(s, axis=-1, keepdims=True)
        p = jnp.exp(s - m)
        o = _dot(p.astype(BF16), v_ref[:, hs]) / jnp.sum(p, axis=-1, keepdims=True)
        part = _dot(o.astype(BF16), wo_ref[hs, :])
        acc = part if acc is None else acc + part
    o_ref[...] = x + acc * _rms_scale(acc) * gpost_ref[...]


def _cross_attention(x, gpre, wq, k, v, wo, layer, gpost, bsz, seq, tq):
    d = D_MODEL
    per_row = seq // tq
    mem_spec = pl.BlockSpec((MEM_LEN, d), lambda b, i: (b, 0))
    gain_spec = pl.BlockSpec((1, d), lambda b, i: (0, 0))
    weight_spec = _resident((None, d, d), lambda b, i: (layer, 0, 0))
    row_spec = pl.BlockSpec((tq, d), lambda b, i: (b * per_row + i, 0))
    return pl.pallas_call(
        _cross_attn_kernel,
        grid=(bsz, per_row),
        in_specs=[row_spec, gain_spec, weight_spec, mem_spec, mem_spec, weight_spec, gain_spec],
        out_specs=row_spec,
        out_shape=jax.ShapeDtypeStruct((bsz * seq, d), F32),
        compiler_params=_cparams(("parallel", "parallel")),
        name="cross_attention",
    )(x, gpre.reshape(1, d), wq, k, v, wo, gpost.reshape(1, d))


def _ffn_kernel(x_ref, gpre_ref, wu_ref, wd_ref, gpost_ref, o_ref, hn_ref):
    f = pl.program_id(1)

    @pl.when(f == 0)
    def _():
        x = x_ref[...]
        hn_ref[...] = (x * _rms_scale(x) * gpre_ref[...]).astype(BF16)
        o_ref[...] = jnp.zeros(o_ref.shape, F32)

    up = _dot(hn_ref[...], wu_ref[...])
    act = jnp.square(jnp.maximum(up, 0.0)).astype(BF16)
    o_ref[...] += _dot(act, wd_ref[...])

    @pl.when(f == pl.num_programs(1) - 1)
    def _():
        ff = o_ref[...]
        o_ref[...] = x_ref[...] + ff * _rms_scale(ff) * gpost_ref[...]


def _ffn(x, gpre, w_up, w_down, layer, gpost, tm, tf):
    m, d = x.shape
    dff = w_up.shape[2]
    return pl.pallas_call(
        _ffn_kernel,
        grid=(m // tm, dff // tf),
        in_specs=[pl.BlockSpec((tm, d), lambda i, f: (i, 0)),
                  pl.BlockSpec((1, d), lambda i, f: (0, 0)),
                  pl.BlockSpec((None, d, tf), lambda i, f: (layer, 0, f)),
                  pl.BlockSpec((None, tf, d), lambda i, f: (layer, f, 0)),
                  pl.BlockSpec((1, d), lambda i, f: (0, 0))],
        out_specs=pl.BlockSpec((tm, d), lambda i, f: (i, 0)),
        out_shape=jax.ShapeDtypeStruct((m, d), F32),
        scratch_shapes=[pltpu.VMEM((tm, d), BF16)],
        compiler_params=_cparams(("parallel", "arbitrary")),
        name="ffn",
    )(x, gpre.reshape(1, d), w_up, w_down, gpost.reshape(1, d))


def _split_w_in(w):
    z0 = A_Q + 2 * A_KV + 2 * B_QK + 2 * B_V
    z1 = z0 + 2 * GATE_RANK
    gates = jnp.pad(w[..., z0:z1].astype(BF16), ((0, 0), (0, 0), (0, LANE - 2 * GATE_RANK)))
    return w[..., :z0].astype(BF16), w[..., z1:].astype(BF16), gates


def _gate_expansion(w2_f, w2_b):
    w = jnp.zeros((w2_f.shape[0], LANE, 2 * B_QK), F32)
    w = w.at[:, 0:GATE_RANK, 0:B_QK].set(w2_f)
    w = w.at[:, GATE_RANK:2 * GATE_RANK, B_QK:].set(w2_b)
    return w.astype(BF16)


def kernel(x, mem, rel_bias, w_in, w_out, attn_sink, gla_w2_f, gla_b2_f, gla_w2_b, gla_b2_b, gla_norm, conv_w, conv_b, lru_wa, lru_ba, lru_wx, lru_bx, lru_lambda, xq, xk, xv, xo, w_up, w_down, norm_mix_pre, norm_mix_post, norm_mem, norm_x_pre, norm_x_post, norm_ff_pre, norm_ff_post):
    bsz, seq, d = x.shape
    assert d == D_MODEL and seq % LRU_CHUNK == 0 and seq % 512 == 0
    xs = x.reshape(bsz * seq, d)
    mems = mem.reshape(bsz * mem.shape[1], d)
    bias_tab = _attention_bias_table(rel_bias)
    masks = _gla_masks()
    w_in_main, w_in_c, w_in_z = _split_w_in(w_in)
    w2_b = _gate_expansion(gla_w2_f, gla_w2_b)
    b2 = jnp.concatenate([gla_b2_f, gla_b2_b], axis=-1)
    bf = lambda w: w.astype(BF16)
    w_out_b, wa_b, wx_b = bf(w_out), bf(lru_wa), bf(lru_wx)
    xq_b, xk_b, xv_b, xo_b = bf(xq), bf(xk), bf(xv), bf(xo)
    w_up_b, w_down_b = bf(w_up), bf(w_down)
    for l in range(DEPTH):
        proj = _in_proj(xs, norm_mix_pre[l], w_in_main, w_in_c, w_in_z, l, tm=512)
        oa = _windowed_attention(proj, bias_tab, attn_sink[l], bsz, seq)
        ob = _gla(proj, w2_b, l, b2[l].reshape(1, 2 * B_QK), gla_norm[l].reshape(1, B_V), masks, bsz, seq)
        oc = _rglru(proj, conv_w[l], conv_b[l], wa_b, wx_b, l, lru_ba[l], lru_bx[l], lru_lambda[l],
                    bsz, seq)
        xs = _out_proj(oa, ob, oc, w_out_b, l, norm_mix_post[l], xs, tm=512)

        k = _norm_matmul(mems, norm_mem[l], xk_b, l, BF16, tm=512, tn=1024)
        v = _norm_matmul(mems, norm_mem[l], xv_b, l, BF16, tm=512, tn=1024)
        xs = _cross_attention(xs, norm_x_pre[l], xq_b, k, v, xo_b, l, norm_x_post[l], bsz, seq, tq=512)

        xs = _ffn(xs, norm_ff_pre[l], w_up_b, w_down_b, l, norm_ff_post[l], tm=1024, tf=512)
    return xs.reshape(bsz, seq, d)
```

```python
import functools
import math

import numpy as np
import jax
import jax.numpy as jnp
from jax import lax
from jax.experimental import pallas as pl
from jax.experimental.pallas import tpu as pltpu

F32 = jnp.float32
BF16 = jnp.bfloat16

D_MODEL = 2048
DEPTH = 4
MEM_LEN = 256
A_HEAD_DIM = 128
A_HEADS = 8
A_KV_HEADS = 2
A_GROUP = 4
WINDOW = 128
BLOCK = 128
N_BUCKETS = 32
MAX_DISTANCE = 128
B_HEADS = 4
B_KEY_DIM = 64
B_VAL_DIM = 128
GATE_RANK = 16
GATE_TAU = 16.0
C_WIDTH = 512
C_BLOCKS = 4
C_BLOCK_DIM = 128
CONV_WIDTH = 4
LRU_C = 8.0
X_HEADS = 4
X_HEAD_DIM = 512
D_FF = 8192
EPS = 1e-6
NEG_INF = -1e30
LOG2E = math.log2(math.e)

A_Q = A_HEADS * A_HEAD_DIM
A_KV = A_KV_HEADS * A_HEAD_DIM
B_QK = B_HEADS * B_KEY_DIM
B_V = B_HEADS * B_VAL_DIM

LANE = 128
COL_AQ = 0
COL_AK = COL_AQ + A_Q
COL_AV = COL_AK + A_KV
COL_BQ = COL_AV + A_KV
COL_BK = COL_BQ + B_QK
COL_BV = COL_BK + B_QK
COL_BG = COL_BV + B_V
COL_CX = COL_BG + B_V
COL_CY = COL_CX + C_WIDTH
COL_Z = COL_CY + C_WIDTH
D_PROJ = COL_Z + LANE

GLA_BLOCK = 128
GLA_LEVELS = (1, 2, 4, 8, 16, 32, 64)
GLA_PAIRS = B_HEADS // 2
VMEM_LIMIT_MIB = 56


def _cparams(semantics, vmem_mib=VMEM_LIMIT_MIB):
    return pltpu.CompilerParams(dimension_semantics=semantics,
                                vmem_limit_bytes=vmem_mib * 1024 * 1024)


def _dot(a, b):
    return jnp.dot(a, b, preferred_element_type=F32)


def _dot_nt(a, b):
    return lax.dot_general(a, b, (((1,), (1,)), ((), ())), preferred_element_type=F32)


def _dot_tn(a, b):
    return lax.dot_general(a, b, (((0,), (0,)), ((), ())), preferred_element_type=F32)


def _softplus(x):
    return jnp.maximum(x, 0.0) + jnp.log(1.0 + jnp.exp(-jnp.abs(x)))


def _rms_scale(x):
    return lax.rsqrt(jnp.mean(jnp.square(x), axis=-1, keepdims=True) + EPS)


def _norm_matmul_kernel(x_ref, g_ref, w_ref, o_ref, hn_ref):
    @pl.when(pl.program_id(1) == 0)
    def _():
        x = x_ref[...]
        hn_ref[...] = (x * _rms_scale(x) * g_ref[...]).astype(BF16)

    o_ref[...] = _dot(hn_ref[...], w_ref[...]).astype(o_ref.dtype)


def _norm_matmul(x, g, w, layer, out_dtype, tm, tn):
    m, k = x.shape
    n = w.shape[2]
    return pl.pallas_call(
        _norm_matmul_kernel,
        grid=(m // tm, n // tn),
        in_specs=[pl.BlockSpec((tm, k), lambda i, j: (i, 0)),
                  pl.BlockSpec((1, k), lambda i, j: (0, 0)),
                  pl.BlockSpec((None, k, tn), lambda i, j: (layer, 0, j))],
        out_specs=pl.BlockSpec((tm, tn), lambda i, j: (i, j)),
        out_shape=jax.ShapeDtypeStruct((m, n), out_dtype),
        scratch_shapes=[pltpu.VMEM((tm, k), BF16)],
        compiler_params=_cparams(("parallel", "arbitrary")),
        name="norm_matmul",
    )(x, g.reshape(1, k), w)


def _resident(shape, index_map):
    return pl.BlockSpec(shape, index_map, pipeline_mode=pl.Buffered(1))


def _in_proj_kernel(x_ref, g_ref, wm_ref, wc_ref, wz_ref, o_ref):
    x = x_ref[...]
    hn = (x * _rms_scale(x) * g_ref[...]).astype(BF16)
    o_ref[:, 0:COL_CX] = _dot_nt(hn, wm_ref[...])
    o_ref[:, COL_CX:COL_Z] = _dot_nt(hn, wc_ref[...])
    o_ref[:, COL_Z:D_PROJ] = _dot_nt(hn, wz_ref[...])


def _in_proj(x, g, w_main, w_c, w_z, layer, tm):
    m, k = x.shape

    def weight(w):
        return _resident((None, w.shape[1], k), lambda i: (layer, 0, 0))

    return pl.pallas_call(
        _in_proj_kernel,
        grid=(m // tm,),
        in_specs=[pl.BlockSpec((tm, k), lambda i: (i, 0)), pl.BlockSpec((1, k), lambda i: (0, 0)),
                  weight(w_main), weight(w_c), weight(w_z)],
        out_specs=pl.BlockSpec((tm, D_PROJ), lambda i: (i, 0)),
        out_shape=jax.ShapeDtypeStruct((m, D_PROJ), F32),
        compiler_params=_cparams(("parallel",)),
        name="in_proj",
    )(x, g.reshape(1, k), w_main, w_c, w_z)


def _attn_kernel(sink_ref, q_ref, k_ref, v_ref, bias_ref, o_ref, kb_ref, vt_ref, p_ref, sw_ref):
    g = pl.program_id(1)
    seq = q_ref.shape[0]
    nblk = seq // BLOCK
    kb_ref[...] = k_ref[...].astype(BF16)
    for n in range(nblk):
        vt_ref[n] = v_ref[n * BLOCK:(n + 1) * BLOCK, :].T.astype(BF16)
    sink = jnp.concatenate(
        [jnp.full((1, BLOCK), sink_ref[g * A_GROUP + h] * LOG2E, F32) for h in range(A_GROUP)], axis=1)

    def tiles(first, last):
        return (1 if first else 0), (2 if last else 3)

    def scores(n, first, last):
        t0, t1 = tiles(first, last)
        nk = (t1 - t0) * BLOCK
        r0 = pl.multiple_of(n * BLOCK, BLOCK)
        k = kb_ref[pl.ds(pl.multiple_of(r0 + (t0 - 1) * BLOCK, BLOCK), nk), :]
        q = jnp.concatenate([q_ref[pl.ds(r0, BLOCK), h * A_HEAD_DIM:(h + 1) * A_HEAD_DIM]
                             for h in range(A_GROUP)], axis=0).astype(BF16)
        s = _dot_nt(k, q) * (A_HEAD_DIM ** -0.5 * LOG2E) + bias_ref[t0 * BLOCK:t1 * BLOCK, :]
        m = jnp.maximum(jnp.max(s, axis=0, keepdims=True), sink)
        p_ref[0:nk, :] = jnp.exp2(s - m).astype(BF16)
        sw_ref[...] = jnp.exp2(sink - m)

    def output(n, first, last):
        t0, t1 = tiles(first, last)
        nk = (t1 - t0) * BLOCK
        r0 = pl.multiple_of(n * BLOCK, BLOCK)
        v_t = jnp.concatenate([vt_ref[n + t - 1] for t in range(t0, t1)], axis=1)
        ones = jnp.ones((2 * SUBLANES, nk), BF16)
        o = _dot(jnp.concatenate([v_t, ones], axis=0), p_ref[0:nk, :])
        denom = o[A_HEAD_DIM:A_HEAD_DIM + 1, :] + sw_ref[...]
        o = o[0:A_HEAD_DIM, :] / denom
        for h in range(A_GROUP):
            o_ref[pl.ds(r0, BLOCK), h * A_HEAD_DIM:(h + 1) * A_HEAD_DIM] = (
                o[:, h * BLOCK:(h + 1) * BLOCK].T.astype(o_ref.dtype))

    def body(n, carry):
        output(n, False, False)
        scores(n + 1, False, False)
        return carry

    scores(0, True, False)
    output(0, True, False)
    scores(1, False, False)
    lax.fori_loop(1, nblk - 2, body, 0)
    output(nblk - 2, False, False)
    scores(nblk - 1, False, True)
    output(nblk - 1, False, True)


def _windowed_attention(proj, bias_tab, sink, bsz, seq):
    qw = A_GROUP * A_HEAD_DIM
    nblk = seq // BLOCK

    def kv_spec(col0):
        return pl.BlockSpec((seq, A_HEAD_DIM), lambda b, g: (b, col0 // A_HEAD_DIM + g))

    return pl.pallas_call(
        _attn_kernel,
        grid=(bsz, A_KV_HEADS),
        in_specs=[pl.BlockSpec(memory_space=pltpu.SMEM),
                  pl.BlockSpec((seq, qw), lambda b, g: (b, g)),
                  kv_spec(COL_AK), kv_spec(COL_AV),
                  pl.BlockSpec((None, 3 * BLOCK, qw), lambda b, g: (g, 0, 0))],
        out_specs=pl.BlockSpec((seq, qw), lambda b, g: (b, g)),
        out_shape=jax.ShapeDtypeStruct((bsz * seq, A_Q), BF16),
        scratch_shapes=[pltpu.VMEM((seq, A_HEAD_DIM), BF16),
                        pltpu.VMEM((nblk, A_HEAD_DIM, BLOCK), BF16),
                        pltpu.VMEM((3 * BLOCK, qw), BF16),
                        pltpu.VMEM((1, qw), F32)],
        compiler_params=_cparams(("parallel", "parallel")),
        name="windowed_attention",
    )(sink, proj, proj, proj, bias_tab)


def _t5_bucket(rel):
    nb = N_BUCKETS // 2
    max_exact = nb // 2
    ret = jnp.where(rel > 0, nb, 0)
    n = jnp.abs(rel)
    nf = jnp.maximum(n, 1).astype(F32)
    large = max_exact + (jnp.log(nf / max_exact) / math.log(MAX_DISTANCE / max_exact)
                         * (nb - max_exact)).astype(jnp.int32)
    large = jnp.minimum(large, nb - 1)
    return ret + jnp.where(n < max_exact, n, large)


def _attention_bias_table(rel_bias):
    kj = jnp.arange(3 * BLOCK)[:, None]
    qi = jnp.arange(BLOCK)[None, :]
    rel = kj - BLOCK - qi
    onehot = jax.nn.one_hot(_t5_bucket(rel), N_BUCKETS, dtype=F32)
    bias = jnp.einsum("kqn,nh->hkq", onehot, rel_bias.astype(F32), precision=lax.Precision.HIGHEST)
    bias = jnp.where((jnp.abs(rel) <= WINDOW)[None], bias * LOG2E, NEG_INF)
    bias = bias.reshape(A_KV_HEADS, A_GROUP, 3 * BLOCK, BLOCK)
    return jnp.transpose(bias, (0, 2, 1, 3)).reshape(A_KV_HEADS, 3 * BLOCK, A_GROUP * BLOCK)


def _gla_masks():
    c = GLA_BLOCK
    i = np.arange(c)[:, None]
    j = np.arange(c)[None, :]
    fwd = [i == j]
    for s in GLA_LEVELS:
        fwd.append((i // (2 * s) == j // (2 * s)) & ((i // s) % 2 == 1) & ((j // s) % 2 == 0))
    fwd = np.stack(fwd)
    return jnp.asarray(np.stack([fwd, fwd.transpose(0, 2, 1)]).astype(np.float32))


def _gla_block(direction, rows, q_ref, k_ref, v_ref, z_ref, w2_ref, b2_ref, m_ref, st_ref):
    c = GLA_BLOCK
    q = q_ref[rows, :] * (B_KEY_DIM ** -0.5)
    k = k_ref[rows, :]
    v = v_ref[rows, :].astype(BF16)
    gate_cols = slice(direction * B_QK, (direction + 1) * B_QK)
    pre = _dot(z_ref[rows, :].astype(BF16), w2_ref[:, gate_cols]) + b2_ref[:, gate_cols]
    log_a = -_softplus(-pre) * (LOG2E / GATE_TAU)

    row = lax.broadcasted_iota(jnp.int32, (c, B_QK), 0)
    first_head = lax.broadcasted_iota(jnp.int32, (c, LANE), 1) < B_KEY_DIM
    pair = [slice(p * LANE, (p + 1) * LANE) for p in range(GLA_PAIRS)]
    k_first = [jnp.where(first_head, k[:, ps], 0.0) for ps in pair]
    k_second = [jnp.where(first_head, 0.0, k[:, ps]) for ps in pair]

    def exponents(p_s, t_s):
        if direction == 0:
            return p_s, t_s - p_s
        return t_s - p_s + log_a, p_s - log_a

    def scaled(e_q, e_k, p):
        w_q = jnp.exp2(e_q[:, pair[p]])
        w_k = jnp.exp2(e_k[:, pair[p]])
        q_t = (q[:, pair[p]] * w_q).astype(BF16)
        k_t = jnp.concatenate([(k_first[p] * w_k).astype(BF16), (k_second[p] * w_k).astype(BF16)], axis=0)
        return q_t, k_t

    attn = [None] * GLA_PAIRS

    def add_level(level, q_t, k_t, p):
        mask = m_ref[direction, level]
        contrib = jnp.concatenate([mask, mask], axis=1) * _dot_nt(q_t, k_t)
        attn[p] = contrib if attn[p] is None else attn[p] + contrib

    if direction == 0:
        for p in range(GLA_PAIRS):
            k_t = jnp.concatenate([k_first[p].astype(BF16), k_second[p].astype(BF16)], axis=0)
            add_level(0, q[:, pair[p]].astype(BF16), k_t, p)
    p_s = log_a
    t_s = log_a
    for li, s in enumerate(GLA_LEVELS):
        e_q, e_k = exponents(p_s, t_s)
        for p in range(GLA_PAIRS):
            q_t, k_t = scaled(e_q, e_k, p)
            add_level(li + 1, q_t, k_t, p)
        upper = (row & s) != 0
        t_before = pltpu.roll(t_s, s, 0)
        t_after = pltpu.roll(t_s, c - s, 0)
        p_s = p_s + jnp.where(upper, t_before, 0.0)
        t_s = t_s + jnp.where(upper, t_before, t_after)

    e_q, e_k = exponents(p_s, t_s)
    state_decay = jnp.exp2(t_s[0:1, :])
    own_block = lax.broadcasted_iota(jnp.int32, (2 * B_VAL_DIM, LANE), 0) < B_VAL_DIM
    own_block = own_block == (lax.broadcasted_iota(jnp.int32, (2 * B_VAL_DIM, LANE), 1) < B_KEY_DIM)
    zero_v = jnp.zeros((c, B_VAL_DIM), BF16)
    outs = []
    for p in range(GLA_PAIRS):
        v_pair = v[:, 2 * p * B_VAL_DIM:(2 * p + 2) * B_VAL_DIM]
        v_diag = jnp.concatenate(
            [jnp.concatenate([v_pair[:, :B_VAL_DIM], zero_v], axis=1),
             jnp.concatenate([zero_v, v_pair[:, B_VAL_DIM:]], axis=1)], axis=0)
        state_t = st_ref[direction, p]
        q_t = (q[:, pair[p]] * jnp.exp2(e_q[:, pair[p]])).astype(BF16)
        k_t = (k[:, pair[p]] * jnp.exp2(e_k[:, pair[p]])).astype(BF16)
        o = _dot_nt(q_t, state_t.astype(BF16)) + _dot(attn[p].astype(BF16), v_diag)
        update = jnp.where(own_block, _dot_tn(v_pair, k_t), 0.0)
        st_ref[direction, p] = state_t * state_decay[:, pair[p]] + update
        outs.append(o)
    return jnp.concatenate(outs, axis=1)


def _gla_kernel(q_ref, k_ref, v_ref, g_ref, z_ref, w2_ref, b2_ref, gn_ref, m_ref,
                o_ref, of_ref, st_ref):
    nblk = q_ref.shape[0] // GLA_BLOCK
    st_ref[...] = jnp.zeros(st_ref.shape, F32)
    block = functools.partial(_gla_block, q_ref=q_ref, k_ref=k_ref, v_ref=v_ref, z_ref=z_ref,
                              w2_ref=w2_ref, b2_ref=b2_ref, m_ref=m_ref, st_ref=st_ref)

    def forward(n, carry):
        rows = pl.ds(pl.multiple_of(n * GLA_BLOCK, GLA_BLOCK), GLA_BLOCK)
        of_ref[rows, :] = block(0, rows)
        return carry

    lax.fori_loop(0, nblk, forward, 0)

    def backward(step, carry):
        n = nblk - 1 - step
        rows = pl.ds(pl.multiple_of(n * GLA_BLOCK, GLA_BLOCK), GLA_BLOCK)
        o = of_ref[rows, :] + block(1, rows)
        normed = []
        for h in range(B_HEADS):
            oh = o[:, h * B_VAL_DIM:(h + 1) * B_VAL_DIM]
            normed.append(oh * _rms_scale(oh))
        g = g_ref[rows, :]
        out = jnp.concatenate(normed, axis=1) * gn_ref[...] * (g * jax.nn.sigmoid(g))
        o_ref[rows, :] = out.astype(o_ref.dtype)
        return carry

    lax.fori_loop(0, nblk, backward, 0)


def _gla(proj, w2, layer, b2, gn, masks, bsz, seq):
    def col(width, start):
        return pl.BlockSpec((seq, width), lambda b: (b, start // width))

    def whole(a):
        return pl.BlockSpec(a.shape, lambda b: (0,) * a.ndim)

    return pl.pallas_call(
        _gla_kernel,
        grid=(bsz,),
        in_specs=[col(B_QK, COL_BQ), col(B_QK, COL_BK), col(B_V, COL_BV), col(B_V, COL_BG),
                  col(LANE, COL_Z), pl.BlockSpec((None,) + w2.shape[1:], lambda b: (layer, 0, 0)),
                  whole(b2), whole(gn), whole(masks)],
        out_specs=pl.BlockSpec((seq, B_V), lambda b: (b, 0)),
        out_shape=jax.ShapeDtypeStruct((bsz * seq, B_V), BF16),
        scratch_shapes=[pltpu.VMEM((seq, B_V), F32),
                        pltpu.VMEM((2, GLA_PAIRS, 2 * B_VAL_DIM, LANE), F32)],
        compiler_params=_cparams(("parallel",)),
        name="gla",
    )(proj, proj, proj, proj, proj, w2, b2, gn, masks)


LRU_CHUNK = 256
SUBLANES = 8


def _lru_kernel(x_ref, y_ref, cw_ref, cb_ref, wa_ref, wx_ref, ba_ref, bx_ref, lam_ref, o_ref,
                xp_ref, af_ref, hf_ref, ab_ref, hb_ref):
    seq = x_ref.shape[0]
    ch = x_ref.shape[1]
    nchunk = seq // LRU_CHUNK
    groups = LRU_CHUNK // SUBLANES
    halo = SUBLANES
    xp_ref[0:halo, :] = jnp.zeros((halo, ch), F32)
    xp_ref[seq + halo:seq + 2 * halo, :] = jnp.zeros((halo, ch), F32)
    xp_ref[halo:seq + halo, :] = x_ref[...]
    decay_rate = _softplus(-lam_ref[...])
    sub = lax.broadcasted_iota(jnp.int32, (groups, SUBLANES, ch), 1)
    scan_refs = ((af_ref, hf_ref), (ab_ref, hb_ref))

    def gates_and_local_scan(c, carry):
        r0 = pl.multiple_of(c * LRU_CHUNK, LRU_CHUNK)
        ext = LRU_CHUNK + 2 * halo
        xe = xp_ref[pl.ds(r0, ext), :]
        xc = (cw_ref[0:1, :] * pltpu.roll(xe, 2, 0) + cw_ref[1:2, :] * pltpu.roll(xe, 1, 0)
              + cw_ref[2:3, :] * xe + cw_ref[3:4, :] * pltpu.roll(xe, ext - 1, 0))
        xc = xc[halo:halo + LRU_CHUNK] + cb_ref[...]
        xb = xc.astype(BF16)
        for d in range(2):
            r = jax.nn.sigmoid(_dot(xb, wa_ref[d, 0]) + ba_ref[d:d + 1, :])
            i = jax.nn.sigmoid(_dot(xb, wx_ref[d, 0]) + bx_ref[d:d + 1, :])
            log_a = -LRU_C * r * decay_rate[d:d + 1, :]
            a = jnp.exp(log_a).reshape(groups, SUBLANES, ch)
            t = jnp.tanh(log_a)
            u = (jnp.sqrt(-2.0 * t / (1.0 - t)) * (i * xc)).reshape(groups, SUBLANES, ch)
            for s in (1, 2, 4):
                shift = s if d == 0 else SUBLANES - s
                keep = (sub >= s) if d == 0 else (sub < SUBLANES - s)
                a_sh = pltpu.roll(a, shift, 1)
                u_sh = pltpu.roll(u, shift, 1)
                u = jnp.where(keep, a * u_sh + u, u)
                a = jnp.where(keep, a * a_sh, a)
            a_ref, h_ref = scan_refs[d]
            a_ref[pl.ds(r0, LRU_CHUNK), :] = a.reshape(LRU_CHUNK, ch)
            h_ref[pl.ds(r0, LRU_CHUNK), :] = u.reshape(LRU_CHUNK, ch)
        return carry

    lax.fori_loop(0, nchunk, gates_and_local_scan, 0)

    ngroups = seq // SUBLANES

    def carry_groups(gi, carry):
        cf, cb = carry
        rf = pl.ds(pl.multiple_of(gi * SUBLANES, SUBLANES), SUBLANES)
        hf = hf_ref[rf, :] + af_ref[rf, :] * cf
        hf_ref[rf, :] = hf
        rb = pl.ds(pl.multiple_of((ngroups - 1 - gi) * SUBLANES, SUBLANES), SUBLANES)
        hb = hb_ref[rb, :] + ab_ref[rb, :] * cb
        hb_ref[rb, :] = hb
        return hf[SUBLANES - 1:SUBLANES, :], hb[0:1, :]

    zero = jnp.zeros((1, ch), F32)
    lax.fori_loop(0, ngroups, carry_groups, (zero, zero), unroll=8)

    def gate_out(c, carry):
        rows = pl.ds(pl.multiple_of(c * LRU_CHUNK, LRU_CHUNK), LRU_CHUNK)
        h = hf_ref[rows, :] + hb_ref[rows, :]
        o_ref[rows, :] = (h * jax.nn.gelu(y_ref[rows, :])).astype(o_ref.dtype)
        return carry

    lax.fori_loop(0, nchunk, gate_out, 0)


def _rglru(proj, conv_w, conv_b, w_a, w_x, layer, b_a, b_x, lam, bsz, seq):
    ch = C_BLOCK_DIM

    def col(start):
        return pl.BlockSpec((seq, ch), lambda b, c: (b, start // ch + c))

    def per_block(rows):
        return pl.BlockSpec((rows, ch), lambda b, c: (0, c))

    gate_w = pl.BlockSpec((None, 2, 1, ch, ch), lambda b, c: (layer, 0, c, 0, 0))
    return pl.pallas_call(
        _lru_kernel,
        grid=(bsz, C_BLOCKS),
        in_specs=[col(COL_CX), col(COL_CY), per_block(CONV_WIDTH), per_block(1), gate_w, gate_w,
                  per_block(2), per_block(2), per_block(2)],
        out_specs=pl.BlockSpec((seq, ch), lambda b, c: (b, c)),
        out_shape=jax.ShapeDtypeStruct((bsz * seq, C_WIDTH), BF16),
        scratch_shapes=[pltpu.VMEM((seq + 2 * SUBLANES, ch), F32)] + [pltpu.VMEM((seq, ch), F32)] * 4,
        compiler_params=_cparams(("parallel", "parallel")),
        name="rglru",
    )(proj, proj, conv_w, conv_b.reshape(1, C_WIDTH), w_a, w_x, b_a, b_x, lam)


def _out_proj_kernel(oa_ref, ob_ref, oc_ref, w_ref, g_ref, x_ref, o_ref):
    mixed = (_dot(oa_ref[...], w_ref[0:A_Q, :]) + _dot(ob_ref[...], w_ref[A_Q:A_Q + B_V, :])
             + _dot(oc_ref[...], w_ref[A_Q + B_V:A_Q + B_V + C_WIDTH, :]))
    o_ref[...] = x_ref[...] + mixed * _rms_scale(mixed) * g_ref[...]


def _out_proj(oa, ob, oc, w, layer, g, x, tm):
    m, d = x.shape

    def rows(width):
        return pl.BlockSpec((tm, width), lambda i: (i, 0))

    return pl.pallas_call(
        _out_proj_kernel,
        grid=(m // tm,),
        in_specs=[rows(A_Q), rows(B_V), rows(C_WIDTH),
                  pl.BlockSpec((None,) + w.shape[1:], lambda i: (layer, 0, 0)),
                  pl.BlockSpec((1, d), lambda i: (0, 0)),
                  rows(d)],
        out_specs=rows(d),
        out_shape=jax.ShapeDtypeStruct((m, d), F32),
        compiler_params=_cparams(("parallel",)),
        name="out_proj",
    )(oa, ob, oc, w, g.reshape(1, d), x)


def _cross_attn_kernel(x_ref, gpre_ref, wq_ref, k_ref, v_ref, wo_ref, gpost_ref, o_ref):
    x = x_ref[...]
    hn = (x * _rms_scale(x) * gpre_ref[...]).astype(BF16)
    q = _dot(hn, wq_ref[...]).astype(BF16)
    acc = None
    for h in range(X_HEADS):
        hs = slice(h * X_HEAD_DIM, (h + 1) * X_HEAD_DIM)
        s = _dot_nt(q[:, hs], k_ref[:, hs]) * (X_HEAD_DIM ** -0.5)
        m = jnp.max(s, axis=-1, keepdims=True)
        p = jnp.exp(s - m)
        o = _dot(p.astype(BF16), v_ref[:, hs]) / jnp.sum(p, axis=-1, keepdims=True)
        part = _dot(o.astype(BF16), wo_ref[hs, :])
        acc = part if acc is None else acc + part
    o_ref[...] = x + acc * _rms_scale(acc) * gpost_ref[...]


def _cross_attention(x, gpre, wq, k, v, wo, layer, gpost, bsz, seq, tq):
    d = D_MODEL
    per_row = seq // tq
    mem_spec = pl.BlockSpec((MEM_LEN, d), lambda b, i: (b, 0))
    gain_spec = pl.BlockSpec((1, d), lambda b, i: (0, 0))
    weight_spec = _resident((None, d, d), lambda b, i: (layer, 0, 0))
    row_spec = pl.BlockSpec((tq, d), lambda b, i: (b * per_row + i, 0))
    return pl.pallas_call(
        _cross_attn_kernel,
        grid=(bsz, per_row),
        in_specs=[row_spec, gain_spec, weight_spec, mem_spec, mem_spec, weight_spec, gain_spec],
        out_specs=row_spec,
        out_shape=jax.ShapeDtypeStruct((bsz * seq, d), F32),
        compiler_params=_cparams(("parallel", "parallel")),
        name="cross_attention",
    )(x, gpre.reshape(1, d), wq, k, v, wo, gpost.reshape(1, d))


def _ffn_kernel(x_ref, gpre_ref, wu_ref, wd_ref, gpost_ref, o_ref, hn_ref, acc_ref):
    f = pl.program_id(1)

    @pl.when(f == 0)
    def _():
        x = x_ref[...]
        hn_ref[...] = (x * _rms_scale(x) * gpre_ref[...]).astype(BF16)
        acc_ref[...] = jnp.zeros(acc_ref.shape, F32)

    up = _dot(hn_ref[...], wu_ref[...])
    act = jnp.square(jnp.maximum(up, 0.0)).astype(BF16)
    acc_ref[...] += _dot(act, wd_ref[...])

    @pl.when(f == pl.num_programs(1) - 1)
    def _():
        ff = acc_ref[...]
        o_ref[...] = x_ref[...] + ff * _rms_scale(ff) * gpost_ref[...]


def _ffn(x, gpre, w_up, w_down, layer, gpost, tm, tf):
    m, d = x.shape
    dff = w_up.shape[2]
    return pl.pallas_call(
        _ffn_kernel,
        grid=(m // tm, dff // tf),
        in_specs=[pl.BlockSpec((tm, d), lambda i, f: (i, 0)),
                  pl.BlockSpec((1, d), lambda i, f: (0, 0)),
                  pl.BlockSpec((None, d, tf), lambda i, f: (layer, 0, f)),
                  pl.BlockSpec((None, tf, d), lambda i, f: (layer, f, 0)),
                  pl.BlockSpec((1, d), lambda i, f: (0, 0))],
        out_specs=pl.BlockSpec((tm, d), lambda i, f: (i, 0)),
        out_shape=jax.ShapeDtypeStruct((m, d), F32),
        scratch_shapes=[pltpu.VMEM((tm, d), BF16), pltpu.VMEM((tm, d), F32)],
        compiler_params=_cparams(("parallel", "arbitrary")),
        name="ffn",
    )(x, gpre.reshape(1, d), w_up, w_down, gpost.reshape(1, d))


def _split_w_in(w):
    z0 = A_Q + 2 * A_KV + 2 * B_QK + 2 * B_V
    z1 = z0 + 2 * GATE_RANK
    wt = jnp.swapaxes(w, 1, 2).astype(BF16)
    gates = jnp.pad(wt[:, z0:z1], ((0, 0), (0, LANE - 2 * GATE_RANK), (0, 0)))
    return wt[:, :z0], wt[:, z1:], gates


def _gate_expansion(w2_f, w2_b):
    w = jnp.zeros((w2_f.shape[0], LANE, 2 * B_QK), F32)
    w = w.at[:, 0:GATE_RANK, 0:B_QK].set(w2_f)
    w = w.at[:, GATE_RANK:2 * GATE_RANK, B_QK:].set(w2_b)
    return w.astype(BF16)


def kernel(x, mem, rel_bias, w_in, w_out, attn_sink, gla_w2_f, gla_b2_f, gla_w2_b, gla_b2_b, gla_norm, conv_w, conv_b, lru_wa, lru_ba, lru_wx, lru_bx, lru_lambda, xq, xk, xv, xo, w_up, w_down, norm_mix_pre, norm_mix_post, norm_mem, norm_x_pre, norm_x_post, norm_ff_pre, norm_ff_post):
    bsz, seq, d = x.shape
    assert d == D_MODEL and seq % LRU_CHUNK == 0 and seq % 512 == 0
    xs = x.reshape(bsz * seq, d)
    mems = mem.reshape(bsz * mem.shape[1], d)
    bias_tab = _attention_bias_table(rel_bias)
    masks = _gla_masks()
    w_in_main, w_in_c, w_in_z = _split_w_in(w_in)
    w2_b = _gate_expansion(gla_w2_f, gla_w2_b)
    b2 = jnp.concatenate([gla_b2_f, gla_b2_b], axis=-1)
    bf = lambda w: w.astype(BF16)
    w_out_b, wa_b, wx_b = bf(w_out), bf(lru_wa), bf(lru_wx)
    xq_b, xk_b, xv_b, xo_b = bf(xq), bf(xk), bf(xv), bf(xo)
    w_up_b, w_down_b = bf(w_up), bf(w_down)
    for l in range(DEPTH):
        proj = _in_proj(xs, norm_mix_pre[l], w_in_main, w_in_c, w_in_z, l, tm=512)
        oa = _windowed_attention(proj, bias_tab, attn_sink[l], bsz, seq)
        ob = _gla(proj, w2_b, l, b2[l].reshape(1, 2 * B_QK), gla_norm[l].reshape(1, B_V), masks, bsz, seq)
        oc = _rglru(proj, conv_w[l], conv_b[l], wa_b, wx_b, l, lru_ba[l], lru_bx[l], lru_lambda[l],
                    bsz, seq)
        xs = _out_proj(oa, ob, oc, w_out_b, l, norm_mix_post[l], xs, tm=512)

        k = _norm_matmul(mems, norm_mem[l], xk_b, l, BF16, tm=512, tn=1024)
        v = _norm_matmul(mems, norm_mem[l], xv_b, l, BF16, tm=512, tn=1024)
        xs = _cross_attention(xs, norm_x_pre[l], xq_b, k, v, xo_b, l, norm_x_post[l], bsz, seq, tq=512)

        xs = _ffn(xs, norm_ff_pre[l], w_up_b, w_down_b, l, norm_ff_post[l], tm=512, tf=1024)
    return xs.reshape(bsz, seq, d)
```

```python
import functools
import math

import numpy as np
import jax
import jax.numpy as jnp
from jax import lax
from jax.experimental import pallas as pl
from jax.experimental.pallas import tpu as pltpu

F32 = jnp.float32
BF16 = jnp.bfloat16

D_MODEL = 2048
DEPTH = 4
MEM_LEN = 256
A_HEAD_DIM = 128
A_HEADS = 8
A_KV_HEADS = 2
A_GROUP = 4
WINDOW = 128
BLOCK = 128
N_BUCKETS = 32
MAX_DISTANCE = 128
B_HEADS = 4
B_KEY_DIM = 64
B_VAL_DIM = 128
GATE_RANK = 16
GATE_TAU = 16.0
C_WIDTH = 512
C_BLOCKS = 4
C_BLOCK_DIM = 128
CONV_WIDTH = 4
LRU_C = 8.0
X_HEADS = 4
X_HEAD_DIM = 512
D_FF = 8192
EPS = 1e-6
NEG_INF = -1e30
LOG2E = math.log2(math.e)

A_Q = A_HEADS * A_HEAD_DIM
A_KV = A_KV_HEADS * A_HEAD_DIM
B_QK = B_HEADS * B_KEY_DIM
B_V = B_HEADS * B_VAL_DIM

LANE = 128
COL_AQ = 0
COL_AK = COL_AQ + A_Q
COL_AV = COL_AK + A_KV
COL_BQ = COL_AV + A_KV
COL_BK = COL_BQ + B_QK
COL_BV = COL_BK + B_QK
COL_BG = COL_BV + B_V
COL_CX = COL_BG + B_V
COL_CY = COL_CX + C_WIDTH
COL_Z = COL_CY + C_WIDTH
D_PROJ = COL_Z + LANE

GLA_BLOCK = 128
GLA_LEVELS = (1, 2, 4, 8, 16, 32, 64)
GLA_PAIRS = B_HEADS // 2
VMEM_LIMIT_MIB = 56


def _cparams(semantics, vmem_mib=VMEM_LIMIT_MIB):
    return pltpu.CompilerParams(dimension_semantics=semantics,
                                vmem_limit_bytes=vmem_mib * 1024 * 1024)


def _dot(a, b):
    return jnp.dot(a, b, preferred_element_type=F32)


def _dot_nt(a, b):
    return lax.dot_general(a, b, (((1,), (1,)), ((), ())), preferred_element_type=F32)


def _dot_tn(a, b):
    return lax.dot_general(a, b, (((0,), (0,)), ((), ())), preferred_element_type=F32)


def _softplus(x):
    return jnp.maximum(x, 0.0) + jnp.log(1.0 + jnp.exp(-jnp.abs(x)))


def _rms_scale(x):
    return lax.rsqrt(jnp.mean(jnp.square(x), axis=-1, keepdims=True) + EPS)


def _resident(shape, index_map):
    return pl.BlockSpec(shape, index_map, pipeline_mode=pl.Buffered(1))


def _kv_proj_kernel(x_ref, g_ref, wk_ref, wv_ref, k_ref, v_ref):
    x = x_ref[...]
    hn = (x * _rms_scale(x) * g_ref[...]).astype(BF16)
    k_ref[...] = _dot(hn, wk_ref[...]).astype(BF16)
    v_ref[...] = _dot(hn, wv_ref[...]).astype(BF16)


def _kv_proj(x, g, wk, wv, layer, tm):
    m, d = x.shape
    weight = _resident((None, d, d), lambda i: (layer, 0, 0))
    rows = pl.BlockSpec((tm, d), lambda i: (i, 0))
    return pl.pallas_call(
        _kv_proj_kernel,
        grid=(m // tm,),
        in_specs=[rows, pl.BlockSpec((1, d), lambda i: (0, 0)), weight, weight],
        out_specs=[rows, rows],
        out_shape=[jax.ShapeDtypeStruct((m, d), BF16)] * 2,
        compiler_params=_cparams(("parallel",)),
        name="kv_proj",
    )(x, g.reshape(1, d), wk, wv)


def _in_proj_kernel(x_ref, g_ref, wm_ref, wc_ref, wz_ref, o_ref):
    x = x_ref[...]
    hn = (x * _rms_scale(x) * g_ref[...]).astype(BF16)
    o_ref[:, 0:COL_CX] = _dot_nt(hn, wm_ref[...])
    o_ref[:, COL_CX:COL_Z] = _dot_nt(hn, wc_ref[...])
    o_ref[:, COL_Z:D_PROJ] = _dot_nt(hn, wz_ref[...])


def _in_proj(x, g, w_main, w_c, w_z, layer, tm):
    m, k = x.shape

    def weight(w):
        return _resident((None, w.shape[1], k), lambda i: (layer, 0, 0))

    return pl.pallas_call(
        _in_proj_kernel,
        grid=(m // tm,),
        in_specs=[pl.BlockSpec((tm, k), lambda i: (i, 0)), pl.BlockSpec((1, k), lambda i: (0, 0)),
                  weight(w_main), weight(w_c), weight(w_z)],
        out_specs=pl.BlockSpec((tm, D_PROJ), lambda i: (i, 0)),
        out_shape=jax.ShapeDtypeStruct((m, D_PROJ), F32),
        compiler_params=_cparams(("parallel",)),
        name="in_proj",
    )(x, g.reshape(1, k), w_main, w_c, w_z)


def _attn_kernel(sink_ref, q_ref, k_ref, v_ref, bias_ref, o_ref, kb_ref, vt_ref, p_ref, sw_ref):
    g = pl.program_id(1)
    seq = q_ref.shape[0]
    nblk = seq // BLOCK
    kb_ref[...] = k_ref[...].astype(BF16)
    for n in range(nblk):
        vt_ref[n] = v_ref[n * BLOCK:(n + 1) * BLOCK, :].T.astype(BF16)
    sink = jnp.concatenate(
        [jnp.full((1, BLOCK), sink_ref[g * A_GROUP + h] * LOG2E, F32) for h in range(A_GROUP)], axis=1)

    def tiles(first, last):
        return (1 if first else 0), (2 if last else 3)

    def scores(n, first, last):
        t0, t1 = tiles(first, last)
        nk = (t1 - t0) * BLOCK
        r0 = pl.multiple_of(n * BLOCK, BLOCK)
        k = kb_ref[pl.ds(pl.multiple_of(r0 + (t0 - 1) * BLOCK, BLOCK), nk), :]
        q = jnp.concatenate([q_ref[pl.ds(r0, BLOCK), h * A_HEAD_DIM:(h + 1) * A_HEAD_DIM]
                             for h in range(A_GROUP)], axis=0).astype(BF16)
        s = _dot_nt(k, q) * (A_HEAD_DIM ** -0.5 * LOG2E) + bias_ref[t0 * BLOCK:t1 * BLOCK, :]
        m = jnp.maximum(jnp.max(s, axis=0, keepdims=True), sink)
        p_ref[0:nk, :] = jnp.exp2(s - m).astype(BF16)
        sw_ref[...] = jnp.exp2(sink - m)

    def output(n, first, last):
        t0, t1 = tiles(first, last)
        nk = (t1 - t0) * BLOCK
        r0 = pl.multiple_of(n * BLOCK, BLOCK)
        v_t = jnp.concatenate([vt_ref[n + t - 1] for t in range(t0, t1)], axis=1)
        ones = jnp.ones((2 * SUBLANES, nk), BF16)
        o = _dot(jnp.concatenate([v_t, ones], axis=0), p_ref[0:nk, :])
        denom = o[A_HEAD_DIM:A_HEAD_DIM + 1, :] + sw_ref[...]
        o = o[0:A_HEAD_DIM, :] / denom
        for h in range(A_GROUP):
            o_ref[pl.ds(r0, BLOCK), h * A_HEAD_DIM:(h + 1) * A_HEAD_DIM] = (
                o[:, h * BLOCK:(h + 1) * BLOCK].T.astype(o_ref.dtype))

    def body(n, carry):
        output(n, False, False)
        scores(n + 1, False, False)
        return carry

    scores(0, True, False)
    output(0, True, False)
    scores(1, False, False)
    lax.fori_loop(1, nblk - 2, body, 0)
    output(nblk - 2, False, False)
    scores(nblk - 1, False, True)
    output(nblk - 1, False, True)


def _windowed_attention(proj, bias_tab, sink, bsz, seq):
    qw = A_GROUP * A_HEAD_DIM
    nblk = seq // BLOCK

    def kv_spec(col0):
        return pl.BlockSpec((seq, A_HEAD_DIM), lambda b, g: (b, col0 // A_HEAD_DIM + g))

    return pl.pallas_call(
        _attn_kernel,
        grid=(bsz, A_KV_HEADS),
        in_specs=[pl.BlockSpec(memory_space=pltpu.SMEM),
                  pl.BlockSpec((seq, qw), lambda b, g: (b, g)),
                  kv_spec(COL_AK), kv_spec(COL_AV),
                  pl.BlockSpec((None, 3 * BLOCK, qw), lambda b, g: (g, 0, 0))],
        out_specs=pl.BlockSpec((seq, qw), lambda b, g: (b, g)),
        out_shape=jax.ShapeDtypeStruct((bsz * seq, A_Q), BF16),
        scratch_shapes=[pltpu.VMEM((seq, A_HEAD_DIM), BF16),
                        pltpu.VMEM((nblk, A_HEAD_DIM, BLOCK), BF16),
                        pltpu.VMEM((3 * BLOCK, qw), BF16),
                        pltpu.VMEM((1, qw), F32)],
        compiler_params=_cparams(("parallel", "parallel")),
        name="windowed_attention",
    )(sink, proj, proj, proj, bias_tab)


def _t5_bucket(rel):
    nb = N_BUCKETS // 2
    max_exact = nb // 2
    ret = jnp.where(rel > 0, nb, 0)
    n = jnp.abs(rel)
    nf = jnp.maximum(n, 1).astype(F32)
    large = max_exact + (jnp.log(nf / max_exact) / math.log(MAX_DISTANCE / max_exact)
                         * (nb - max_exact)).astype(jnp.int32)
    large = jnp.minimum(large, nb - 1)
    return ret + jnp.where(n < max_exact, n, large)


def _attention_bias_table(rel_bias):
    kj = jnp.arange(3 * BLOCK)[:, None]
    qi = jnp.arange(BLOCK)[None, :]
    rel = kj - BLOCK - qi
    onehot = jax.nn.one_hot(_t5_bucket(rel), N_BUCKETS, dtype=F32)
    bias = jnp.einsum("kqn,nh->hkq", onehot, rel_bias.astype(F32), precision=lax.Precision.HIGHEST)
    bias = jnp.where((jnp.abs(rel) <= WINDOW)[None], bias * LOG2E, NEG_INF)
    bias = bias.reshape(A_KV_HEADS, A_GROUP, 3 * BLOCK, BLOCK)
    return jnp.transpose(bias, (0, 2, 1, 3)).reshape(A_KV_HEADS, 3 * BLOCK, A_GROUP * BLOCK)


def _gla_masks():
    c = GLA_BLOCK
    i = np.arange(c)[:, None]
    j = np.arange(c)[None, :]
    fwd = [i == j]
    for s in GLA_LEVELS:
        fwd.append((i // (2 * s) == j // (2 * s)) & ((i // s) % 2 == 1) & ((j // s) % 2 == 0))
    fwd = np.stack(fwd)
    return jnp.asarray(np.stack([fwd, fwd.transpose(0, 2, 1)]).astype(np.float32))


def _gla_block(direction, rows, q_ref, k_ref, v_ref, z_ref, w2_ref, b2_ref, m_ref, st_ref):
    c = GLA_BLOCK
    q = q_ref[rows, :] * (B_KEY_DIM ** -0.5)
    k = k_ref[rows, :]
    v = v_ref[rows, :].astype(BF16)
    gate_cols = slice(direction * B_QK, (direction + 1) * B_QK)
    pre = _dot(z_ref[rows, :].astype(BF16), w2_ref[:, gate_cols]) + b2_ref[:, gate_cols]
    log_a = -_softplus(-pre) * (LOG2E / GATE_TAU)

    row = lax.broadcasted_iota(jnp.int32, (c, B_QK), 0)
    first_head = lax.broadcasted_iota(jnp.int32, (c, LANE), 1) < B_KEY_DIM
    pair = [slice(p * LANE, (p + 1) * LANE) for p in range(GLA_PAIRS)]
    k_first = [jnp.where(first_head, k[:, ps], 0.0) for ps in pair]
    k_second = [jnp.where(first_head, 0.0, k[:, ps]) for ps in pair]

    def exponents(p_s, t_s):
        if direction == 0:
            return p_s, t_s - p_s
        return t_s - p_s + log_a, p_s - log_a

    def scaled(e_q, e_k, p):
        w_q = jnp.exp2(e_q[:, pair[p]])
        w_k = jnp.exp2(e_k[:, pair[p]])
        q_t = (q[:, pair[p]] * w_q).astype(BF16)
        k_t = jnp.concatenate([(k_first[p] * w_k).astype(BF16), (k_second[p] * w_k).astype(BF16)], axis=0)
        return q_t, k_t

    attn = [None] * GLA_PAIRS

    def add_level(level, q_t, k_t, p):
        mask = m_ref[direction, level]
        contrib = jnp.concatenate([mask, mask], axis=1) * _dot_nt(q_t, k_t)
        attn[p] = contrib if attn[p] is None else attn[p] + contrib

    if direction == 0:
        for p in range(GLA_PAIRS):
            k_t = jnp.concatenate([k_first[p].astype(BF16), k_second[p].astype(BF16)], axis=0)
            add_level(0, q[:, pair[p]].astype(BF16), k_t, p)
    p_s = log_a
    t_s = log_a
    for li, s in enumerate(GLA_LEVELS):
        e_q, e_k = exponents(p_s, t_s)
        for p in range(GLA_PAIRS):
            q_t, k_t = scaled(e_q, e_k, p)
            add_level(li + 1, q_t, k_t, p)
        upper = (row & s) != 0
        t_before = pltpu.roll(t_s, s, 0)
        t_after = pltpu.roll(t_s, c - s, 0)
        p_s = p_s + jnp.where(upper, t_before, 0.0)
        t_s = t_s + jnp.where(upper, t_before, t_after)

    e_q, e_k = exponents(p_s, t_s)
    state_decay = jnp.exp2(t_s[0:1, :])
    own_block = lax.broadcasted_iota(jnp.int32, (2 * B_VAL_DIM, LANE), 0) < B_VAL_DIM
    own_block = own_block == (lax.broadcasted_iota(jnp.int32, (2 * B_VAL_DIM, LANE), 1) < B_KEY_DIM)
    zero_v = jnp.zeros((c, B_VAL_DIM), BF16)
    outs = []
    for p in range(GLA_PAIRS):
        v_pair = v[:, 2 * p * B_VAL_DIM:(2 * p + 2) * B_VAL_DIM]
        v_diag = jnp.concatenate(
            [jnp.concatenate([v_pair[:, :B_VAL_DIM], zero_v], axis=1),
             jnp.concatenate([zero_v, v_pair[:, B_VAL_DIM:]], axis=1)], axis=0)
        state_t = st_ref[direction, p]
        q_t = (q[:, pair[p]] * jnp.exp2(e_q[:, pair[p]])).astype(BF16)
        k_t = (k[:, pair[p]] * jnp.exp2(e_k[:, pair[p]])).astype(BF16)
        o = _dot_nt(q_t, state_t.astype(BF16)) + _dot(attn[p].astype(BF16), v_diag)
        update = jnp.where(own_block, _dot_tn(v_pair, k_t), 0.0)
        st_ref[direction, p] = state_t * state_decay[:, pair[p]] + update
        outs.append(o)
    return jnp.concatenate(outs, axis=1)


def _gla_kernel(q_ref, k_ref, v_ref, g_ref, z_ref, w2_ref, b2_ref, gn_ref, m_ref,
                o_ref, of_ref, st_ref):
    nblk = q_ref.shape[0] // GLA_BLOCK
    st_ref[...] = jnp.zeros(st_ref.shape, F32)
    block = functools.partial(_gla_block, q_ref=q_ref, k_ref=k_ref, v_ref=v_ref, z_ref=z_ref,
                              w2_ref=w2_ref, b2_ref=b2_ref, m_ref=m_ref, st_ref=st_ref)

    def forward(n, carry):
        rows = pl.ds(pl.multiple_of(n * GLA_BLOCK, GLA_BLOCK), GLA_BLOCK)
        of_ref[rows, :] = block(0, rows)
        return carry

    lax.fori_loop(0, nblk, forward, 0)

    def backward(step, carry):
        n = nblk - 1 - step
        rows = pl.ds(pl.multiple_of(n * GLA_BLOCK, GLA_BLOCK), GLA_BLOCK)
        o = of_ref[rows, :] + block(1, rows)
        normed = []
        for h in range(B_HEADS):
            oh = o[:, h * B_VAL_DIM:(h + 1) * B_VAL_DIM]
            normed.append(oh * _rms_scale(oh))
        g = g_ref[rows, :]
        out = jnp.concatenate(normed, axis=1) * gn_ref[...] * (g * jax.nn.sigmoid(g))
        o_ref[rows, :] = out.astype(o_ref.dtype)
        return carry

    lax.fori_loop(0, nblk, backward, 0)


def _gla(proj, w2, layer, b2, gn, masks, bsz, seq):
    def col(width, start):
        return pl.BlockSpec((seq, width), lambda b: (b, start // width))

    def whole(a):
        return pl.BlockSpec(a.shape, lambda b: (0,) * a.ndim)

    return pl.pallas_call(
        _gla_kernel,
        grid=(bsz,),
        in_specs=[col(B_QK, COL_BQ), col(B_QK, COL_BK), col(B_V, COL_BV), col(B_V, COL_BG),
                  col(LANE, COL_Z), pl.BlockSpec((None,) + w2.shape[1:], lambda b: (layer, 0, 0)),
                  whole(b2), whole(gn), whole(masks)],
        out_specs=pl.BlockSpec((seq, B_V), lambda b: (b, 0)),
        out_shape=jax.ShapeDtypeStruct((bsz * seq, B_V), BF16),
        scratch_shapes=[pltpu.VMEM((seq, B_V), F32),
                        pltpu.VMEM((2, GLA_PAIRS, 2 * B_VAL_DIM, LANE), F32)],
        compiler_params=_cparams(("parallel",)),
        name="gla",
    )(proj, proj, proj, proj, proj, w2, b2, gn, masks)


LRU_SEG = 256
SUBLANES = 8
LRU_PITCH = LRU_SEG + SUBLANES


def _lru_kernel(x_ref, y_ref, cw_ref, cb_ref, wa_ref, wx_ref, ba_ref, bx_ref, lam_ref, o_ref,
                xp_ref, af_ref, hf_ref, ab_ref, hb_ref):
    seq = x_ref.shape[0]
    ch = x_ref.shape[1]
    nseg = seq // LRU_SEG
    halo = SUBLANES
    xp_ref[0:halo, :] = jnp.zeros((halo, ch), F32)
    xp_ref[seq + halo:seq + 2 * halo, :] = jnp.zeros((halo, ch), F32)
    xp_ref[halo:seq + halo, :] = x_ref[...]
    decay_rate = _softplus(-lam_ref[...])
    scan_refs = ((af_ref, hf_ref), (ab_ref, hb_ref))

    def gates(c, carry):
        r0 = pl.multiple_of(c * LRU_SEG, LRU_SEG)
        dst = pl.ds(pl.multiple_of(c * LRU_PITCH, SUBLANES), LRU_SEG)
        ext = LRU_SEG + 2 * halo
        xe = xp_ref[pl.ds(r0, ext), :]
        xc = (cw_ref[0:1, :] * pltpu.roll(xe, 2, 0) + cw_ref[1:2, :] * pltpu.roll(xe, 1, 0)
              + cw_ref[2:3, :] * xe + cw_ref[3:4, :] * pltpu.roll(xe, ext - 1, 0))
        xc = xc[halo:halo + LRU_SEG] + cb_ref[...]
        xb = xc.astype(BF16)
        for d in range(2):
            r = jax.nn.sigmoid(_dot(xb, wa_ref[d, 0]) + ba_ref[d:d + 1, :])
            i = jax.nn.sigmoid(_dot(xb, wx_ref[d, 0]) + bx_ref[d:d + 1, :])
            log_a = -LRU_C * r * decay_rate[d:d + 1, :]
            t = jnp.tanh(log_a)
            a_ref, u_ref = scan_refs[d]
            a_ref[dst, :] = jnp.exp(log_a)
            u_ref[dst, :] = jnp.sqrt(-2.0 * t / (1.0 - t)) * (i * xc)
        return carry

    lax.fori_loop(0, nseg, gates, 0)

    def scan_step(j, carry):
        hf, pf, hb, pb = carry
        rf = pl.ds(j, nseg, stride=LRU_PITCH)
        a = af_ref[rf, :]
        hf = a * hf + hf_ref[rf, :]
        pf = a * pf
        hf_ref[rf, :] = hf
        af_ref[rf, :] = pf
        rb = pl.ds(LRU_SEG - 1 - j, nseg, stride=LRU_PITCH)
        a = ab_ref[rb, :]
        hb = a * hb + hb_ref[rb, :]
        pb = a * pb
        hb_ref[rb, :] = hb
        ab_ref[rb, :] = pb
        return hf, pf, hb, pb

    zeros = jnp.zeros((nseg, ch), F32)
    ones = jnp.ones((nseg, ch), F32)
    lax.fori_loop(0, LRU_SEG, scan_step, (zeros, ones, zeros, ones), unroll=4)

    enter_f = [None] * nseg
    state = jnp.zeros((1, ch), F32)
    for s in range(nseg):
        enter_f[s] = state
        last = s * LRU_PITCH + LRU_SEG - 1
        state = hf_ref[last:last + 1, :] + af_ref[last:last + 1, :] * state
    enter_b = [None] * nseg
    state = jnp.zeros((1, ch), F32)
    for s in reversed(range(nseg)):
        enter_b[s] = state
        first = s * LRU_PITCH
        state = hb_ref[first:first + 1, :] + ab_ref[first:first + 1, :] * state

    for s in range(nseg):
        seg = slice(s * LRU_PITCH, s * LRU_PITCH + LRU_SEG)
        rows = slice(s * LRU_SEG, (s + 1) * LRU_SEG)
        h = (hf_ref[seg, :] + af_ref[seg, :] * enter_f[s]) + (hb_ref[seg, :] + ab_ref[seg, :] * enter_b[s])
        o_ref[rows, :] = (h * jax.nn.gelu(y_ref[rows, :])).astype(o_ref.dtype)


def _rglru(proj, conv_w, conv_b, w_a, w_x, layer, b_a, b_x, lam, bsz, seq):
    ch = C_BLOCK_DIM

    def col(start):
        return pl.BlockSpec((seq, ch), lambda b, c: (b, start // ch + c))

    def per_block(rows):
        return pl.BlockSpec((rows, ch), lambda b, c: (0, c))

    gate_w = pl.BlockSpec((None, 2, 1, ch, ch), lambda b, c: (layer, 0, c, 0, 0))
    return pl.pallas_call(
        _lru_kernel,
        grid=(bsz, C_BLOCKS),
        in_specs=[col(COL_CX), col(COL_CY), per_block(CONV_WIDTH), per_block(1), gate_w, gate_w,
                  per_block(2), per_block(2), per_block(2)],
        out_specs=pl.BlockSpec((seq, ch), lambda b, c: (b, c)),
        out_shape=jax.ShapeDtypeStruct((bsz * seq, C_WIDTH), BF16),
        scratch_shapes=([pltpu.VMEM((seq + 2 * SUBLANES, ch), F32)]
                        + [pltpu.VMEM((seq // LRU_SEG * LRU_PITCH, ch), F32)] * 4),
        compiler_params=_cparams(("parallel", "parallel")),
        name="rglru",
    )(proj, proj, conv_w, conv_b.reshape(1, C_WIDTH), w_a, w_x, b_a, b_x, lam)


def _out_proj_kernel(oa_ref, ob_ref, oc_ref, w_ref, g_ref, x_ref, o_ref):
    mixed = (_dot(oa_ref[...], w_ref[0:A_Q, :]) + _dot(ob_ref[...], w_ref[A_Q:A_Q + B_V, :])
             + _dot(oc_ref[...], w_ref[A_Q + B_V:A_Q + B_V + C_WIDTH, :]))
    o_ref[...] = x_ref[...] + mixed * _rms_scale(mixed) * g_ref[...]


def _out_proj(oa, ob, oc, w, layer, g, x, tm):
    m, d = x.shape

    def rows(width):
        return pl.BlockSpec((tm, width), lambda i: (i, 0))

    return pl.pallas_call(
        _out_proj_kernel,
        grid=(m // tm,),
        in_specs=[rows(A_Q), rows(B_V), rows(C_WIDTH),
                  pl.BlockSpec((None,) + w.shape[1:], lambda i: (layer, 0, 0)),
                  pl.BlockSpec((1, d), lambda i: (0, 0)),
                  rows(d)],
        out_specs=rows(d),
        out_shape=jax.ShapeDtypeStruct((m, d), F32),
        compiler_params=_cparams(("parallel",)),
        name="out_proj",
    )(oa, ob, oc, w, g.reshape(1, d), x)


def _cross_attn_kernel(x_ref, gpre_ref, wq_ref, k_ref, v_ref, wo_ref, gpost_ref, o_ref):
    x = x_ref[...]
    hn = (x * _rms_scale(x) * gpre_ref[...]).astype(BF16)
    q = _dot(hn, wq_ref[...]).astype(BF16)
    acc = None
    for h in range(X_HEADS):
        hs = slice(h * X_HEAD_DIM, (h + 1) * X_HEAD_DIM)
        s = _dot_nt(q[:, hs], k_ref[:, hs]) * (X_HEAD_DIM ** -0.5)
        m = jnp.max(s, axis=-1, keepdims=True)
        p = jnp.exp(s - m)
        o = _dot(p.astype(BF16), v_ref[:, hs]) / jnp.sum(p, axis=-1, keepdims=True)
        part = _dot(o.astype(BF16), wo_ref[hs, :])
        acc = part if acc is None else acc + part
    o_ref[...] = x + acc * _rms_scale(acc) * gpost_ref[...]


def _cross_attention(x, gpre, wq, k, v, wo, layer, gpost, bsz, seq, tq):
    d = D_MODEL
    per_row = seq // tq
    mem_spec = pl.BlockSpec((MEM_LEN, d), lambda b, i: (b, 0))
    gain_spec = pl.BlockSpec((1, d), lambda b, i: (0, 0))
    weight_spec = _resident((None, d, d), lambda b, i: (layer, 0, 0))
    row_spec = pl.BlockSpec((tq, d), lambda b, i: (b * per_row + i, 0))
    return pl.pallas_call(
        _cross_attn_kernel,
        grid=(bsz, per_row),
        in_specs=[row_spec, gain_spec, weight_spec, mem_spec, mem_spec, weight_spec, gain_spec],
        out_specs=row_spec,
        out_shape=jax.ShapeDtypeStruct((bsz * seq, d), F32),
        compiler_params=_cparams(("parallel", "parallel")),
        name="cross_attention",
    )(x, gpre.reshape(1, d), wq, k, v, wo, gpost.reshape(1, d))


def _ffn_kernel(x_ref, gpre_ref, wu_ref, wd_ref, gpost_ref, o_ref, hn_ref, acc_ref):
    f = pl.program_id(1)

    @pl.when(f == 0)
    def _():
        x = x_ref[...]
        hn_ref[...] = (x * _rms_scale(x) * gpre_ref[...]).astype(BF16)
        acc_ref[...] = jnp.zeros(acc_ref.shape, F32)

    up = _dot(hn_ref[...], wu_ref[...])
    act = jnp.square(jnp.maximum(up, 0.0)).astype(BF16)
    acc_ref[...] += _dot(act, wd_ref[...])

    @pl.when(f == pl.num_programs(1) - 1)
    def _():
        ff = acc_ref[...]
        o_ref[...] = x_ref[...] + ff * _rms_scale(ff) * gpost_ref[...]


def _ffn(x, gpre, w_up, w_down, layer, gpost, tm, tf):
    m, d = x.shape
    dff = w_up.shape[2]
    return pl.pallas_call(
        _ffn_kernel,
        grid=(m // tm, dff // tf),
        in_specs=[pl.BlockSpec((tm, d), lambda i, f: (i, 0)),
                  pl.BlockSpec((1, d), lambda i, f: (0, 0)),
                  pl.BlockSpec((None, d, tf), lambda i, f: (layer, 0, f)),
                  pl.BlockSpec((None, tf, d), lambda i, f: (layer, f, 0)),
                  pl.BlockSpec((1, d), lambda i, f: (0, 0))],
        out_specs=pl.BlockSpec((tm, d), lambda i, f: (i, 0)),
        out_shape=jax.ShapeDtypeStruct((m, d), F32),
        scratch_shapes=[pltpu.VMEM((tm, d), BF16), pltpu.VMEM((tm, d), F32)],
        compiler_params=_cparams(("parallel", "arbitrary")),
        name="ffn",
    )(x, gpre.reshape(1, d), w_up, w_down, gpost.reshape(1, d))


def _split_w_in(w):
    z0 = A_Q + 2 * A_KV + 2 * B_QK + 2 * B_V
    z1 = z0 + 2 * GATE_RANK
    wt = jnp.swapaxes(w, 1, 2).astype(BF16)
    gates = jnp.pad(wt[:, z0:z1], ((0, 0), (0, LANE - 2 * GATE_RANK), (0, 0)))
    return wt[:, :z0], wt[:, z1:], gates


def _gate_expansion(w2_f, w2_b):
    w = jnp.zeros((w2_f.shape[0], LANE, 2 * B_QK), F32)
    w = w.at[:, 0:GATE_RANK, 0:B_QK].set(w2_f)
    w = w.at[:, GATE_RANK:2 * GATE_RANK, B_QK:].set(w2_b)
    return w.astype(BF16)


def kernel(x, mem, rel_bias, w_in, w_out, attn_sink, gla_w2_f, gla_b2_f, gla_w2_b, gla_b2_b, gla_norm, conv_w, conv_b, lru_wa, lru_ba, lru_wx, lru_bx, lru_lambda, xq, xk, xv, xo, w_up, w_down, norm_mix_pre, norm_mix_post, norm_mem, norm_x_pre, norm_x_post, norm_ff_pre, norm_ff_post):
    bsz, seq, d = x.shape
    assert d == D_MODEL and seq == SUBLANES * LRU_SEG
    xs = x.reshape(bsz * seq, d)
    mems = mem.reshape(bsz * mem.shape[1], d)
    bias_tab = _attention_bias_table(rel_bias)
    masks = _gla_masks()
    w_in_main, w_in_c, w_in_z = _split_w_in(w_in)
    w2_b = _gate_expansion(gla_w2_f, gla_w2_b)
    b2 = jnp.concatenate([gla_b2_f, gla_b2_b], axis=-1)
    bf = lambda w: w.astype(BF16)
    w_out_b, wa_b, wx_b = bf(w_out), bf(lru_wa), bf(lru_wx)
    xq_b, xk_b, xv_b, xo_b = bf(xq), bf(xk), bf(xv), bf(xo)
    w_up_b, w_down_b = bf(w_up), bf(w_down)
    for l in range(DEPTH):
        proj = _in_proj(xs, norm_mix_pre[l], w_in_main, w_in_c, w_in_z, l, tm=512)
        oa = _windowed_attention(proj, bias_tab, attn_sink[l], bsz, seq)
        ob = _gla(proj, w2_b, l, b2[l].reshape(1, 2 * B_QK), gla_norm[l].reshape(1, B_V), masks, bsz, seq)
        oc = _rglru(proj, conv_w[l], conv_b[l], wa_b, wx_b, l, lru_ba[l], lru_bx[l], lru_lambda[l],
                    bsz, seq)
        xs = _out_proj(oa, ob, oc, w_out_b, l, norm_mix_post[l], xs, tm=512)

        k, v = _kv_proj(mems, norm_mem[l], xk_b, xv_b, l, tm=512)
        xs = _cross_attention(xs, norm_x_pre[l], xq_b, k, v, xo_b, l, norm_x_post[l], bsz, seq, tq=512)

        xs = _ffn(xs, norm_ff_pre[l], w_up_b, w_down_b, l, norm_ff_post[l], tm=512, tf=1024)
    return xs.reshape(bsz, seq, d)
```

```python
import functools
import math

import numpy as np
import jax
import jax.numpy as jnp
from jax import lax
from jax.experimental import pallas as pl
from jax.experimental.pallas import tpu as pltpu

F32 = jnp.float32
BF16 = jnp.bfloat16

D_MODEL = 2048
DEPTH = 4
MEM_LEN = 256
A_HEAD_DIM = 128
A_HEADS = 8
A_KV_HEADS = 2
A_GROUP = 4
WINDOW = 128
BLOCK = 128
N_BUCKETS = 32
MAX_DISTANCE = 128
B_HEADS = 4
B_KEY_DIM = 64
B_VAL_DIM = 128
GATE_RANK = 16
GATE_TAU = 16.0
C_WIDTH = 512
C_BLOCKS = 4
C_BLOCK_DIM = 128
CONV_WIDTH = 4
LRU_C = 8.0
X_HEADS = 4
X_HEAD_DIM = 512
D_FF = 8192
EPS = 1e-6
NEG_INF = -1e30
LOG2E = math.log2(math.e)

A_Q = A_HEADS * A_HEAD_DIM
A_KV = A_KV_HEADS * A_HEAD_DIM
B_QK = B_HEADS * B_KEY_DIM
B_V = B_HEADS * B_VAL_DIM

LANE = 128
COL_AQ = 0
COL_AK = COL_AQ + A_Q
COL_AV = COL_AK + A_KV
COL_BQ = COL_AV + A_KV
COL_BK = COL_BQ + B_QK
COL_BV = COL_BK + B_QK
COL_BG = COL_BV + B_V
COL_CX = COL_BG + B_V
COL_CY = COL_CX + C_WIDTH
COL_Z = COL_CY + C_WIDTH
D_PROJ = COL_Z + LANE

GLA_BLOCK = 128
GLA_LEVELS = (1, 2, 4, 8, 16, 32, 64)
GLA_PAIRS = B_HEADS // 2
VMEM_LIMIT_MIB = 56


def _cparams(semantics, vmem_mib=VMEM_LIMIT_MIB):
    return pltpu.CompilerParams(dimension_semantics=semantics,
                                vmem_limit_bytes=vmem_mib * 1024 * 1024)


def _dot(a, b):
    return jnp.dot(a, b, preferred_element_type=F32)


def _dot_nt(a, b):
    return lax.dot_general(a, b, (((1,), (1,)), ((), ())), preferred_element_type=F32)


def _dot_tn(a, b):
    return lax.dot_general(a, b, (((0,), (0,)), ((), ())), preferred_element_type=F32)


def _softplus(x):
    return jnp.maximum(x, 0.0) + jnp.log(1.0 + jnp.exp(-jnp.abs(x)))


def _rms_scale(x):
    return lax.rsqrt(jnp.mean(jnp.square(x), axis=-1, keepdims=True) + EPS)


def _resident(shape, index_map):
    return pl.BlockSpec(shape, index_map, pipeline_mode=pl.Buffered(1))


def _kv_proj_kernel(x_ref, g_ref, wk_ref, wv_ref, k_ref, v_ref):
    x = x_ref[...]
    hn = (x * _rms_scale(x) * g_ref[...]).astype(BF16)
    k_ref[...] = _dot(hn, wk_ref[...]).astype(BF16)
    v_ref[...] = _dot(hn, wv_ref[...]).astype(BF16)


def _kv_proj(x, g, wk, wv, layer, tm):
    m, d = x.shape
    weight = _resident((None, d, d), lambda i: (layer, 0, 0))
    rows = pl.BlockSpec((tm, d), lambda i: (i, 0))
    return pl.pallas_call(
        _kv_proj_kernel,
        grid=(m // tm,),
        in_specs=[rows, pl.BlockSpec((1, d), lambda i: (0, 0)), weight, weight],
        out_specs=[rows, rows],
        out_shape=[jax.ShapeDtypeStruct((m, d), BF16)] * 2,
        compiler_params=_cparams(("parallel",)),
        name="kv_proj",
    )(x, g.reshape(1, d), wk, wv)


def _in_proj_kernel(x_ref, g_ref, wm_ref, wc_ref, wz_ref, o_ref):
    x = x_ref[...]
    hn = (x * _rms_scale(x) * g_ref[...]).astype(BF16)
    o_ref[:, 0:COL_CX] = _dot_nt(hn, wm_ref[...])
    o_ref[:, COL_CX:COL_Z] = _dot_nt(hn, wc_ref[...])
    o_ref[:, COL_Z:D_PROJ] = _dot_nt(hn, wz_ref[...])


def _in_proj(x, g, w_main, w_c, w_z, layer, tm):
    m, k = x.shape

    def weight(w):
        return _resident((None, w.shape[1], k), lambda i: (layer, 0, 0))

    return pl.pallas_call(
        _in_proj_kernel,
        grid=(m // tm,),
        in_specs=[pl.BlockSpec((tm, k), lambda i: (i, 0)), pl.BlockSpec((1, k), lambda i: (0, 0)),
                  weight(w_main), weight(w_c), weight(w_z)],
        out_specs=pl.BlockSpec((tm, D_PROJ), lambda i: (i, 0)),
        out_shape=jax.ShapeDtypeStruct((m, D_PROJ), F32),
        compiler_params=_cparams(("parallel",)),
        name="in_proj",
    )(x, g.reshape(1, k), w_main, w_c, w_z)


def _attn_kernel(sink_ref, q_ref, k_ref, v_ref, bias_ref, o_ref, kb_ref, vt_ref, p_ref, sw_ref):
    seq = q_ref.shape[0]
    nblk = seq // BLOCK
    kb_ref[...] = k_ref[...].astype(BF16)
    for g in range(A_KV_HEADS):
        for n in range(nblk):
            vt_ref[g * nblk + n] = (
                v_ref[n * BLOCK:(n + 1) * BLOCK, g * A_HEAD_DIM:(g + 1) * A_HEAD_DIM].T.astype(BF16))
    sinks = [jnp.concatenate([jnp.full((1, BLOCK), sink_ref[g * A_GROUP + h] * LOG2E, F32)
                              for h in range(A_GROUP)], axis=1) for g in range(A_KV_HEADS)]

    def tiles(first, last):
        return (1 if first else 0), (2 if last else 3)

    def scores(g, n, first, last):
        t0, t1 = tiles(first, last)
        nk = (t1 - t0) * BLOCK
        r0 = pl.multiple_of(n * BLOCK, BLOCK)
        k = kb_ref[pl.ds(pl.multiple_of(r0 + (t0 - 1) * BLOCK, BLOCK), nk),
                   g * A_HEAD_DIM:(g + 1) * A_HEAD_DIM]
        q = jnp.concatenate(
            [q_ref[pl.ds(r0, BLOCK), (g * A_GROUP + h) * A_HEAD_DIM:(g * A_GROUP + h + 1) * A_HEAD_DIM]
             for h in range(A_GROUP)], axis=0).astype(BF16)
        s = _dot_nt(k, q) * (A_HEAD_DIM ** -0.5 * LOG2E) + bias_ref[g, t0 * BLOCK:t1 * BLOCK, :]
        m = jnp.maximum(jnp.max(s, axis=0, keepdims=True), sinks[g])
        p_ref[g, 0:nk, :] = jnp.exp2(s - m).astype(BF16)
        sw_ref[g] = jnp.exp2(sinks[g] - m)

    def output(g, n, first, last):
        t0, t1 = tiles(first, last)
        nk = (t1 - t0) * BLOCK
        r0 = pl.multiple_of(n * BLOCK, BLOCK)
        v_t = jnp.concatenate([vt_ref[g * nblk + n + t - 1] for t in range(t0, t1)], axis=1)
        ones = jnp.ones((2 * SUBLANES, nk), BF16)
        o = _dot(jnp.concatenate([v_t, ones], axis=0), p_ref[g, 0:nk, :])
        denom = o[A_HEAD_DIM:A_HEAD_DIM + 1, :] + sw_ref[g]
        o = o[0:A_HEAD_DIM, :] / denom
        for h in range(A_GROUP):
            col = (g * A_GROUP + h) * A_HEAD_DIM
            o_ref[pl.ds(r0, BLOCK), col:col + A_HEAD_DIM] = (
                o[:, h * BLOCK:(h + 1) * BLOCK].T.astype(o_ref.dtype))

    def both(fn, *args):
        for g in range(A_KV_HEADS):
            fn(g, *args)

    def body(n, carry):
        both(output, n, False, False)
        both(scores, n + 1, False, False)
        return carry

    both(scores, 0, True, False)
    both(output, 0, True, False)
    both(scores, 1, False, False)
    lax.fori_loop(1, nblk - 2, body, 0)
    both(output, nblk - 2, False, False)
    both(scores, nblk - 1, False, True)
    both(output, nblk - 1, False, True)


def _windowed_attention(proj, bias_tab, sink, bsz, seq):
    qw = A_GROUP * A_HEAD_DIM
    nblk = seq // BLOCK
    return pl.pallas_call(
        _attn_kernel,
        grid=(bsz,),
        in_specs=[pl.BlockSpec(memory_space=pltpu.SMEM),
                  pl.BlockSpec((seq, A_Q), lambda b: (b, COL_AQ // A_Q)),
                  pl.BlockSpec((seq, A_KV), lambda b: (b, COL_AK // A_KV)),
                  pl.BlockSpec((seq, A_KV), lambda b: (b, COL_AV // A_KV)),
                  pl.BlockSpec(bias_tab.shape, lambda b: (0, 0, 0))],
        out_specs=pl.BlockSpec((seq, A_Q), lambda b: (b, 0)),
        out_shape=jax.ShapeDtypeStruct((bsz * seq, A_Q), BF16),
        scratch_shapes=[pltpu.VMEM((seq, A_KV), BF16),
                        pltpu.VMEM((A_KV_HEADS * nblk, A_HEAD_DIM, BLOCK), BF16),
                        pltpu.VMEM((A_KV_HEADS, 3 * BLOCK, qw), BF16),
                        pltpu.VMEM((A_KV_HEADS, 1, qw), F32)],
        compiler_params=_cparams(("parallel",)),
        name="windowed_attention",
    )(sink, proj, proj, proj, bias_tab)


def _t5_bucket(rel):
    nb = N_BUCKETS // 2
    max_exact = nb // 2
    ret = jnp.where(rel > 0, nb, 0)
    n = jnp.abs(rel)
    nf = jnp.maximum(n, 1).astype(F32)
    large = max_exact + (jnp.log(nf / max_exact) / math.log(MAX_DISTANCE / max_exact)
                         * (nb - max_exact)).astype(jnp.int32)
    large = jnp.minimum(large, nb - 1)
    return ret + jnp.where(n < max_exact, n, large)


def _attention_bias_table(rel_bias):
    kj = jnp.arange(3 * BLOCK)[:, None]
    qi = jnp.arange(BLOCK)[None, :]
    rel = kj - BLOCK - qi
    onehot = jax.nn.one_hot(_t5_bucket(rel), N_BUCKETS, dtype=F32)
    bias = jnp.einsum("kqn,nh->hkq", onehot, rel_bias.astype(F32), precision=lax.Precision.HIGHEST)
    bias = jnp.where((jnp.abs(rel) <= WINDOW)[None], bias * LOG2E, NEG_INF)
    bias = bias.reshape(A_KV_HEADS, A_GROUP, 3 * BLOCK, BLOCK)
    return jnp.transpose(bias, (0, 2, 1, 3)).reshape(A_KV_HEADS, 3 * BLOCK, A_GROUP * BLOCK)


def _gla_masks():
    c = GLA_BLOCK
    i = np.arange(c)[:, None]
    j = np.arange(c)[None, :]
    fwd = [i == j]
    for s in GLA_LEVELS:
        fwd.append((i // (2 * s) == j // (2 * s)) & ((i // s) % 2 == 1) & ((j // s) % 2 == 0))
    fwd = np.stack(fwd)
    return jnp.asarray(np.stack([fwd, fwd.transpose(0, 2, 1)]).astype(np.float32))


def _gla_block(direction, rows, q_ref, k_ref, v_ref, z_ref, w2_ref, b2_ref, m_ref, st_ref):
    c = GLA_BLOCK
    q = q_ref[rows, :] * (B_KEY_DIM ** -0.5)
    k = k_ref[rows, :]
    v = v_ref[rows, :].astype(BF16)
    gate_cols = slice(direction * B_QK, (direction + 1) * B_QK)
    pre = _dot(z_ref[rows, :].astype(BF16), w2_ref[:, gate_cols]) + b2_ref[:, gate_cols]
    log_a = -_softplus(-pre) * (LOG2E / GATE_TAU)

    row = lax.broadcasted_iota(jnp.int32, (c, B_QK), 0)
    first_head = lax.broadcasted_iota(jnp.int32, (c, LANE), 1) < B_KEY_DIM
    pair = [slice(p * LANE, (p + 1) * LANE) for p in range(GLA_PAIRS)]
    k_first = [jnp.where(first_head, k[:, ps], 0.0) for ps in pair]
    k_second = [jnp.where(first_head, 0.0, k[:, ps]) for ps in pair]

    def exponents(p_s, t_s):
        if direction == 0:
            return p_s, t_s - p_s
        return t_s - p_s + log_a, p_s - log_a

    def scaled(e_q, e_k, p):
        w_q = jnp.exp2(e_q[:, pair[p]])
        w_k = jnp.exp2(e_k[:, pair[p]])
        q_t = (q[:, pair[p]] * w_q).astype(BF16)
        k_t = jnp.concatenate([(k_first[p] * w_k).astype(BF16), (k_second[p] * w_k).astype(BF16)], axis=0)
        return q_t, k_t

    attn = [None] * GLA_PAIRS

    def add_level(level, q_t, k_t, p):
        mask = m_ref[direction, level]
        contrib = jnp.concatenate([mask, mask], axis=1) * _dot_nt(q_t, k_t)
        attn[p] = contrib if attn[p] is None else attn[p] + contrib

    if direction == 0:
        for p in range(GLA_PAIRS):
            k_t = jnp.concatenate([k_first[p].astype(BF16), k_second[p].astype(BF16)], axis=0)
            add_level(0, q[:, pair[p]].astype(BF16), k_t, p)
    p_s = log_a
    t_s = log_a
    for li, s in enumerate(GLA_LEVELS):
        e_q, e_k = exponents(p_s, t_s)
        for p in range(GLA_PAIRS):
            q_t, k_t = scaled(e_q, e_k, p)
            add_level(li + 1, q_t, k_t, p)
        upper = (row & s) != 0
        t_before = pltpu.roll(t_s, s, 0)
        t_after = pltpu.roll(t_s, c - s, 0)
        p_s = p_s + jnp.where(upper, t_before, 0.0)
        t_s = t_s + jnp.where(upper, t_before, t_after)

    e_q, e_k = exponents(p_s, t_s)
    state_decay = jnp.exp2(t_s[0:1, :])
    own_block = lax.broadcasted_iota(jnp.int32, (2 * B_VAL_DIM, LANE), 0) < B_VAL_DIM
    own_block = own_block == (lax.broadcasted_iota(jnp.int32, (2 * B_VAL_DIM, LANE), 1) < B_KEY_DIM)
    zero_v = jnp.zeros((c, B_VAL_DIM), BF16)
    outs = []
    for p in range(GLA_PAIRS):
        v_pair = v[:, 2 * p * B_VAL_DIM:(2 * p + 2) * B_VAL_DIM]
        v_diag = jnp.concatenate(
            [jnp.concatenate([v_pair[:, :B_VAL_DIM], zero_v], axis=1),
             jnp.concatenate([zero_v, v_pair[:, B_VAL_DIM:]], axis=1)], axis=0)
        state_t = st_ref[direction, p]
        q_t = (q[:, pair[p]] * jnp.exp2(e_q[:, pair[p]])).astype(BF16)
        k_t = (k[:, pair[p]] * jnp.exp2(e_k[:, pair[p]])).astype(BF16)
        o = _dot_nt(q_t, state_t.astype(BF16)) + _dot(attn[p].astype(BF16), v_diag)
        update = jnp.where(own_block, _dot_tn(v_pair, k_t), 0.0)
        st_ref[direction, p] = state_t * state_decay[:, pair[p]] + update
        outs.append(o)
    return jnp.concatenate(outs, axis=1)


def _gla_kernel(q_ref, k_ref, v_ref, g_ref, z_ref, w2_ref, b2_ref, gn_ref, m_ref,
                o_ref, of_ref, ob_ref, st_ref):
    nblk = q_ref.shape[0] // GLA_BLOCK
    st_ref[...] = jnp.zeros(st_ref.shape, F32)
    block = functools.partial(_gla_block, q_ref=q_ref, k_ref=k_ref, v_ref=v_ref, z_ref=z_ref,
                              w2_ref=w2_ref, b2_ref=b2_ref, m_ref=m_ref, st_ref=st_ref)

    def rows_of(n):
        return pl.ds(pl.multiple_of(n * GLA_BLOCK, GLA_BLOCK), GLA_BLOCK)

    def finish(rows, o):
        normed = []
        for h in range(B_HEADS):
            oh = o[:, h * B_VAL_DIM:(h + 1) * B_VAL_DIM]
            normed.append(oh * _rms_scale(oh))
        g = g_ref[rows, :]
        out = jnp.concatenate(normed, axis=1) * gn_ref[...] * (g * jax.nn.sigmoid(g))
        o_ref[rows, :] = out.astype(o_ref.dtype)

    def first_half(n, carry):
        rows_f, rows_b = rows_of(n), rows_of(nblk - 1 - n)
        of_ref[rows_f, :] = block(0, rows_f)
        ob_ref[rows_b, :] = block(1, rows_b)
        return carry

    def second_half(n, carry):
        rows_f, rows_b = rows_of(n), rows_of(nblk - 1 - n)
        finish(rows_f, block(0, rows_f) + ob_ref[rows_f, :])
        finish(rows_b, of_ref[rows_b, :] + block(1, rows_b))
        return carry

    lax.fori_loop(0, nblk // 2, first_half, 0)
    lax.fori_loop(nblk // 2, nblk, second_half, 0)


def _gla(proj, w2, layer, b2, gn, masks, bsz, seq):
    def col(width, start):
        return pl.BlockSpec((seq, width), lambda b: (b, start // width))

    def whole(a):
        return pl.BlockSpec(a.shape, lambda b: (0,) * a.ndim)

    return pl.pallas_call(
        _gla_kernel,
        grid=(bsz,),
        in_specs=[col(B_QK, COL_BQ), col(B_QK, COL_BK), col(B_V, COL_BV), col(B_V, COL_BG),
                  col(LANE, COL_Z), pl.BlockSpec((None,) + w2.shape[1:], lambda b: (layer, 0, 0)),
                  whole(b2), whole(gn), whole(masks)],
        out_specs=pl.BlockSpec((seq, B_V), lambda b: (b, 0)),
        out_shape=jax.ShapeDtypeStruct((bsz * seq, B_V), BF16),
        scratch_shapes=[pltpu.VMEM((seq, B_V), F32), pltpu.VMEM((seq, B_V), F32),
                        pltpu.VMEM((2, GLA_PAIRS, 2 * B_VAL_DIM, LANE), F32)],
        compiler_params=_cparams(("parallel",)),
        name="gla",
    )(proj, proj, proj, proj, proj, w2, b2, gn, masks)


LRU_SEG = 256
SUBLANES = 8
LRU_PITCH = LRU_SEG + SUBLANES


def _lru_kernel(x_ref, y_ref, cw_ref, cb_ref, wa_ref, wx_ref, ba_ref, bx_ref, lam_ref, o_ref,
                xp_ref, af_ref, hf_ref, ab_ref, hb_ref):
    seq = x_ref.shape[0]
    ch = x_ref.shape[1]
    nseg = seq // LRU_SEG
    halo = SUBLANES
    xp_ref[0:halo, :] = jnp.zeros((halo, ch), F32)
    xp_ref[seq + halo:seq + 2 * halo, :] = jnp.zeros((halo, ch), F32)
    xp_ref[halo:seq + halo, :] = x_ref[...]
    decay_rate = _softplus(-lam_ref[...])
    scan_refs = ((af_ref, hf_ref), (ab_ref, hb_ref))

    def gates(c, carry):
        r0 = pl.multiple_of(c * LRU_SEG, LRU_SEG)
        dst = pl.ds(pl.multiple_of(c * LRU_PITCH, SUBLANES), LRU_SEG)
        ext = LRU_SEG + 2 * halo
        xe = xp_ref[pl.ds(r0, ext), :]
        xc = (cw_ref[0:1, :] * pltpu.roll(xe, 2, 0) + cw_ref[1:2, :] * pltpu.roll(xe, 1, 0)
              + cw_ref[2:3, :] * xe + cw_ref[3:4, :] * pltpu.roll(xe, ext - 1, 0))
        xc = xc[halo:halo + LRU_SEG] + cb_ref[...]
        xb = xc.astype(BF16)
        for d in range(2):
            r = jax.nn.sigmoid(_dot(xb, wa_ref[d, 0]) + ba_ref[d:d + 1, :])
            i = jax.nn.sigmoid(_dot(xb, wx_ref[d, 0]) + bx_ref[d:d + 1, :])
            log_a = -LRU_C * r * decay_rate[d:d + 1, :]
            t = jnp.tanh(log_a)
            a_ref, u_ref = scan_refs[d]
            a_ref[dst, :] = jnp.exp(log_a)
            u_ref[dst, :] = jnp.sqrt(-2.0 * t / (1.0 - t)) * (i * xc)
        return carry

    lax.fori_loop(0, nseg, gates, 0)

    def scan_step(j, carry):
        hf, pf, hb, pb = carry
        rf = pl.ds(j, nseg, stride=LRU_PITCH)
        a = af_ref[rf, :]
        hf = a * hf + hf_ref[rf, :]
        pf = a * pf
        hf_ref[rf, :] = hf
        af_ref[rf, :] = pf
        rb = pl.ds(LRU_SEG - 1 - j, nseg, stride=LRU_PITCH)
        a = ab_ref[rb, :]
        hb = a * hb + hb_ref[rb, :]
        pb = a * pb
        hb_ref[rb, :] = hb
        ab_ref[rb, :] = pb
        return hf, pf, hb, pb

    zeros = jnp.zeros((nseg, ch), F32)
    ones = jnp.ones((nseg, ch), F32)
    lax.fori_loop(0, LRU_SEG, scan_step, (zeros, ones, zeros, ones), unroll=4)

    enter_f = [None] * nseg
    state = jnp.zeros((1, ch), F32)
    for s in range(nseg):
        enter_f[s] = state
        last = s * LRU_PITCH + LRU_SEG - 1
        state = hf_ref[last:last + 1, :] + af_ref[last:last + 1, :] * state
    enter_b = [None] * nseg
    state = jnp.zeros((1, ch), F32)
    for s in reversed(range(nseg)):
        enter_b[s] = state
        first = s * LRU_PITCH
        state = hb_ref[first:first + 1, :] + ab_ref[first:first + 1, :] * state

    for s in range(nseg):
        seg = slice(s * LRU_PITCH, s * LRU_PITCH + LRU_SEG)
        rows = slice(s * LRU_SEG, (s + 1) * LRU_SEG)
        h = (hf_ref[seg, :] + af_ref[seg, :] * enter_f[s]) + (hb_ref[seg, :] + ab_ref[seg, :] * enter_b[s])
        o_ref[rows, :] = (h * jax.nn.gelu(y_ref[rows, :])).astype(o_ref.dtype)


def _rglru(proj, conv_w, conv_b, w_a, w_x, layer, b_a, b_x, lam, bsz, seq):
    ch = C_BLOCK_DIM

    def col(start):
        return pl.BlockSpec((seq, ch), lambda b, c: (b, start // ch + c))

    def per_block(rows):
        return pl.BlockSpec((rows, ch), lambda b, c: (0, c))

    gate_w = pl.BlockSpec((None, 2, 1, ch, ch), lambda b, c: (layer, 0, c, 0, 0))
    return pl.pallas_call(
        _lru_kernel,
        grid=(bsz, C_BLOCKS),
        in_specs=[col(COL_CX), col(COL_CY), per_block(CONV_WIDTH), per_block(1), gate_w, gate_w,
                  per_block(2), per_block(2), per_block(2)],
        out_specs=pl.BlockSpec((seq, ch), lambda b, c: (b, c)),
        out_shape=jax.ShapeDtypeStruct((bsz * seq, C_WIDTH), BF16),
        scratch_shapes=([pltpu.VMEM((seq + 2 * SUBLANES, ch), F32)]
                        + [pltpu.VMEM((seq // LRU_SEG * LRU_PITCH, ch), F32)] * 4),
        compiler_params=_cparams(("parallel", "parallel")),
        name="rglru",
    )(proj, proj, conv_w, conv_b.reshape(1, C_WIDTH), w_a, w_x, b_a, b_x, lam)


def _out_proj_kernel(oa_ref, ob_ref, oc_ref, w_ref, g_ref, x_ref, o_ref):
    mixed = (_dot(oa_ref[...], w_ref[0:A_Q, :]) + _dot(ob_ref[...], w_ref[A_Q:A_Q + B_V, :])
             + _dot(oc_ref[...], w_ref[A_Q + B_V:A_Q + B_V + C_WIDTH, :]))
    o_ref[...] = x_ref[...] + mixed * _rms_scale(mixed) * g_ref[...]


def _out_proj(oa, ob, oc, w, layer, g, x, tm):
    m, d = x.shape

    def rows(width):
        return pl.BlockSpec((tm, width), lambda i: (i, 0))

    return pl.pallas_call(
        _out_proj_kernel,
        grid=(m // tm,),
        in_specs=[rows(A_Q), rows(B_V), rows(C_WIDTH),
                  pl.BlockSpec((None,) + w.shape[1:], lambda i: (layer, 0, 0)),
                  pl.BlockSpec((1, d), lambda i: (0, 0)),
                  rows(d)],
        out_specs=rows(d),
        out_shape=jax.ShapeDtypeStruct((m, d), F32),
        compiler_params=_cparams(("parallel",)),
        name="out_proj",
    )(oa, ob, oc, w, g.reshape(1, d), x)


def _cross_attn_kernel(x_ref, gpre_ref, wq_ref, k_ref, v_ref, wo_ref, gpost_ref, o_ref):
    x = x_ref[...]
    hn = (x * _rms_scale(x) * gpre_ref[...]).astype(BF16)
    q = _dot(hn, wq_ref[...]).astype(BF16)
    acc = None
    for h in range(X_HEADS):
        hs = slice(h * X_HEAD_DIM, (h + 1) * X_HEAD_DIM)
        s = _dot_nt(q[:, hs], k_ref[:, hs]) * (X_HEAD_DIM ** -0.5)
        m = jnp.max(s, axis=-1, keepdims=True)
        p = jnp.exp(s - m)
        o = _dot(p.astype(BF16), v_ref[:, hs]) / jnp.sum(p, axis=-1, keepdims=True)
        part = _dot(o.astype(BF16), wo_ref[hs, :])
        acc = part if acc is None else acc + part
    o_ref[...] = x + acc * _rms_scale(acc) * gpost_ref[...]


def _cross_attention(x, gpre, wq, k, v, wo, layer, gpost, bsz, seq, tq):
    d = D_MODEL
    per_row = seq // tq
    mem_spec = pl.BlockSpec((MEM_LEN, d), lambda b, i: (b, 0))
    gain_spec = pl.BlockSpec((1, d), lambda b, i: (0, 0))
    weight_spec = _resident((None, d, d), lambda b, i: (layer, 0, 0))
    row_spec = pl.BlockSpec((tq, d), lambda b, i: (b * per_row + i, 0))
    return pl.pallas_call(
        _cross_attn_kernel,
        grid=(bsz, per_row),
        in_specs=[row_spec, gain_spec, weight_spec, mem_spec, mem_spec, weight_spec, gain_spec],
        out_specs=row_spec,
        out_shape=jax.ShapeDtypeStruct((bsz * seq, d), F32),
        compiler_params=_cparams(("parallel", "parallel")),
        name="cross_attention",
    )(x, gpre.reshape(1, d), wq, k, v, wo, gpost.reshape(1, d))


def _ffn_kernel(x_ref, gpre_ref, wu_ref, wd_ref, gpost_ref, o_ref, hn_ref, acc_ref):
    f = pl.program_id(1)

    @pl.when(f == 0)
    def _():
        x = x_ref[...]
        hn_ref[...] = (x * _rms_scale(x) * gpre_ref[...]).astype(BF16)
        acc_ref[...] = jnp.zeros(acc_ref.shape, F32)

    up = _dot(hn_ref[...], wu_ref[...])
    act = jnp.square(jnp.maximum(up, 0.0)).astype(BF16)
    acc_ref[...] += _dot(act, wd_ref[...])

    @pl.when(f == pl.num_programs(1) - 1)
    def _():
        ff = acc_ref[...]
        o_ref[...] = x_ref[...] + ff * _rms_scale(ff) * gpost_ref[...]


def _ffn(x, gpre, w_up, w_down, layer, gpost, tm, tf):
    m, d = x.shape
    dff = w_up.shape[2]
    return pl.pallas_call(
        _ffn_kernel,
        grid=(m // tm, dff // tf),
        in_specs=[pl.BlockSpec((tm, d), lambda i, f: (i, 0)),
                  pl.BlockSpec((1, d), lambda i, f: (0, 0)),
                  pl.BlockSpec((None, d, tf), lambda i, f: (layer, 0, f)),
                  pl.BlockSpec((None, tf, d), lambda i, f: (layer, f, 0)),
                  pl.BlockSpec((1, d), lambda i, f: (0, 0))],
        out_specs=pl.BlockSpec((tm, d), lambda i, f: (i, 0)),
        out_shape=jax.ShapeDtypeStruct((m, d), F32),
        scratch_shapes=[pltpu.VMEM((tm, d), BF16), pltpu.VMEM((tm, d), F32)],
        compiler_params=_cparams(("parallel", "arbitrary")),
        name="ffn",
    )(x, gpre.reshape(1, d), w_up, w_down, gpost.reshape(1, d))


def _split_w_in(w):
    z0 = A_Q + 2 * A_KV + 2 * B_QK + 2 * B_V
    z1 = z0 + 2 * GATE_RANK
    wt = jnp.swapaxes(w, 1, 2).astype(BF16)
    gates = jnp.pad(wt[:, z0:z1], ((0, 0), (0, LANE - 2 * GATE_RANK), (0, 0)))
    return wt[:, :z0], wt[:, z1:], gates


def _gate_expansion(w2_f, w2_b):
    w = jnp.zeros((w2_f.shape[0], LANE, 2 * B_QK), F32)
    w = w.at[:, 0:GATE_RANK, 0:B_QK].set(w2_f)
    w = w.at[:, GATE_RANK:2 * GATE_RANK, B_QK:].set(w2_b)
    return w.astype(BF16)


def kernel(x, mem, rel_bias, w_in, w_out, attn_sink, gla_w2_f, gla_b2_f, gla_w2_b, gla_b2_b, gla_norm, conv_w, conv_b, lru_wa, lru_ba, lru_wx, lru_bx, lru_lambda, xq, xk, xv, xo, w_up, w_down, norm_mix_pre, norm_mix_post, norm_mem, norm_x_pre, norm_x_post, norm_ff_pre, norm_ff_post):
    bsz, seq, d = x.shape
    assert d == D_MODEL and seq == SUBLANES * LRU_SEG
    xs = x.reshape(bsz * seq, d)
    mems = mem.reshape(bsz * mem.shape[1], d)
    bias_tab = _attention_bias_table(rel_bias)
    masks = _gla_masks()
    w_in_main, w_in_c, w_in_z = _split_w_in(w_in)
    w2_b = _gate_expansion(gla_w2_f, gla_w2_b)
    b2 = jnp.concatenate([gla_b2_f, gla_b2_b], axis=-1)
    bf = lambda w: w.astype(BF16)
    w_out_b, wa_b, wx_b = bf(w_out), bf(lru_wa), bf(lru_wx)
    xq_b, xk_b, xv_b, xo_b = bf(xq), bf(xk), bf(xv), bf(xo)
    w_up_b, w_down_b = bf(w_up), bf(w_down)
    for l in range(DEPTH):
        proj = _in_proj(xs, norm_mix_pre[l], w_in_main, w_in_c, w_in_z, l, tm=512)
        oa = _windowed_attention(proj, bias_tab, attn_sink[l], bsz, seq)
        ob = _gla(proj, w2_b, l, b2[l].reshape(1, 2 * B_QK), gla_norm[l].reshape(1, B_V), masks, bsz, seq)
        oc = _rglru(proj, conv_w[l], conv_b[l], wa_b, wx_b, l, lru_ba[l], lru_bx[l], lru_lambda[l],
                    bsz, seq)
        xs = _out_proj(oa, ob, oc, w_out_b, l, norm_mix_post[l], xs, tm=512)

        k, v = _kv_proj(mems, norm_mem[l], xk_b, xv_b, l, tm=512)
        xs = _cross_attention(xs, norm_x_pre[l], xq_b, k, v, xo_b, l, norm_x_post[l], bsz, seq, tq=512)

        xs = _ffn(xs, norm_ff_pre[l], w_up_b, w_down_b, l, norm_ff_post[l], tm=512, tf=1024)
    return xs.reshape(bsz, seq, d)
```

```python
import functools
import math

import numpy as np
import jax
import jax.numpy as jnp
from jax import lax
from jax.experimental import pallas as pl
from jax.experimental.pallas import tpu as pltpu

F32 = jnp.float32
BF16 = jnp.bfloat16

D_MODEL = 2048
DEPTH = 4
MEM_LEN = 256
A_HEAD_DIM = 128
A_HEADS = 8
A_KV_HEADS = 2
A_GROUP = 4
WINDOW = 128
BLOCK = 128
N_BUCKETS = 32
MAX_DISTANCE = 128
B_HEADS = 4
B_KEY_DIM = 64
B_VAL_DIM = 128
GATE_RANK = 16
GATE_TAU = 16.0
C_WIDTH = 512
C_BLOCKS = 4
C_BLOCK_DIM = 128
CONV_WIDTH = 4
LRU_C = 8.0
X_HEADS = 4
X_HEAD_DIM = 512
D_FF = 8192
EPS = 1e-6
NEG_INF = -1e30
LOG2E = math.log2(math.e)

A_Q = A_HEADS * A_HEAD_DIM
A_KV = A_KV_HEADS * A_HEAD_DIM
B_QK = B_HEADS * B_KEY_DIM
B_V = B_HEADS * B_VAL_DIM

LANE = 128
COL_AQ = 0
COL_AK = COL_AQ + A_Q
COL_AV = COL_AK + A_KV
COL_BQ = COL_AV + A_KV
COL_BK = COL_BQ + B_QK
COL_BV = COL_BK + B_QK
COL_BG = COL_BV + B_V
COL_CX = COL_BG + B_V
COL_CY = COL_CX + C_WIDTH
COL_Z = COL_CY + C_WIDTH
D_PROJ = COL_Z + LANE

GLA_BLOCK = 128
GLA_LEVELS = (1, 2, 4, 8, 16, 32, 64)
GLA_PAIRS = B_HEADS // 2
VMEM_LIMIT_MIB = 56


def _cparams(semantics, vmem_mib=VMEM_LIMIT_MIB):
    return pltpu.CompilerParams(dimension_semantics=semantics,
                                vmem_limit_bytes=vmem_mib * 1024 * 1024)


def _dot(a, b):
    return jnp.dot(a, b, preferred_element_type=F32)


def _dot_nt(a, b):
    return lax.dot_general(a, b, (((1,), (1,)), ((), ())), preferred_element_type=F32)


def _dot_tn(a, b):
    return lax.dot_general(a, b, (((0,), (0,)), ((), ())), preferred_element_type=F32)


def _softplus(x):
    return jnp.maximum(x, 0.0) + jnp.log(1.0 + jnp.exp(-jnp.abs(x)))


def _rms_scale(x):
    return lax.rsqrt(jnp.mean(jnp.square(x), axis=-1, keepdims=True) + EPS)


ROW_CHAINS = 2


def _row_groups(rows):
    step = rows // ROW_CHAINS
    return [slice(c * step, (c + 1) * step) for c in range(ROW_CHAINS)]


def _resident(shape, index_map):
    return pl.BlockSpec(shape, index_map, pipeline_mode=pl.Buffered(1))


WEIGHT_CHUNK_ROWS = 256


def _weight_scratch(k, n):
    return [pltpu.VMEM((k, n), BF16), pltpu.VMEM((2, WEIGHT_CHUNK_ROWS, n), F32),
            pltpu.SemaphoreType.DMA((2,))]


def _fetch_weight(w_hbm, layer, dst_ref, stage_ref, sem_ref):
    nchunk = dst_ref.shape[0] // WEIGHT_CHUNK_ROWS

    def copy(c):
        rows = pl.ds(c * WEIGHT_CHUNK_ROWS, WEIGHT_CHUNK_ROWS)
        return pltpu.make_async_copy(w_hbm.at[layer, rows, :], stage_ref.at[c % 2], sem_ref.at[c % 2])

    copy(0).start()
    for c in range(nchunk):
        if c + 1 < nchunk:
            copy(c + 1).start()
        copy(c).wait()
        dst_ref[c * WEIGHT_CHUNK_ROWS:(c + 1) * WEIGHT_CHUNK_ROWS, :] = stage_ref[c % 2].astype(BF16)


def _kv_proj_kernel(x_ref, g_ref, wk_hbm, wv_hbm, k_ref, v_ref, wk_ref, wv_ref, stage_ref, sem_ref, *, layer):
    @pl.when(pl.program_id(0) == 0)
    def _():
        _fetch_weight(wk_hbm, layer, wk_ref, stage_ref, sem_ref)
        _fetch_weight(wv_hbm, layer, wv_ref, stage_ref, sem_ref)

    x = x_ref[...]
    hn = (x * _rms_scale(x) * g_ref[...]).astype(BF16)
    k_ref[...] = _dot(hn, wk_ref[...]).astype(BF16)
    v_ref[...] = _dot(hn, wv_ref[...]).astype(BF16)


def _kv_proj(x, g, wk, wv, layer, tm):
    m, d = x.shape
    weight = pl.BlockSpec(memory_space=pl.ANY)
    rows = pl.BlockSpec((tm, d), lambda i: (i, 0))
    w_bf, stage, sem = _weight_scratch(d, d)
    return pl.pallas_call(
        functools.partial(_kv_proj_kernel, layer=layer),
        grid=(m // tm,),
        in_specs=[rows, pl.BlockSpec((1, d), lambda i: (0, 0)), weight, weight],
        out_specs=[rows, rows],
        out_shape=[jax.ShapeDtypeStruct((m, d), BF16)] * 2,
        scratch_shapes=[w_bf, w_bf, stage, sem],
        compiler_params=_cparams(("arbitrary",)),
        name="kv_proj",
    )(x, g.reshape(1, d), wk, wv)


def _in_proj_kernel(x_ref, g_ref, wm_ref, wc_ref, wz_ref, o_ref):
    for r in _row_groups(x_ref.shape[0]):
        x = x_ref[r, :]
        hn = (x * _rms_scale(x) * g_ref[...]).astype(BF16)
        o_ref[r, 0:COL_CX] = _dot_nt(hn, wm_ref[...])
        o_ref[r, COL_CX:COL_Z] = _dot_nt(hn, wc_ref[...])
        o_ref[r, COL_Z:D_PROJ] = _dot_nt(hn, wz_ref[...])


def _in_proj(x, g, w_main, w_c, w_z, layer, tm):
    m, k = x.shape

    def weight(w):
        return _resident((None, w.shape[1], k), lambda i: (layer, 0, 0))

    return pl.pallas_call(
        _in_proj_kernel,
        grid=(m // tm,),
        in_specs=[pl.BlockSpec((tm, k), lambda i: (i, 0)), pl.BlockSpec((1, k), lambda i: (0, 0)),
                  weight(w_main), weight(w_c), weight(w_z)],
        out_specs=pl.BlockSpec((tm, D_PROJ), lambda i: (i, 0)),
        out_shape=jax.ShapeDtypeStruct((m, D_PROJ), F32),
        compiler_params=_cparams(("parallel",)),
        name="in_proj",
    )(x, g.reshape(1, k), w_main, w_c, w_z)


def _attn_kernel(sink_ref, q_ref, k_ref, v_ref, bias_ref, o_ref, kb_ref, vt_ref, p_ref, sw_ref):
    seq = q_ref.shape[0]
    nblk = seq // BLOCK
    kb_ref[...] = k_ref[...].astype(BF16)
    for g in range(A_KV_HEADS):
        for n in range(nblk):
            vt_ref[g * nblk + n] = (
                v_ref[n * BLOCK:(n + 1) * BLOCK, g * A_HEAD_DIM:(g + 1) * A_HEAD_DIM].T.astype(BF16))
    sinks = [jnp.concatenate([jnp.full((1, BLOCK), sink_ref[g * A_GROUP + h] * LOG2E, F32)
                              for h in range(A_GROUP)], axis=1) for g in range(A_KV_HEADS)]

    def tiles(first, last):
        return (1 if first else 0), (2 if last else 3)

    def scores(g, n, first, last):
        t0, t1 = tiles(first, last)
        nk = (t1 - t0) * BLOCK
        r0 = pl.multiple_of(n * BLOCK, BLOCK)
        k = kb_ref[pl.ds(pl.multiple_of(r0 + (t0 - 1) * BLOCK, BLOCK), nk),
                   g * A_HEAD_DIM:(g + 1) * A_HEAD_DIM]
        q = jnp.concatenate(
            [q_ref[pl.ds(r0, BLOCK), (g * A_GROUP + h) * A_HEAD_DIM:(g * A_GROUP + h + 1) * A_HEAD_DIM]
             for h in range(A_GROUP)], axis=0).astype(BF16)
        s = _dot_nt(k, q) * (A_HEAD_DIM ** -0.5 * LOG2E) + bias_ref[g, t0 * BLOCK:t1 * BLOCK, :]
        m = jnp.maximum(jnp.max(s, axis=0, keepdims=True), sinks[g])
        p_ref[g, 0:nk, :] = jnp.exp2(s - m).astype(BF16)
        sw_ref[g] = jnp.exp2(sinks[g] - m)

    def output(g, n, first, last):
        t0, t1 = tiles(first, last)
        nk = (t1 - t0) * BLOCK
        r0 = pl.multiple_of(n * BLOCK, BLOCK)
        v_t = jnp.concatenate([vt_ref[g * nblk + n + t - 1] for t in range(t0, t1)], axis=1)
        ones = jnp.ones((2 * SUBLANES, nk), BF16)
        o = _dot(jnp.concatenate([v_t, ones], axis=0), p_ref[g, 0:nk, :])
        denom = o[A_HEAD_DIM:A_HEAD_DIM + 1, :] + sw_ref[g]
        o = o[0:A_HEAD_DIM, :] / denom
        for h in range(A_GROUP):
            col = (g * A_GROUP + h) * A_HEAD_DIM
            o_ref[pl.ds(r0, BLOCK), col:col + A_HEAD_DIM] = (
                o[:, h * BLOCK:(h + 1) * BLOCK].T.astype(o_ref.dtype))

    def both(fn, *args):
        for g in range(A_KV_HEADS):
            fn(g, *args)

    def body(n, carry):
        both(output, n, False, False)
        both(scores, n + 1, False, False)
        return carry

    both(scores, 0, True, False)
    both(output, 0, True, False)
    both(scores, 1, False, False)
    lax.fori_loop(1, nblk - 2, body, 0)
    both(output, nblk - 2, False, False)
    both(scores, nblk - 1, False, True)
    both(output, nblk - 1, False, True)


def _windowed_attention(proj, bias_tab, sink, bsz, seq):
    qw = A_GROUP * A_HEAD_DIM
    nblk = seq // BLOCK
    return pl.pallas_call(
        _attn_kernel,
        grid=(bsz,),
        in_specs=[pl.BlockSpec(memory_space=pltpu.SMEM),
                  pl.BlockSpec((seq, A_Q), lambda b: (b, COL_AQ // A_Q)),
                  pl.BlockSpec((seq, A_KV), lambda b: (b, COL_AK // A_KV)),
                  pl.BlockSpec((seq, A_KV), lambda b: (b, COL_AV // A_KV)),
                  pl.BlockSpec(bias_tab.shape, lambda b: (0, 0, 0))],
        out_specs=pl.BlockSpec((seq, A_Q), lambda b: (b, 0)),
        out_shape=jax.ShapeDtypeStruct((bsz * seq, A_Q), BF16),
        scratch_shapes=[pltpu.VMEM((seq, A_KV), BF16),
                        pltpu.VMEM((A_KV_HEADS * nblk, A_HEAD_DIM, BLOCK), BF16),
                        pltpu.VMEM((A_KV_HEADS, 3 * BLOCK, qw), BF16),
                        pltpu.VMEM((A_KV_HEADS, 1, qw), F32)],
        compiler_params=_cparams(("parallel",)),
        name="windowed_attention",
    )(sink, proj, proj, proj, bias_tab)


def _t5_bucket(rel):
    nb = N_BUCKETS // 2
    max_exact = nb // 2
    ret = jnp.where(rel > 0, nb, 0)
    n = jnp.abs(rel)
    nf = jnp.maximum(n, 1).astype(F32)
    large = max_exact + (jnp.log(nf / max_exact) / math.log(MAX_DISTANCE / max_exact)
                         * (nb - max_exact)).astype(jnp.int32)
    large = jnp.minimum(large, nb - 1)
    return ret + jnp.where(n < max_exact, n, large)


def _attention_bias_table(rel_bias):
    kj = jnp.arange(3 * BLOCK)[:, None]
    qi = jnp.arange(BLOCK)[None, :]
    rel = kj - BLOCK - qi
    onehot = jax.nn.one_hot(_t5_bucket(rel), N_BUCKETS, dtype=F32)
    bias = jnp.einsum("kqn,nh->hkq", onehot, rel_bias.astype(F32), precision=lax.Precision.HIGHEST)
    bias = jnp.where((jnp.abs(rel) <= WINDOW)[None], bias * LOG2E, NEG_INF)
    bias = bias.reshape(A_KV_HEADS, A_GROUP, 3 * BLOCK, BLOCK)
    return jnp.transpose(bias, (0, 2, 1, 3)).reshape(A_KV_HEADS, 3 * BLOCK, A_GROUP * BLOCK)


def _gla_masks():
    c = GLA_BLOCK
    i = np.arange(c)[:, None]
    j = np.arange(c)[None, :]
    fwd = [i == j]
    for s in GLA_LEVELS:
        fwd.append((i // (2 * s) == j // (2 * s)) & ((i // s) % 2 == 1) & ((j // s) % 2 == 0))
    fwd = np.stack(fwd)
    return jnp.asarray(np.stack([fwd, fwd.transpose(0, 2, 1)]).astype(np.float32))


def _gla_block(direction, rows, q_ref, k_ref, v_ref, z_ref, w2_ref, b2_ref, m_ref, st_ref):
    c = GLA_BLOCK
    q = q_ref[rows, :] * (B_KEY_DIM ** -0.5)
    k = k_ref[rows, :]
    v = v_ref[rows, :].astype(BF16)
    gate_cols = slice(direction * B_QK, (direction + 1) * B_QK)
    pre = _dot(z_ref[rows, :].astype(BF16), w2_ref[:, gate_cols]) + b2_ref[:, gate_cols]
    log_a = -_softplus(-pre) * (LOG2E / GATE_TAU)

    row = lax.broadcasted_iota(jnp.int32, (c, B_QK), 0)
    first_head = lax.broadcasted_iota(jnp.int32, (c, LANE), 1) < B_KEY_DIM
    pair = [slice(p * LANE, (p + 1) * LANE) for p in range(GLA_PAIRS)]
    k_first = [jnp.where(first_head, k[:, ps], 0.0) for ps in pair]
    k_second = [jnp.where(first_head, 0.0, k[:, ps]) for ps in pair]

    def exponents(p_s, t_s):
        if direction == 0:
            return p_s, t_s - p_s
        return t_s - p_s + log_a, p_s - log_a

    def scaled(e_q, e_k, p):
        w_q = jnp.exp2(e_q[:, pair[p]])
        w_k = jnp.exp2(e_k[:, pair[p]])
        q_t = (q[:, pair[p]] * w_q).astype(BF16)
        k_t = jnp.concatenate([(k_first[p] * w_k).astype(BF16), (k_second[p] * w_k).astype(BF16)], axis=0)
        return q_t, k_t

    attn = [None] * GLA_PAIRS

    def add_level(level, q_t, k_t, p):
        mask = m_ref[direction, level]
        contrib = jnp.concatenate([mask, mask], axis=1) * _dot_nt(q_t, k_t)
        attn[p] = contrib if attn[p] is None else attn[p] + contrib

    if direction == 0:
        for p in range(GLA_PAIRS):
            k_t = jnp.concatenate([k_first[p].astype(BF16), k_second[p].astype(BF16)], axis=0)
            add_level(0, q[:, pair[p]].astype(BF16), k_t, p)
    p_s = log_a
    t_s = log_a
    for li, s in enumerate(GLA_LEVELS):
        e_q, e_k = exponents(p_s, t_s)
        for p in range(GLA_PAIRS):
            q_t, k_t = scaled(e_q, e_k, p)
            add_level(li + 1, q_t, k_t, p)
        upper = (row & s) != 0
        t_before = pltpu.roll(t_s, s, 0)
        t_after = pltpu.roll(t_s, c - s, 0)
        p_s = p_s + jnp.where(upper, t_before, 0.0)
        t_s = t_s + jnp.where(upper, t_before, t_after)

    e_q, e_k = exponents(p_s, t_s)
    state_decay = jnp.exp2(t_s[0:1, :])
    own_block = lax.broadcasted_iota(jnp.int32, (2 * B_VAL_DIM, LANE), 0) < B_VAL_DIM
    own_block = own_block == (lax.broadcasted_iota(jnp.int32, (2 * B_VAL_DIM, LANE), 1) < B_KEY_DIM)
    zero_v = jnp.zeros((c, B_VAL_DIM), BF16)
    outs = []
    for p in range(GLA_PAIRS):
        v_pair = v[:, 2 * p * B_VAL_DIM:(2 * p + 2) * B_VAL_DIM]
        v_diag = jnp.concatenate(
            [jnp.concatenate([v_pair[:, :B_VAL_DIM], zero_v], axis=1),
             jnp.concatenate([zero_v, v_pair[:, B_VAL_DIM:]], axis=1)], axis=0)
        state_t = st_ref[direction, p]
        q_t = (q[:, pair[p]] * jnp.exp2(e_q[:, pair[p]])).astype(BF16)
        k_t = (k[:, pair[p]] * jnp.exp2(e_k[:, pair[p]])).astype(BF16)
        o = _dot_nt(q_t, state_t.astype(BF16)) + _dot(attn[p].astype(BF16), v_diag)
        update = jnp.where(own_block, _dot_tn(v_pair, k_t), 0.0)
        st_ref[direction, p] = state_t * state_decay[:, pair[p]] + update
        outs.append(o)
    return jnp.concatenate(outs, axis=1)


def _gla_kernel(q_ref, k_ref, v_ref, g_ref, z_ref, w2_ref, b2_ref, gn_ref, m_ref,
                o_ref, of_ref, ob_ref, st_ref):
    nblk = q_ref.shape[0] // GLA_BLOCK
    st_ref[...] = jnp.zeros(st_ref.shape, F32)
    block = functools.partial(_gla_block, q_ref=q_ref, k_ref=k_ref, v_ref=v_ref, z_ref=z_ref,
                              w2_ref=w2_ref, b2_ref=b2_ref, m_ref=m_ref, st_ref=st_ref)

    def rows_of(n):
        return pl.ds(pl.multiple_of(n * GLA_BLOCK, GLA_BLOCK), GLA_BLOCK)

    def finish(rows, o):
        normed = []
        for h in range(B_HEADS):
            oh = o[:, h * B_VAL_DIM:(h + 1) * B_VAL_DIM]
            normed.append(oh * _rms_scale(oh))
        g = g_ref[rows, :]
        out = jnp.concatenate(normed, axis=1) * gn_ref[...] * (g * jax.nn.sigmoid(g))
        o_ref[rows, :] = out.astype(o_ref.dtype)

    def first_half(n, carry):
        rows_f, rows_b = rows_of(n), rows_of(nblk - 1 - n)
        of_ref[rows_f, :] = block(0, rows_f)
        ob_ref[rows_b, :] = block(1, rows_b)
        return carry

    def second_half(n, carry):
        rows_f, rows_b = rows_of(n), rows_of(nblk - 1 - n)
        finish(rows_f, block(0, rows_f) + ob_ref[rows_f, :])
        finish(rows_b, of_ref[rows_b, :] + block(1, rows_b))
        return carry

    lax.fori_loop(0, nblk // 2, first_half, 0)
    lax.fori_loop(nblk // 2, nblk, second_half, 0)


def _gla(proj, w2, layer, b2, gn, masks, bsz, seq):
    def col(width, start):
        return pl.BlockSpec((seq, width), lambda b: (b, start // width))

    def whole(a):
        return pl.BlockSpec(a.shape, lambda b: (0,) * a.ndim)

    return pl.pallas_call(
        _gla_kernel,
        grid=(bsz,),
        in_specs=[col(B_QK, COL_BQ), col(B_QK, COL_BK), col(B_V, COL_BV), col(B_V, COL_BG),
                  col(LANE, COL_Z), pl.BlockSpec((None,) + w2.shape[1:], lambda b: (layer, 0, 0)),
                  whole(b2), whole(gn), whole(masks)],
        out_specs=pl.BlockSpec((seq, B_V), lambda b: (b, 0)),
        out_shape=jax.ShapeDtypeStruct((bsz * seq, B_V), BF16),
        scratch_shapes=[pltpu.VMEM((seq, B_V), F32), pltpu.VMEM((seq, B_V), F32),
                        pltpu.VMEM((2, GLA_PAIRS, 2 * B_VAL_DIM, LANE), F32)],
        compiler_params=_cparams(("parallel",)),
        name="gla",
    )(proj, proj, proj, proj, proj, w2, b2, gn, masks)


LRU_SEG = 256
SUBLANES = 8
LRU_PITCH = LRU_SEG + SUBLANES


def _lru_kernel(x_ref, y_ref, cw_ref, cb_ref, wa_ref, wx_ref, ba_ref, bx_ref, lam_ref, o_ref,
                xp_ref, af_ref, hf_ref, ab_ref, hb_ref):
    seq = x_ref.shape[0]
    ch = x_ref.shape[1]
    nseg = seq // LRU_SEG
    halo = SUBLANES
    xp_ref[0:halo, :] = jnp.zeros((halo, ch), F32)
    xp_ref[seq + halo:seq + 2 * halo, :] = jnp.zeros((halo, ch), F32)
    xp_ref[halo:seq + halo, :] = x_ref[...]
    decay_rate = _softplus(-lam_ref[...])
    scan_refs = ((af_ref, hf_ref), (ab_ref, hb_ref))

    def gates(c, carry):
        r0 = pl.multiple_of(c * LRU_SEG, LRU_SEG)
        dst = pl.ds(pl.multiple_of(c * LRU_PITCH, SUBLANES), LRU_SEG)
        ext = LRU_SEG + 2 * halo
        xe = xp_ref[pl.ds(r0, ext), :]
        xc = (cw_ref[0:1, :] * pltpu.roll(xe, 2, 0) + cw_ref[1:2, :] * pltpu.roll(xe, 1, 0)
              + cw_ref[2:3, :] * xe + cw_ref[3:4, :] * pltpu.roll(xe, ext - 1, 0))
        xc = xc[halo:halo + LRU_SEG] + cb_ref[...]
        xb = xc.astype(BF16)
        for d in range(2):
            r = jax.nn.sigmoid(_dot(xb, wa_ref[d, 0]) + ba_ref[d:d + 1, :])
            i = jax.nn.sigmoid(_dot(xb, wx_ref[d, 0]) + bx_ref[d:d + 1, :])
            log_a = -LRU_C * r * decay_rate[d:d + 1, :]
            t = jnp.tanh(log_a)
            a_ref, u_ref = scan_refs[d]
            a_ref[dst, :] = jnp.exp(log_a)
            u_ref[dst, :] = jnp.sqrt(-2.0 * t / (1.0 - t)) * (i * xc)
        return carry

    lax.fori_loop(0, nseg, gates, 0)

    def scan_step(j, carry):
        hf, pf, hb, pb = carry
        rf = pl.ds(j, nseg, stride=LRU_PITCH)
        a = af_ref[rf, :]
        hf = a * hf + hf_ref[rf, :]
        pf = a * pf
        hf_ref[rf, :] = hf
        af_ref[rf, :] = pf
        rb = pl.ds(LRU_SEG - 1 - j, nseg, stride=LRU_PITCH)
        a = ab_ref[rb, :]
        hb = a * hb + hb_ref[rb, :]
        pb = a * pb
        hb_ref[rb, :] = hb
        ab_ref[rb, :] = pb
        return hf, pf, hb, pb

    zeros = jnp.zeros((nseg, ch), F32)
    ones = jnp.ones((nseg, ch), F32)
    lax.fori_loop(0, LRU_SEG, scan_step, (zeros, ones, zeros, ones), unroll=4)

    enter_f = [None] * nseg
    state = jnp.zeros((1, ch), F32)
    for s in range(nseg):
        enter_f[s] = state
        last = s * LRU_PITCH + LRU_SEG - 1
        state = hf_ref[last:last + 1, :] + af_ref[last:last + 1, :] * state
    enter_b = [None] * nseg
    state = jnp.zeros((1, ch), F32)
    for s in reversed(range(nseg)):
        enter_b[s] = state
        first = s * LRU_PITCH
        state = hb_ref[first:first + 1, :] + ab_ref[first:first + 1, :] * state

    for s in range(nseg):
        seg = slice(s * LRU_PITCH, s * LRU_PITCH + LRU_SEG)
        rows = slice(s * LRU_SEG, (s + 1) * LRU_SEG)
        h = (hf_ref[seg, :] + af_ref[seg, :] * enter_f[s]) + (hb_ref[seg, :] + ab_ref[seg, :] * enter_b[s])
        o_ref[rows, :] = (h * jax.nn.gelu(y_ref[rows, :])).astype(o_ref.dtype)


def _rglru(proj, conv_w, conv_b, w_a, w_x, layer, b_a, b_x, lam, bsz, seq):
    ch = C_BLOCK_DIM

    def col(start):
        return pl.BlockSpec((seq, ch), lambda b, c: (b, start // ch + c))

    def per_block(rows):
        return pl.BlockSpec((rows, ch), lambda b, c: (0, c))

    gate_w = pl.BlockSpec((None, 2, 1, ch, ch), lambda b, c: (layer, 0, c, 0, 0))
    return pl.pallas_call(
        _lru_kernel,
        grid=(bsz, C_BLOCKS),
        in_specs=[col(COL_CX), col(COL_CY), per_block(CONV_WIDTH), per_block(1), gate_w, gate_w,
                  per_block(2), per_block(2), per_block(2)],
        out_specs=pl.BlockSpec((seq, ch), lambda b, c: (b, c)),
        out_shape=jax.ShapeDtypeStruct((bsz * seq, C_WIDTH), BF16),
        scratch_shapes=([pltpu.VMEM((seq + 2 * SUBLANES, ch), F32)]
                        + [pltpu.VMEM((seq // LRU_SEG * LRU_PITCH, ch), F32)] * 4),
        compiler_params=_cparams(("parallel", "parallel")),
        name="rglru",
    )(proj, proj, conv_w, conv_b.reshape(1, C_WIDTH), w_a, w_x, b_a, b_x, lam)


def _out_proj_kernel(oa_ref, ob_ref, oc_ref, w_hbm, g_ref, x_ref, o_ref, w_ref, stage_ref, sem_ref, *, layer):
    @pl.when(pl.program_id(0) == 0)
    def _():
        _fetch_weight(w_hbm, layer, w_ref, stage_ref, sem_ref)

    for r in _row_groups(x_ref.shape[0]):
        mixed = (_dot(oa_ref[r, :], w_ref[0:A_Q, :]) + _dot(ob_ref[r, :], w_ref[A_Q:A_Q + B_V, :])
                 + _dot(oc_ref[r, :], w_ref[A_Q + B_V:A_Q + B_V + C_WIDTH, :]))
        o_ref[r, :] = x_ref[r, :] + mixed * _rms_scale(mixed) * g_ref[...]


def _out_proj(oa, ob, oc, w, layer, g, x, tm):
    m, d = x.shape

    def rows(width):
        return pl.BlockSpec((tm, width), lambda i: (i, 0))

    return pl.pallas_call(
        functools.partial(_out_proj_kernel, layer=layer),
        grid=(m // tm,),
        in_specs=[rows(A_Q), rows(B_V), rows(C_WIDTH),
                  pl.BlockSpec(memory_space=pl.ANY),
                  pl.BlockSpec((1, d), lambda i: (0, 0)),
                  rows(d)],
        out_specs=rows(d),
        out_shape=jax.ShapeDtypeStruct((m, d), F32),
        scratch_shapes=_weight_scratch(w.shape[1], w.shape[2]),
        compiler_params=_cparams(("arbitrary",)),
        name="out_proj",
    )(oa, ob, oc, w, g.reshape(1, d), x)


def _cross_attn_kernel(x_ref, gpre_ref, wq_hbm, k_ref, v_ref, wo_hbm, gpost_ref, o_ref,
                       wq_ref, wo_ref, stage_ref, sem_ref, *, layer):
    @pl.when(jnp.logical_and(pl.program_id(0) == 0, pl.program_id(1) == 0))
    def _():
        _fetch_weight(wq_hbm, layer, wq_ref, stage_ref, sem_ref)
        _fetch_weight(wo_hbm, layer, wo_ref, stage_ref, sem_ref)

    x = x_ref[...]
    hn = (x * _rms_scale(x) * gpre_ref[...]).astype(BF16)
    q = _dot(hn, wq_ref[...]).astype(BF16)
    acc = None
    for h in range(X_HEADS):
        hs = slice(h * X_HEAD_DIM, (h + 1) * X_HEAD_DIM)
        s = _dot_nt(q[:, hs], k_ref[:, hs]) * (X_HEAD_DIM ** -0.5)
        m = jnp.max(s, axis=-1, keepdims=True)
        p = jnp.exp(s - m)
        o = _dot(p.astype(BF16), v_ref[:, hs]) / jnp.sum(p, axis=-1, keepdims=True)
        part = _dot(o.astype(BF16), wo_ref[hs, :])
        acc = part if acc is None else acc + part
    o_ref[...] = x + acc * _rms_scale(acc) * gpost_ref[...]


def _cross_attention(x, gpre, wq, k, v, wo, layer, gpost, bsz, seq, tq):
    d = D_MODEL
    per_row = seq // tq
    mem_spec = pl.BlockSpec((MEM_LEN, d), lambda b, i: (b, 0))
    gain_spec = pl.BlockSpec((1, d), lambda b, i: (0, 0))
    weight_spec = pl.BlockSpec(memory_space=pl.ANY)
    w_bf, stage, sem = _weight_scratch(d, d)
    row_spec = pl.BlockSpec((tq, d), lambda b, i: (b * per_row + i, 0))
    return pl.pallas_call(
        functools.partial(_cross_attn_kernel, layer=layer),
        grid=(bsz, per_row),
        in_specs=[row_spec, gain_spec, weight_spec, mem_spec, mem_spec, weight_spec, gain_spec],
        out_specs=row_spec,
        out_shape=jax.ShapeDtypeStruct((bsz * seq, d), F32),
        scratch_shapes=[w_bf, w_bf, stage, sem],
        compiler_params=_cparams(("arbitrary", "arbitrary")),
        name="cross_attention",
    )(x, gpre.reshape(1, d), wq, k, v, wo, gpost.reshape(1, d))


def _ffn_kernel(x_ref, gpre_ref, wu_ref, wd_ref, gpost_ref, o_ref, hn_ref, acc_ref):
    f = pl.program_id(1)

    @pl.when(f == 0)
    def _():
        x = x_ref[...]
        hn_ref[...] = (x * _rms_scale(x) * gpre_ref[...]).astype(BF16)
        acc_ref[...] = jnp.zeros(acc_ref.shape, F32)

    up = _dot(hn_ref[...], wu_ref[...])
    act = jnp.square(jnp.maximum(up, 0.0)).astype(BF16)
    acc_ref[...] += _dot(act, wd_ref[...])

    @pl.when(f == pl.num_programs(1) - 1)
    def _():
        ff = acc_ref[...]
        o_ref[...] = x_ref[...] + ff * _rms_scale(ff) * gpost_ref[...]


def _ffn(x, gpre, w_up, w_down, layer, gpost, tm, tf):
    m, d = x.shape
    dff = w_up.shape[2]
    return pl.pallas_call(
        _ffn_kernel,
        grid=(m // tm, dff // tf),
        in_specs=[pl.BlockSpec((tm, d), lambda i, f: (i, 0)),
                  pl.BlockSpec((1, d), lambda i, f: (0, 0)),
                  pl.BlockSpec((None, d, tf), lambda i, f: (layer, 0, f)),
                  pl.BlockSpec((None, tf, d), lambda i, f: (layer, f, 0)),
                  pl.BlockSpec((1, d), lambda i, f: (0, 0))],
        out_specs=pl.BlockSpec((tm, d), lambda i, f: (i, 0)),
        out_shape=jax.ShapeDtypeStruct((m, d), F32),
        scratch_shapes=[pltpu.VMEM((tm, d), BF16), pltpu.VMEM((tm, d), F32)],
        compiler_params=_cparams(("parallel", "arbitrary")),
        name="ffn",
    )(x, gpre.reshape(1, d), w_up, w_down, gpost.reshape(1, d))


def _split_w_in(w):
    z0 = A_Q + 2 * A_KV + 2 * B_QK + 2 * B_V
    z1 = z0 + 2 * GATE_RANK
    wt = jnp.swapaxes(w, 1, 2).astype(BF16)
    gates = jnp.pad(wt[:, z0:z1], ((0, 0), (0, LANE - 2 * GATE_RANK), (0, 0)))
    return wt[:, :z0], wt[:, z1:], gates


def _gate_expansion(w2_f, w2_b):
    w = jnp.zeros((w2_f.shape[0], LANE, 2 * B_QK), F32)
    w = w.at[:, 0:GATE_RANK, 0:B_QK].set(w2_f)
    w = w.at[:, GATE_RANK:2 * GATE_RANK, B_QK:].set(w2_b)
    return w.astype(BF16)


def kernel(x, mem, rel_bias, w_in, w_out, attn_sink, gla_w2_f, gla_b2_f, gla_w2_b, gla_b2_b, gla_norm, conv_w, conv_b, lru_wa, lru_ba, lru_wx, lru_bx, lru_lambda, xq, xk, xv, xo, w_up, w_down, norm_mix_pre, norm_mix_post, norm_mem, norm_x_pre, norm_x_post, norm_ff_pre, norm_ff_post):
    bsz, seq, d = x.shape
    assert d == D_MODEL and seq == SUBLANES * LRU_SEG
    xs = x.reshape(bsz * seq, d)
    mems = mem.reshape(bsz * mem.shape[1], d)
    bias_tab = _attention_bias_table(rel_bias)
    masks = _gla_masks()
    w_in_main, w_in_c, w_in_z = _split_w_in(w_in)
    w2_b = _gate_expansion(gla_w2_f, gla_w2_b)
    b2 = jnp.concatenate([gla_b2_f, gla_b2_b], axis=-1)
    bf = lambda w: w.astype(BF16)
    wa_b, wx_b = bf(lru_wa), bf(lru_wx)
    w_up_b, w_down_b = bf(w_up), bf(w_down)
    for l in range(DEPTH):
        proj = _in_proj(xs, norm_mix_pre[l], w_in_main, w_in_c, w_in_z, l, tm=512)
        oa = _windowed_attention(proj, bias_tab, attn_sink[l], bsz, seq)
        ob = _gla(proj, w2_b, l, b2[l].reshape(1, 2 * B_QK), gla_norm[l].reshape(1, B_V), masks, bsz, seq)
        oc = _rglru(proj, conv_w[l], conv_b[l], wa_b, wx_b, l, lru_ba[l], lru_bx[l], lru_lambda[l],
                    bsz, seq)
        xs = _out_proj(oa, ob, oc, w_out, l, norm_mix_post[l], xs, tm=512)

        k, v = _kv_proj(mems, norm_mem[l], xk, xv, l, tm=512)
        xs = _cross_attention(xs, norm_x_pre[l], xq, k, v, xo, l, norm_x_post[l], bsz, seq, tq=512)

        xs = _ffn(xs, norm_ff_pre[l], w_up_b, w_down_b, l, norm_ff_post[l], tm=512, tf=1024)
    return xs.reshape(bsz, seq, d)
```

```python
import functools
import math

import numpy as np
import jax
import jax.numpy as jnp
from jax import lax
from jax.experimental import pallas as pl
from jax.experimental.pallas import tpu as pltpu

F32 = jnp.float32
BF16 = jnp.bfloat16

D_MODEL = 2048
DEPTH = 4
MEM_LEN = 256
A_HEAD_DIM = 128
A_HEADS = 8
A_KV_HEADS = 2
A_GROUP = 4
WINDOW = 128
BLOCK = 128
N_BUCKETS = 32
MAX_DISTANCE = 128
B_HEADS = 4
B_KEY_DIM = 64
B_VAL_DIM = 128
GATE_RANK = 16
GATE_TAU = 16.0
C_WIDTH = 512
C_BLOCKS = 4
C_BLOCK_DIM = 128
CONV_WIDTH = 4
LRU_C = 8.0
X_HEADS = 4
X_HEAD_DIM = 512
D_FF = 8192
EPS = 1e-6
NEG_INF = -1e30
LOG2E = math.log2(math.e)

A_Q = A_HEADS * A_HEAD_DIM
A_KV = A_KV_HEADS * A_HEAD_DIM
B_QK = B_HEADS * B_KEY_DIM
B_V = B_HEADS * B_VAL_DIM

LANE = 128
COL_AQ = 0
COL_AK = COL_AQ + A_Q
COL_AV = COL_AK + A_KV
COL_BQ = COL_AV + A_KV
COL_BK = COL_BQ + B_QK
COL_BV = COL_BK + B_QK
COL_BG = COL_BV + B_V
COL_CX = COL_BG + B_V
COL_CY = COL_CX + C_WIDTH
COL_Z = COL_CY + C_WIDTH
D_PROJ = COL_Z + LANE

GLA_BLOCK = 128
GLA_LEVELS = (1, 2, 4, 8, 16, 32, 64)
GLA_PAIRS = B_HEADS // 2
VMEM_LIMIT_MIB = 56


def _cparams(semantics, vmem_mib=VMEM_LIMIT_MIB):
    return pltpu.CompilerParams(dimension_semantics=semantics,
                                vmem_limit_bytes=vmem_mib * 1024 * 1024)


def _dot(a, b):
    return jnp.dot(a, b, preferred_element_type=F32)


def _dot_nt(a, b):
    return lax.dot_general(a, b, (((1,), (1,)), ((), ())), preferred_element_type=F32)


def _dot_tn(a, b):
    return lax.dot_general(a, b, (((0,), (0,)), ((), ())), preferred_element_type=F32)


def _softplus(x):
    return jnp.maximum(x, 0.0) + jnp.log(1.0 + jnp.exp(-jnp.abs(x)))


def _sigmoid(x):
    return 0.5 * jnp.tanh(0.5 * x) + 0.5


def _rms_scale(x):
    return lax.rsqrt(jnp.mean(jnp.square(x), axis=-1, keepdims=True) + EPS)


ROW_CHAINS = 2


def _row_groups(rows):
    step = rows // ROW_CHAINS
    return [slice(c * step, (c + 1) * step) for c in range(ROW_CHAINS)]


def _resident(shape, index_map):
    return pl.BlockSpec(shape, index_map, pipeline_mode=pl.Buffered(1))


WEIGHT_CHUNK_ROWS = 256


def _weight_scratch(k, n):
    return [pltpu.VMEM((k, n), BF16), pltpu.VMEM((2, WEIGHT_CHUNK_ROWS, n), F32),
            pltpu.SemaphoreType.DMA((2,))]


def _fetch_weight(w_hbm, layer, dst_ref, stage_ref, sem_ref):
    nchunk = dst_ref.shape[0] // WEIGHT_CHUNK_ROWS

    def copy(c):
        rows = pl.ds(c * WEIGHT_CHUNK_ROWS, WEIGHT_CHUNK_ROWS)
        return pltpu.make_async_copy(w_hbm.at[layer, rows, :], stage_ref.at[c % 2], sem_ref.at[c % 2])

    copy(0).start()
    for c in range(nchunk):
        if c + 1 < nchunk:
            copy(c + 1).start()
        copy(c).wait()
        dst_ref[c * WEIGHT_CHUNK_ROWS:(c + 1) * WEIGHT_CHUNK_ROWS, :] = stage_ref[c % 2].astype(BF16)


def _kv_proj_kernel(x_ref, g_ref, wk_hbm, wv_hbm, k_ref, v_ref, wk_ref, wv_ref, stage_ref, sem_ref, *, layer):
    @pl.when(pl.program_id(0) == 0)
    def _():
        _fetch_weight(wk_hbm, layer, wk_ref, stage_ref, sem_ref)
        _fetch_weight(wv_hbm, layer, wv_ref, stage_ref, sem_ref)

    x = x_ref[...]
    hn = (x * _rms_scale(x) * g_ref[...]).astype(BF16)
    k_ref[...] = _dot(hn, wk_ref[...]).astype(BF16)
    v_ref[...] = _dot(hn, wv_ref[...]).astype(BF16)


def _kv_proj(x, g, wk, wv, layer, tm):
    m, d = x.shape
    weight = pl.BlockSpec(memory_space=pl.ANY)
    rows = pl.BlockSpec((tm, d), lambda i: (i, 0))
    w_bf, stage, sem = _weight_scratch(d, d)
    return pl.pallas_call(
        functools.partial(_kv_proj_kernel, layer=layer),
        grid=(m // tm,),
        in_specs=[rows, pl.BlockSpec((1, d), lambda i: (0, 0)), weight, weight],
        out_specs=[rows, rows],
        out_shape=[jax.ShapeDtypeStruct((m, d), BF16)] * 2,
        scratch_shapes=[w_bf, w_bf, stage, sem],
        compiler_params=_cparams(("arbitrary",)),
        name="kv_proj",
    )(x, g.reshape(1, d), wk, wv)


IN_Z0 = A_Q + 2 * A_KV + 2 * B_QK + 2 * B_V
IN_Z1 = IN_Z0 + 2 * GATE_RANK
D_IN = IN_Z1 + 2 * C_WIDTH
IN_CHUNK_ROWS = 128


def _fetch_w_in(w_hbm, layer, wm_ref, wc_ref, wz_ref, stage_ref, sem_ref):
    k = wm_ref.shape[0]
    nchunk = k // IN_CHUNK_ROWS
    wz_ref[...] = jnp.zeros(wz_ref.shape, BF16)

    def copy(c):
        rows = pl.ds(c * IN_CHUNK_ROWS, IN_CHUNK_ROWS)
        return pltpu.make_async_copy(w_hbm.at[layer, rows, :], stage_ref.at[c % 2], sem_ref.at[c % 2])

    copy(0).start()
    for c in range(nchunk):
        if c + 1 < nchunk:
            copy(c + 1).start()
        copy(c).wait()
        rows = slice(c * IN_CHUNK_ROWS, (c + 1) * IN_CHUNK_ROWS)
        wm_ref[rows, :] = stage_ref[c % 2, :, 0:IN_Z0].astype(BF16)
        wc_ref[rows, :] = stage_ref[c % 2, :, IN_Z1:D_IN].astype(BF16)
        wz_ref[rows, 0:IN_Z1 - IN_Z0] = stage_ref[c % 2, :, IN_Z0:IN_Z1].astype(BF16)


def _in_proj_kernel(x_ref, g_ref, w_hbm, o_ref, wm_ref, wc_ref, wz_ref, stage_ref, sem_ref, *, layer):
    @pl.when(pl.program_id(0) == 0)
    def _():
        _fetch_w_in(w_hbm, layer, wm_ref, wc_ref, wz_ref, stage_ref, sem_ref)

    for r in _row_groups(x_ref.shape[0]):
        x = x_ref[r, :]
        hn = (x * _rms_scale(x) * g_ref[...]).astype(BF16)
        o_ref[r, 0:COL_CX] = _dot(hn, wm_ref[...])
        o_ref[r, COL_CX:COL_Z] = _dot(hn, wc_ref[...])
        o_ref[r, COL_Z:D_PROJ] = _dot(hn, wz_ref[...])


def _in_proj(x, g, w_in, layer, tm):
    m, k = x.shape
    assert w_in.shape[1:] == (k, D_IN)
    return pl.pallas_call(
        functools.partial(_in_proj_kernel, layer=layer),
        grid=(m // tm,),
        in_specs=[pl.BlockSpec((tm, k), lambda i: (i, 0)), pl.BlockSpec((1, k), lambda i: (0, 0)),
                  pl.BlockSpec(memory_space=pl.ANY)],
        out_specs=pl.BlockSpec((tm, D_PROJ), lambda i: (i, 0)),
        out_shape=jax.ShapeDtypeStruct((m, D_PROJ), F32),
        scratch_shapes=[pltpu.VMEM((k, IN_Z0), BF16), pltpu.VMEM((k, 2 * C_WIDTH), BF16),
                        pltpu.VMEM((k, LANE), BF16), pltpu.VMEM((2, IN_CHUNK_ROWS, D_IN), F32),
                        pltpu.SemaphoreType.DMA((2,))],
        compiler_params=_cparams(("arbitrary",)),
        name="in_proj",
    )(x, g.reshape(1, k), w_in)


def _attn_kernel(sink_ref, q_ref, k_ref, v_ref, bias_ref, o_ref, kb_ref, vt_ref, p_ref, sw_ref):
    seq = q_ref.shape[0]
    nblk = seq // BLOCK
    kb_ref[...] = k_ref[...].astype(BF16)
    for g in range(A_KV_HEADS):
        for n in range(nblk):
            vt_ref[g * nblk + n] = (
                v_ref[n * BLOCK:(n + 1) * BLOCK, g * A_HEAD_DIM:(g + 1) * A_HEAD_DIM].T.astype(BF16))
    sinks = [jnp.concatenate([jnp.full((1, BLOCK), sink_ref[g * A_GROUP + h] * LOG2E, F32)
                              for h in range(A_GROUP)], axis=1) for g in range(A_KV_HEADS)]

    def tiles(first, last):
        return (1 if first else 0), (2 if last else 3)

    def scores(g, n, first, last):
        t0, t1 = tiles(first, last)
        nk = (t1 - t0) * BLOCK
        r0 = pl.multiple_of(n * BLOCK, BLOCK)
        k = kb_ref[pl.ds(pl.multiple_of(r0 + (t0 - 1) * BLOCK, BLOCK), nk),
                   g * A_HEAD_DIM:(g + 1) * A_HEAD_DIM]
        q = jnp.concatenate(
            [q_ref[pl.ds(r0, BLOCK), (g * A_GROUP + h) * A_HEAD_DIM:(g * A_GROUP + h + 1) * A_HEAD_DIM]
             for h in range(A_GROUP)], axis=0).astype(BF16)
        s = _dot_nt(k, q) * (A_HEAD_DIM ** -0.5 * LOG2E) + bias_ref[g, t0 * BLOCK:t1 * BLOCK, :]
        m = jnp.maximum(jnp.max(s, axis=0, keepdims=True), sinks[g])
        p_ref[g, 0:nk, :] = jnp.exp2(s - m).astype(BF16)
        sw_ref[g] = jnp.exp2(sinks[g] - m)

    def output(g, n, first, last):
        t0, t1 = tiles(first, last)
        nk = (t1 - t0) * BLOCK
        r0 = pl.multiple_of(n * BLOCK, BLOCK)
        v_t = jnp.concatenate([vt_ref[g * nblk + n + t - 1] for t in range(t0, t1)], axis=1)
        ones = jnp.ones((2 * SUBLANES, nk), BF16)
        o = _dot(jnp.concatenate([v_t, ones], axis=0), p_ref[g, 0:nk, :])
        denom = o[A_HEAD_DIM:A_HEAD_DIM + 1, :] + sw_ref[g]
        o = o[0:A_HEAD_DIM, :] / denom
        for h in range(A_GROUP):
            col = (g * A_GROUP + h) * A_HEAD_DIM
            o_ref[pl.ds(r0, BLOCK), col:col + A_HEAD_DIM] = (
                o[:, h * BLOCK:(h + 1) * BLOCK].T.astype(o_ref.dtype))

    def both(fn, *args):
        for g in range(A_KV_HEADS):
            fn(g, *args)

    def body(n, carry):
        both(output, n, False, False)
        both(scores, n + 1, False, False)
        return carry

    both(scores, 0, True, False)
    both(output, 0, True, False)
    both(scores, 1, False, False)
    lax.fori_loop(1, nblk - 2, body, 0)
    both(output, nblk - 2, False, False)
    both(scores, nblk - 1, False, True)
    both(output, nblk - 1, False, True)


def _windowed_attention(proj, bias_tab, sink, bsz, seq):
    qw = A_GROUP * A_HEAD_DIM
    nblk = seq // BLOCK
    return pl.pallas_call(
        _attn_kernel,
        grid=(bsz,),
        in_specs=[pl.BlockSpec(memory_space=pltpu.SMEM),
                  pl.BlockSpec((seq, A_Q), lambda b: (b, COL_AQ // A_Q)),
                  pl.BlockSpec((seq, A_KV), lambda b: (b, COL_AK // A_KV)),
                  pl.BlockSpec((seq, A_KV), lambda b: (b, COL_AV // A_KV)),
                  pl.BlockSpec(bias_tab.shape, lambda b: (0, 0, 0))],
        out_specs=pl.BlockSpec((seq, A_Q), lambda b: (b, 0)),
        out_shape=jax.ShapeDtypeStruct((bsz * seq, A_Q), BF16),
        scratch_shapes=[pltpu.VMEM((seq, A_KV), BF16),
                        pltpu.VMEM((A_KV_HEADS * nblk, A_HEAD_DIM, BLOCK), BF16),
                        pltpu.VMEM((A_KV_HEADS, 3 * BLOCK, qw), BF16),
                        pltpu.VMEM((A_KV_HEADS, 1, qw), F32)],
        compiler_params=_cparams(("parallel",)),
        name="windowed_attention",
    )(sink, proj, proj, proj, bias_tab)


def _t5_bucket(rel):
    nb = N_BUCKETS // 2
    max_exact = nb // 2
    ret = jnp.where(rel > 0, nb, 0)
    n = jnp.abs(rel)
    nf = jnp.maximum(n, 1).astype(F32)
    large = max_exact + (jnp.log(nf / max_exact) / math.log(MAX_DISTANCE / max_exact)
                         * (nb - max_exact)).astype(jnp.int32)
    large = jnp.minimum(large, nb - 1)
    return ret + jnp.where(n < max_exact, n, large)


def _attention_bias_table(rel_bias):
    kj = jnp.arange(3 * BLOCK)[:, None]
    qi = jnp.arange(BLOCK)[None, :]
    rel = kj - BLOCK - qi
    onehot = jax.nn.one_hot(_t5_bucket(rel), N_BUCKETS, dtype=F32)
    bias = jnp.einsum("kqn,nh->hkq", onehot, rel_bias.astype(F32), precision=lax.Precision.HIGHEST)
    bias = jnp.where((jnp.abs(rel) <= WINDOW)[None], bias * LOG2E, NEG_INF)
    bias = bias.reshape(A_KV_HEADS, A_GROUP, 3 * BLOCK, BLOCK)
    return jnp.transpose(bias, (0, 2, 1, 3)).reshape(A_KV_HEADS, 3 * BLOCK, A_GROUP * BLOCK)


def _gla_masks():
    c = GLA_BLOCK
    i = np.arange(c)[:, None]
    j = np.arange(c)[None, :]
    fwd = [i == j]
    for s in GLA_LEVELS:
        fwd.append((i // (2 * s) == j // (2 * s)) & ((i // s) % 2 == 1) & ((j // s) % 2 == 0))
    fwd = np.stack(fwd)
    return jnp.asarray(np.stack([fwd, fwd.transpose(0, 2, 1)]).astype(np.float32))


def _gla_block(direction, rows, q_ref, k_ref, v_ref, z_ref, w2_ref, b2_ref, m_ref, st_ref):
    c = GLA_BLOCK
    q = q_ref[rows, :] * (B_KEY_DIM ** -0.5)
    k = k_ref[rows, :]
    v = v_ref[rows, :].astype(BF16)
    gate_cols = slice(direction * B_QK, (direction + 1) * B_QK)
    pre = _dot(z_ref[rows, :].astype(BF16), w2_ref[:, gate_cols]) + b2_ref[:, gate_cols]
    log_a = -_softplus(-pre) * (LOG2E / GATE_TAU)

    row = lax.broadcasted_iota(jnp.int32, (c, B_QK), 0)
    first_head = lax.broadcasted_iota(jnp.int32, (c, LANE), 1) < B_KEY_DIM
    pair = [slice(p * LANE, (p + 1) * LANE) for p in range(GLA_PAIRS)]
    q_b = [q[:, ps].astype(BF16) for ps in pair]
    k_b = [k[:, ps].astype(BF16) for ps in pair]
    k_first = [jnp.where(first_head, kb, jnp.zeros_like(kb)) for kb in k_b]
    k_second = [jnp.where(first_head, jnp.zeros_like(kb), kb) for kb in k_b]

    def exponents(p_s, t_s):
        if direction == 0:
            return p_s, t_s - p_s
        return t_s - p_s + log_a, p_s - log_a

    def scaled(e_q, e_k, p):
        w_q = jnp.exp2(e_q[:, pair[p]]).astype(BF16)
        w_k = jnp.exp2(e_k[:, pair[p]]).astype(BF16)
        return q_b[p] * w_q, jnp.concatenate([k_first[p] * w_k, k_second[p] * w_k], axis=0)

    attn = [None] * GLA_PAIRS

    def add_level(level, q_t, k_t, p):
        mask = m_ref[direction, level]
        contrib = jnp.concatenate([mask, mask], axis=1) * _dot_nt(q_t, k_t)
        attn[p] = contrib if attn[p] is None else attn[p] + contrib

    if direction == 0:
        for p in range(GLA_PAIRS):
            add_level(0, q_b[p], jnp.concatenate([k_first[p], k_second[p]], axis=0), p)
    p_s = log_a
    t_s = log_a
    for li, s in enumerate(GLA_LEVELS):
        e_q, e_k = exponents(p_s, t_s)
        for p in range(GLA_PAIRS):
            q_t, k_t = scaled(e_q, e_k, p)
            add_level(li + 1, q_t, k_t, p)
        upper = (row & s) != 0
        t_before = pltpu.roll(t_s, s, 0)
        t_after = pltpu.roll(t_s, c - s, 0)
        p_s = p_s + jnp.where(upper, t_before, 0.0)
        t_s = t_s + jnp.where(upper, t_before, t_after)

    e_q, e_k = exponents(p_s, t_s)
    state_decay = jnp.exp2(t_s[0:1, :])
    own_block = lax.broadcasted_iota(jnp.int32, (2 * B_VAL_DIM, LANE), 0) < B_VAL_DIM
    own_block = own_block == (lax.broadcasted_iota(jnp.int32, (2 * B_VAL_DIM, LANE), 1) < B_KEY_DIM)
    zero_v = jnp.zeros((c, B_VAL_DIM), BF16)
    outs = []
    for p in range(GLA_PAIRS):
        v_pair = v[:, 2 * p * B_VAL_DIM:(2 * p + 2) * B_VAL_DIM]
        v_diag = jnp.concatenate(
            [jnp.concatenate([v_pair[:, :B_VAL_DIM], zero_v], axis=1),
             jnp.concatenate([zero_v, v_pair[:, B_VAL_DIM:]], axis=1)], axis=0)
        state_t = st_ref[direction, p]
        q_t = q_b[p] * jnp.exp2(e_q[:, pair[p]]).astype(BF16)
        k_t = k_b[p] * jnp.exp2(e_k[:, pair[p]]).astype(BF16)
        o = _dot_nt(q_t, state_t.astype(BF16)) + _dot(attn[p].astype(BF16), v_diag)
        update = jnp.where(own_block, _dot_tn(v_pair, k_t), 0.0)
        st_ref[direction, p] = state_t * state_decay[:, pair[p]] + update
        outs.append(o)
    return jnp.concatenate(outs, axis=1)


def _gla_kernel(q_ref, k_ref, v_ref, g_ref, z_ref, w2_ref, b2_ref, gn_ref, m_ref,
                o_ref, of_ref, ob_ref, st_ref):
    nblk = q_ref.shape[0] // GLA_BLOCK
    st_ref[...] = jnp.zeros(st_ref.shape, F32)
    block = functools.partial(_gla_block, q_ref=q_ref, k_ref=k_ref, v_ref=v_ref, z_ref=z_ref,
                              w2_ref=w2_ref, b2_ref=b2_ref, m_ref=m_ref, st_ref=st_ref)

    def rows_of(n):
        return pl.ds(pl.multiple_of(n * GLA_BLOCK, GLA_BLOCK), GLA_BLOCK)

    def finish(rows, o):
        normed = []
        for h in range(B_HEADS):
            oh = o[:, h * B_VAL_DIM:(h + 1) * B_VAL_DIM]
            normed.append(oh * _rms_scale(oh))
        g = g_ref[rows, :]
        out = jnp.concatenate(normed, axis=1) * gn_ref[...] * (g * jax.nn.sigmoid(g))
        o_ref[rows, :] = out.astype(o_ref.dtype)

    def first_half(n, carry):
        rows_f, rows_b = rows_of(n), rows_of(nblk - 1 - n)
        of_ref[rows_f, :] = block(0, rows_f)
        ob_ref[rows_b, :] = block(1, rows_b)
        return carry

    def second_half(n, carry):
        rows_f, rows_b = rows_of(n), rows_of(nblk - 1 - n)
        finish(rows_f, block(0, rows_f) + ob_ref[rows_f, :])
        finish(rows_b, of_ref[rows_b, :] + block(1, rows_b))
        return carry

    lax.fori_loop(0, nblk // 2, first_half, 0)
    lax.fori_loop(nblk // 2, nblk, second_half, 0)


def _gla(proj, w2, layer, b2, gn, masks, bsz, seq):
    def col(width, start):
        return pl.BlockSpec((seq, width), lambda b: (b, start // width))

    def whole(a):
        return pl.BlockSpec(a.shape, lambda b: (0,) * a.ndim)

    return pl.pallas_call(
        _gla_kernel,
        grid=(bsz,),
        in_specs=[col(B_QK, COL_BQ), col(B_QK, COL_BK), col(B_V, COL_BV), col(B_V, COL_BG),
                  col(LANE, COL_Z), pl.BlockSpec((None,) + w2.shape[1:], lambda b: (layer, 0, 0)),
                  whole(b2), whole(gn), whole(masks)],
        out_specs=pl.BlockSpec((seq, B_V), lambda b: (b, 0)),
        out_shape=jax.ShapeDtypeStruct((bsz * seq, B_V), BF16),
        scratch_shapes=[pltpu.VMEM((seq, B_V), F32), pltpu.VMEM((seq, B_V), F32),
                        pltpu.VMEM((2, GLA_PAIRS, 2 * B_VAL_DIM, LANE), F32)],
        compiler_params=_cparams(("parallel",)),
        name="gla",
    )(proj, proj, proj, proj, proj, w2, b2, gn, masks)


LRU_SEG = 256
SUBLANES = 8
LRU_PITCH = LRU_SEG + SUBLANES


def _lru_kernel(x_ref, y_ref, cw_ref, cb_ref, wa_ref, wx_ref, ba_ref, bx_ref, lam_ref, o_ref,
                xp_ref, af_ref, hf_ref, ab_ref, hb_ref):
    seq = x_ref.shape[0]
    ch = x_ref.shape[1]
    nseg = seq // LRU_SEG
    halo = SUBLANES
    xp_ref[0:halo, :] = jnp.zeros((halo, ch), F32)
    xp_ref[seq + halo:seq + 2 * halo, :] = jnp.zeros((halo, ch), F32)
    xp_ref[halo:seq + halo, :] = x_ref[...]
    decay_rate = _softplus(-lam_ref[...])
    scan_refs = ((af_ref, hf_ref), (ab_ref, hb_ref))

    def gates(c, carry):
        r0 = pl.multiple_of(c * LRU_SEG, LRU_SEG)
        dst = pl.ds(pl.multiple_of(c * LRU_PITCH, SUBLANES), LRU_SEG)
        ext = LRU_SEG + 2 * halo
        xe = xp_ref[pl.ds(r0, ext), :]
        xc = (cw_ref[0:1, :] * pltpu.roll(xe, 2, 0) + cw_ref[1:2, :] * pltpu.roll(xe, 1, 0)
              + cw_ref[2:3, :] * xe + cw_ref[3:4, :] * pltpu.roll(xe, ext - 1, 0))
        xc = xc[halo:halo + LRU_SEG] + cb_ref[...]
        xb = xc.astype(BF16)
        for d in range(2):
            r = _sigmoid(_dot(xb, wa_ref[d, 0]) + ba_ref[d:d + 1, :])
            i = _sigmoid(_dot(xb, wx_ref[d, 0]) + bx_ref[d:d + 1, :])
            log_a = -LRU_C * r * decay_rate[d:d + 1, :]
            t = jnp.tanh(log_a)
            a_ref, u_ref = scan_refs[d]
            a_ref[dst, :] = jnp.exp(log_a)
            u_ref[dst, :] = jnp.sqrt(-2.0 * t / (1.0 - t)) * (i * xc)
        return carry

    lax.fori_loop(0, nseg, gates, 0)

    def scan_step(j, carry):
        hf, pf, hb, pb = carry
        rf = pl.ds(j, nseg, stride=LRU_PITCH)
        a = af_ref[rf, :]
        hf = a * hf + hf_ref[rf, :]
        pf = a * pf
        hf_ref[rf, :] = hf
        af_ref[rf, :] = pf
        rb = pl.ds(LRU_SEG - 1 - j, nseg, stride=LRU_PITCH)
        a = ab_ref[rb, :]
        hb = a * hb + hb_ref[rb, :]
        pb = a * pb
        hb_ref[rb, :] = hb
        ab_ref[rb, :] = pb
        return hf, pf, hb, pb

    zeros = jnp.zeros((nseg, ch), F32)
    ones = jnp.ones((nseg, ch), F32)
    lax.fori_loop(0, LRU_SEG, scan_step, (zeros, ones, zeros, ones), unroll=4)

    enter_f = [None] * nseg
    state = jnp.zeros((1, ch), F32)
    for s in range(nseg):
        enter_f[s] = state
        last = s * LRU_PITCH + LRU_SEG - 1
        state = hf_ref[last:last + 1, :] + af_ref[last:last + 1, :] * state
    enter_b = [None] * nseg
    state = jnp.zeros((1, ch), F32)
    for s in reversed(range(nseg)):
        enter_b[s] = state
        first = s * LRU_PITCH
        state = hb_ref[first:first + 1, :] + ab_ref[first:first + 1, :] * state

    for s in range(nseg):
        seg = slice(s * LRU_PITCH, s * LRU_PITCH + LRU_SEG)
        rows = slice(s * LRU_SEG, (s + 1) * LRU_SEG)
        h = (hf_ref[seg, :] + af_ref[seg, :] * enter_f[s]) + (hb_ref[seg, :] + ab_ref[seg, :] * enter_b[s])
        o_ref[rows, :] = (h * jax.nn.gelu(y_ref[rows, :])).astype(o_ref.dtype)


def _rglru(proj, conv_w, conv_b, w_a, w_x, layer, b_a, b_x, lam, bsz, seq):
    ch = C_BLOCK_DIM

    def col(start):
        return pl.BlockSpec((seq, ch), lambda b, c: (b, start // ch + c))

    def per_block(rows):
        return pl.BlockSpec((rows, ch), lambda b, c: (0, c))

    gate_w = pl.BlockSpec((None, 2, 1, ch, ch), lambda b, c: (layer, 0, c, 0, 0))
    return pl.pallas_call(
        _lru_kernel,
        grid=(bsz, C_BLOCKS),
        in_specs=[col(COL_CX), col(COL_CY), per_block(CONV_WIDTH), per_block(1), gate_w, gate_w,
                  per_block(2), per_block(2), per_block(2)],
        out_specs=pl.BlockSpec((seq, ch), lambda b, c: (b, c)),
        out_shape=jax.ShapeDtypeStruct((bsz * seq, C_WIDTH), BF16),
        scratch_shapes=([pltpu.VMEM((seq + 2 * SUBLANES, ch), F32)]
                        + [pltpu.VMEM((seq // LRU_SEG * LRU_PITCH, ch), F32)] * 4),
        compiler_params=_cparams(("parallel", "parallel")),
        name="rglru",
    )(proj, proj, conv_w, conv_b.reshape(1, C_WIDTH), w_a, w_x, b_a, b_x, lam)


def _out_proj_kernel(oa_ref, ob_ref, oc_ref, w_hbm, g_ref, x_ref, o_ref, w_ref, stage_ref, sem_ref, *, layer):
    @pl.when(pl.program_id(0) == 0)
    def _():
        _fetch_weight(w_hbm, layer, w_ref, stage_ref, sem_ref)

    for r in _row_groups(x_ref.shape[0]):
        mixed = (_dot(oa_ref[r, :], w_ref[0:A_Q, :]) + _dot(ob_ref[r, :], w_ref[A_Q:A_Q + B_V, :])
                 + _dot(oc_ref[r, :], w_ref[A_Q + B_V:A_Q + B_V + C_WIDTH, :]))
        o_ref[r, :] = x_ref[r, :] + mixed * _rms_scale(mixed) * g_ref[...]


def _out_proj(oa, ob, oc, w, layer, g, x, tm):
    m, d = x.shape

    def rows(width):
        return pl.BlockSpec((tm, width), lambda i: (i, 0))

    return pl.pallas_call(
        functools.partial(_out_proj_kernel, layer=layer),
        grid=(m // tm,),
        in_specs=[rows(A_Q), rows(B_V), rows(C_WIDTH),
                  pl.BlockSpec(memory_space=pl.ANY),
                  pl.BlockSpec((1, d), lambda i: (0, 0)),
                  rows(d)],
        out_specs=rows(d),
        out_shape=jax.ShapeDtypeStruct((m, d), F32),
        scratch_shapes=_weight_scratch(w.shape[1], w.shape[2]),
        compiler_params=_cparams(("arbitrary",)),
        name="out_proj",
    )(oa, ob, oc, w, g.reshape(1, d), x)


def _cross_attn_kernel(x_ref, gpre_ref, wq_hbm, k_ref, v_ref, wo_hbm, gpost_ref, o_ref,
                       wq_ref, wo_ref, stage_ref, sem_ref, *, layer):
    @pl.when(jnp.logical_and(pl.program_id(0) == 0, pl.program_id(1) == 0))
    def _():
        _fetch_weight(wq_hbm, layer, wq_ref, stage_ref, sem_ref)
        _fetch_weight(wo_hbm, layer, wo_ref, stage_ref, sem_ref)

    x = x_ref[...]
    hn = (x * _rms_scale(x) * gpre_ref[...]).astype(BF16)
    q = _dot(hn, wq_ref[...]).astype(BF16)
    acc = None
    for h in range(X_HEADS):
        hs = slice(h * X_HEAD_DIM, (h + 1) * X_HEAD_DIM)
        s = _dot_nt(q[:, hs], k_ref[:, hs]) * (X_HEAD_DIM ** -0.5)
        m = jnp.max(s, axis=-1, keepdims=True)
        p = jnp.exp(s - m)
        o = _dot(p.astype(BF16), v_ref[:, hs]) / jnp.sum(p, axis=-1, keepdims=True)
        part = _dot(o.astype(BF16), wo_ref[hs, :])
        acc = part if acc is None else acc + part
    o_ref[...] = x + acc * _rms_scale(acc) * gpost_ref[...]


def _cross_attention(x, gpre, wq, k, v, wo, layer, gpost, bsz, seq, tq):
    d = D_MODEL
    per_row = seq // tq
    mem_spec = pl.BlockSpec((MEM_LEN, d), lambda b, i: (b, 0))
    gain_spec = pl.BlockSpec((1, d), lambda b, i: (0, 0))
    weight_spec = pl.BlockSpec(memory_space=pl.ANY)
    w_bf, stage, sem = _weight_scratch(d, d)
    row_spec = pl.BlockSpec((tq, d), lambda b, i: (b * per_row + i, 0))
    return pl.pallas_call(
        functools.partial(_cross_attn_kernel, layer=layer),
        grid=(bsz, per_row),
        in_specs=[row_spec, gain_spec, weight_spec, mem_spec, mem_spec, weight_spec, gain_spec],
        out_specs=row_spec,
        out_shape=jax.ShapeDtypeStruct((bsz * seq, d), F32),
        scratch_shapes=[w_bf, w_bf, stage, sem],
        compiler_params=_cparams(("arbitrary", "arbitrary")),
        name="cross_attention",
    )(x, gpre.reshape(1, d), wq, k, v, wo, gpost.reshape(1, d))


def _ffn_kernel(x_ref, gpre_ref, wu_ref, wd_ref, gpost_ref, o_ref, hn_ref, acc_ref):
    f = pl.program_id(1)

    @pl.when(f == 0)
    def _():
        x = x_ref[...]
        hn_ref[...] = (x * _rms_scale(x) * gpre_ref[...]).astype(BF16)
        acc_ref[...] = jnp.zeros(acc_ref.shape, F32)

    up = _dot(hn_ref[...], wu_ref[...])
    act = jnp.square(jnp.maximum(up, 0.0)).astype(BF16)
    acc_ref[...] += _dot(act, wd_ref[...])

    @pl.when(f == pl.num_programs(1) - 1)
    def _():
        ff = acc_ref[...]
        o_ref[...] = x_ref[...] + ff * _rms_scale(ff) * gpost_ref[...]


def _ffn(x, gpre, w_up, w_down, layer, gpost, tm, tf):
    m, d = x.shape
    dff = w_up.shape[2]
    return pl.pallas_call(
        _ffn_kernel,
        grid=(m // tm, dff // tf),
        in_specs=[pl.BlockSpec((tm, d), lambda i, f: (i, 0)),
                  pl.BlockSpec((1, d), lambda i, f: (0, 0)),
                  pl.BlockSpec((None, d, tf), lambda i, f: (layer, 0, f)),
                  pl.BlockSpec((None, tf, d), lambda i, f: (layer, f, 0)),
                  pl.BlockSpec((1, d), lambda i, f: (0, 0))],
        out_specs=pl.BlockSpec((tm, d), lambda i, f: (i, 0)),
        out_shape=jax.ShapeDtypeStruct((m, d), F32),
        scratch_shapes=[pltpu.VMEM((tm, d), BF16), pltpu.VMEM((tm, d), F32)],
        compiler_params=_cparams(("parallel", "arbitrary")),
        name="ffn",
    )(x, gpre.reshape(1, d), w_up, w_down, gpost.reshape(1, d))


def _gate_expansion(w2_f, w2_b):
    w = jnp.zeros((w2_f.shape[0], LANE, 2 * B_QK), F32)
    w = w.at[:, 0:GATE_RANK, 0:B_QK].set(w2_f)
    w = w.at[:, GATE_RANK:2 * GATE_RANK, B_QK:].set(w2_b)
    return w.astype(BF16)


def kernel(x, mem, rel_bias, w_in, w_out, attn_sink, gla_w2_f, gla_b2_f, gla_w2_b, gla_b2_b, gla_norm, conv_w, conv_b, lru_wa, lru_ba, lru_wx, lru_bx, lru_lambda, xq, xk, xv, xo, w_up, w_down, norm_mix_pre, norm_mix_post, norm_mem, norm_x_pre, norm_x_post, norm_ff_pre, norm_ff_post):
    bsz, seq, d = x.shape
    assert d == D_MODEL and seq == SUBLANES * LRU_SEG
    xs = x.reshape(bsz * seq, d)
    mems = mem.reshape(bsz * mem.shape[1], d)
    bias_tab = _attention_bias_table(rel_bias)
    masks = _gla_masks()
    w2_b = _gate_expansion(gla_w2_f, gla_w2_b)
    b2 = jnp.concatenate([gla_b2_f, gla_b2_b], axis=-1)
    bf = lambda w: w.astype(BF16)
    wa_b, wx_b = bf(lru_wa), bf(lru_wx)
    w_up_b, w_down_b = bf(w_up), bf(w_down)
    for l in range(DEPTH):
        proj = _in_proj(xs, norm_mix_pre[l], w_in, l, tm=512)
        oa = _windowed_attention(proj, bias_tab, attn_sink[l], bsz, seq)
        ob = _gla(proj, w2_b, l, b2[l].reshape(1, 2 * B_QK), gla_norm[l].reshape(1, B_V), masks, bsz, seq)
        oc = _rglru(proj, conv_w[l], conv_b[l], wa_b, wx_b, l, lru_ba[l], lru_bx[l], lru_lambda[l],
                    bsz, seq)
        xs = _out_proj(oa, ob, oc, w_out, l, norm_mix_post[l], xs, tm=512)

        k, v = _kv_proj(mems, norm_mem[l], xk, xv, l, tm=512)
        xs = _cross_attention(xs, norm_x_pre[l], xq, k, v, xo, l, norm_x_post[l], bsz, seq, tq=512)

        xs = _ffn(xs, norm_ff_pre[l], w_up_b, w_down_b, l, norm_ff_post[l], tm=512, tf=1024)
    return xs.reshape(bsz, seq, d)
```

```python
import functools
import math

import numpy as np
import jax
import jax.numpy as jnp
from jax import lax
from jax.experimental import pallas as pl
from jax.experimental.pallas import tpu as pltpu

F32 = jnp.float32
BF16 = jnp.bfloat16

D_MODEL = 2048
DEPTH = 4
MEM_LEN = 256
A_HEAD_DIM = 128
A_HEADS = 8
A_KV_HEADS = 2
A_GROUP = 4
WINDOW = 128
BLOCK = 128
N_BUCKETS = 32
MAX_DISTANCE = 128
B_HEADS = 4
B_KEY_DIM = 64
B_VAL_DIM = 128
GATE_RANK = 16
GATE_TAU = 16.0
C_WIDTH = 512
C_BLOCKS = 4
C_BLOCK_DIM = 128
CONV_WIDTH = 4
LRU_C = 8.0
X_HEADS = 4
X_HEAD_DIM = 512
D_FF = 8192
EPS = 1e-6
NEG_INF = -1e30
LOG2E = math.log2(math.e)

A_Q = A_HEADS * A_HEAD_DIM
A_KV = A_KV_HEADS * A_HEAD_DIM
B_QK = B_HEADS * B_KEY_DIM
B_V = B_HEADS * B_VAL_DIM

LANE = 128
COL_AQ = 0
COL_AK = COL_AQ + A_Q
COL_AV = COL_AK + A_KV
COL_BQ = COL_AV + A_KV
COL_BK = COL_BQ + B_QK
COL_BV = COL_BK + B_QK
COL_BG = COL_BV + B_V
COL_CX = COL_BG + B_V
COL_CY = COL_CX + C_WIDTH
COL_Z = COL_CY + C_WIDTH
D_PROJ = COL_Z + LANE

GLA_BLOCK = 128
GLA_LEVELS = (1, 2, 4, 8, 16, 32, 64)
GLA_PAIRS = B_HEADS // 2
VMEM_LIMIT_MIB = 56


def _cparams(semantics, vmem_mib=VMEM_LIMIT_MIB):
    return pltpu.CompilerParams(dimension_semantics=semantics,
                                vmem_limit_bytes=vmem_mib * 1024 * 1024)


def _dot(a, b):
    return jnp.dot(a, b, preferred_element_type=F32)


def _dot_nt(a, b):
    return lax.dot_general(a, b, (((1,), (1,)), ((), ())), preferred_element_type=F32)


def _dot_tn(a, b):
    return lax.dot_general(a, b, (((0,), (0,)), ((), ())), preferred_element_type=F32)


def _softplus(x):
    return jnp.maximum(x, 0.0) + jnp.log(1.0 + jnp.exp(-jnp.abs(x)))


def _sigmoid(x):
    return 0.5 * jnp.tanh(0.5 * x) + 0.5


def _rms_scale(x):
    return lax.rsqrt(jnp.mean(jnp.square(x), axis=-1, keepdims=True) + EPS)


ROW_CHAINS = 2


def _row_groups(rows):
    step = rows // ROW_CHAINS
    return [slice(c * step, (c + 1) * step) for c in range(ROW_CHAINS)]


def _resident(shape, index_map):
    return pl.BlockSpec(shape, index_map, pipeline_mode=pl.Buffered(1))


WEIGHT_CHUNK_ROWS = 256


def _weight_scratch(k, n):
    return [pltpu.VMEM((k, n), BF16), pltpu.VMEM((2, WEIGHT_CHUNK_ROWS, n), F32),
            pltpu.SemaphoreType.DMA((2,))]


def _fetch_weight(w_hbm, layer, dst_ref, stage_ref, sem_ref):
    nchunk = dst_ref.shape[0] // WEIGHT_CHUNK_ROWS

    def copy(c):
        rows = pl.ds(c * WEIGHT_CHUNK_ROWS, WEIGHT_CHUNK_ROWS)
        return pltpu.make_async_copy(w_hbm.at[layer, rows, :], stage_ref.at[c % 2], sem_ref.at[c % 2])

    copy(0).start()
    for c in range(nchunk):
        if c + 1 < nchunk:
            copy(c + 1).start()
        copy(c).wait()
        dst_ref[c * WEIGHT_CHUNK_ROWS:(c + 1) * WEIGHT_CHUNK_ROWS, :] = stage_ref[c % 2].astype(BF16)


def _kv_proj_kernel(x_ref, g_ref, wk_hbm, wv_hbm, k_ref, v_ref, wk_ref, wv_ref, stage_ref, sem_ref, *, layer):
    @pl.when(pl.program_id(0) == 0)
    def _():
        _fetch_weight(wk_hbm, layer, wk_ref, stage_ref, sem_ref)
        _fetch_weight(wv_hbm, layer, wv_ref, stage_ref, sem_ref)

    x = x_ref[...]
    hn = (x * _rms_scale(x) * g_ref[...]).astype(BF16)
    k_ref[...] = _dot(hn, wk_ref[...]).astype(BF16)
    v_ref[...] = _dot(hn, wv_ref[...]).astype(BF16)


def _kv_proj(x, g, wk, wv, layer, tm):
    m, d = x.shape
    weight = pl.BlockSpec(memory_space=pl.ANY)
    rows = pl.BlockSpec((tm, d), lambda i: (i, 0))
    w_bf, stage, sem = _weight_scratch(d, d)
    return pl.pallas_call(
        functools.partial(_kv_proj_kernel, layer=layer),
        grid=(m // tm,),
        in_specs=[rows, pl.BlockSpec((1, d), lambda i: (0, 0)), weight, weight],
        out_specs=[rows, rows],
        out_shape=[jax.ShapeDtypeStruct((m, d), BF16)] * 2,
        scratch_shapes=[w_bf, w_bf, stage, sem],
        compiler_params=_cparams(("arbitrary",)),
        name="kv_proj",
    )(x, g.reshape(1, d), wk, wv)


IN_Z0 = A_Q + 2 * A_KV + 2 * B_QK + 2 * B_V
IN_Z1 = IN_Z0 + 2 * GATE_RANK
D_IN = IN_Z1 + 2 * C_WIDTH
IN_CHUNK_ROWS = 128


def _fetch_w_in(wt_hbm, layer, wm_ref, wc_ref, wz_ref, stage_ref, sem_ref):
    wz_ref[...] = jnp.zeros(wz_ref.shape, BF16)
    pieces = []
    for start, stop, dst in ((0, IN_Z0, wm_ref), (IN_Z0, IN_Z1, wz_ref), (IN_Z1, D_IN, wc_ref)):
        for r in range(start, stop, IN_CHUNK_ROWS):
            pieces.append((r, min(IN_CHUNK_ROWS, stop - r), dst, r - start))

    def copy(c):
        src_row, rows, _, _ = pieces[c]
        return pltpu.make_async_copy(wt_hbm.at[layer, pl.ds(src_row, rows), :],
                                     stage_ref.at[c % 2, pl.ds(0, rows), :], sem_ref.at[c % 2])

    copy(0).start()
    for c, (_, rows, dst, dst_row) in enumerate(pieces):
        if c + 1 < len(pieces):
            copy(c + 1).start()
        copy(c).wait()
        dst[dst_row:dst_row + rows, :] = stage_ref[c % 2, 0:rows, :].astype(BF16)


def _in_proj_kernel(x_ref, g_ref, wt_hbm, o_ref, wm_ref, wc_ref, wz_ref, stage_ref, sem_ref, *, layer):
    @pl.when(pl.program_id(0) == 0)
    def _():
        _fetch_w_in(wt_hbm, layer, wm_ref, wc_ref, wz_ref, stage_ref, sem_ref)

    for r in _row_groups(x_ref.shape[0]):
        x = x_ref[r, :]
        hn = (x * _rms_scale(x) * g_ref[...]).astype(BF16)
        o_ref[r, 0:COL_CX] = _dot_nt(hn, wm_ref[...])
        o_ref[r, COL_CX:COL_Z] = _dot_nt(hn, wc_ref[...])
        o_ref[r, COL_Z:D_PROJ] = _dot_nt(hn, wz_ref[...])


def _in_proj(x, g, w_in, layer, tm):
    m, k = x.shape
    assert w_in.shape[1:] == (k, D_IN)
    wt = jnp.swapaxes(w_in, 1, 2)
    return pl.pallas_call(
        functools.partial(_in_proj_kernel, layer=layer),
        grid=(m // tm,),
        in_specs=[pl.BlockSpec((tm, k), lambda i: (i, 0)), pl.BlockSpec((1, k), lambda i: (0, 0)),
                  pl.BlockSpec(memory_space=pl.ANY)],
        out_specs=pl.BlockSpec((tm, D_PROJ), lambda i: (i, 0)),
        out_shape=jax.ShapeDtypeStruct((m, D_PROJ), F32),
        scratch_shapes=[pltpu.VMEM((IN_Z0, k), BF16), pltpu.VMEM((2 * C_WIDTH, k), BF16),
                        pltpu.VMEM((LANE, k), BF16), pltpu.VMEM((2, IN_CHUNK_ROWS, k), F32),
                        pltpu.SemaphoreType.DMA((2,))],
        compiler_params=_cparams(("arbitrary",)),
        name="in_proj",
    )(x, g.reshape(1, k), wt)


def _attn_kernel(sink_ref, q_ref, k_ref, v_ref, bias_ref, o_ref, kb_ref, vt_ref, p_ref, sw_ref):
    seq = q_ref.shape[0]
    nblk = seq // BLOCK
    kb_ref[...] = k_ref[...].astype(BF16)
    for g in range(A_KV_HEADS):
        for n in range(nblk):
            vt_ref[g * nblk + n] = (
                v_ref[n * BLOCK:(n + 1) * BLOCK, g * A_HEAD_DIM:(g + 1) * A_HEAD_DIM].T.astype(BF16))
    sinks = [jnp.concatenate([jnp.full((1, BLOCK), sink_ref[g * A_GROUP + h] * LOG2E, F32)
                              for h in range(A_GROUP)], axis=1) for g in range(A_KV_HEADS)]

    def tiles(first, last):
        return (1 if first else 0), (2 if last else 3)

    def scores(g, n, first, last):
        t0, t1 = tiles(first, last)
        nk = (t1 - t0) * BLOCK
        r0 = pl.multiple_of(n * BLOCK, BLOCK)
        k = kb_ref[pl.ds(pl.multiple_of(r0 + (t0 - 1) * BLOCK, BLOCK), nk),
                   g * A_HEAD_DIM:(g + 1) * A_HEAD_DIM]
        q = jnp.concatenate(
            [q_ref[pl.ds(r0, BLOCK), (g * A_GROUP + h) * A_HEAD_DIM:(g * A_GROUP + h + 1) * A_HEAD_DIM]
             for h in range(A_GROUP)], axis=0).astype(BF16)
        s = _dot_nt(k, q) * (A_HEAD_DIM ** -0.5 * LOG2E) + bias_ref[g, t0 * BLOCK:t1 * BLOCK, :]
        m = jnp.maximum(jnp.max(s, axis=0, keepdims=True), sinks[g])
        p_ref[g, 0:nk, :] = jnp.exp2(s - m).astype(BF16)
        sw_ref[g] = jnp.exp2(sinks[g] - m)

    def output(g, n, first, last):
        t0, t1 = tiles(first, last)
        nk = (t1 - t0) * BLOCK
        r0 = pl.multiple_of(n * BLOCK, BLOCK)
        v_t = jnp.concatenate([vt_ref[g * nblk + n + t - 1] for t in range(t0, t1)], axis=1)
        ones = jnp.ones((2 * SUBLANES, nk), BF16)
        o = _dot(jnp.concatenate([v_t, ones], axis=0), p_ref[g, 0:nk, :])
        denom = o[A_HEAD_DIM:A_HEAD_DIM + 1, :] + sw_ref[g]
        o = o[0:A_HEAD_DIM, :] / denom
        for h in range(A_GROUP):
            col = (g * A_GROUP + h) * A_HEAD_DIM
            o_ref[pl.ds(r0, BLOCK), col:col + A_HEAD_DIM] = (
                o[:, h * BLOCK:(h + 1) * BLOCK].T.astype(o_ref.dtype))

    def both(fn, *args):
        for g in range(A_KV_HEADS):
            fn(g, *args)

    def body(n, carry):
        both(output, n, False, False)
        both(scores, n + 1, False, False)
        return carry

    both(scores, 0, True, False)
    both(output, 0, True, False)
    both(scores, 1, False, False)
    lax.fori_loop(1, nblk - 2, body, 0)
    both(output, nblk - 2, False, False)
    both(scores, nblk - 1, False, True)
    both(output, nblk - 1, False, True)


def _windowed_attention(proj, bias_tab, sink, bsz, seq):
    qw = A_GROUP * A_HEAD_DIM
    nblk = seq // BLOCK
    return pl.pallas_call(
        _attn_kernel,
        grid=(bsz,),
        in_specs=[pl.BlockSpec(memory_space=pltpu.SMEM),
                  pl.BlockSpec((seq, A_Q), lambda b: (b, COL_AQ // A_Q)),
                  pl.BlockSpec((seq, A_KV), lambda b: (b, COL_AK // A_KV)),
                  pl.BlockSpec((seq, A_KV), lambda b: (b, COL_AV // A_KV)),
                  pl.BlockSpec(bias_tab.shape, lambda b: (0, 0, 0))],
        out_specs=pl.BlockSpec((seq, A_Q), lambda b: (b, 0)),
        out_shape=jax.ShapeDtypeStruct((bsz * seq, A_Q), BF16),
        scratch_shapes=[pltpu.VMEM((seq, A_KV), BF16),
                        pltpu.VMEM((A_KV_HEADS * nblk, A_HEAD_DIM, BLOCK), BF16),
                        pltpu.VMEM((A_KV_HEADS, 3 * BLOCK, qw), BF16),
                        pltpu.VMEM((A_KV_HEADS, 1, qw), F32)],
        compiler_params=_cparams(("parallel",)),
        name="windowed_attention",
    )(sink, proj, proj, proj, bias_tab)


def _t5_bucket(rel):
    nb = N_BUCKETS // 2
    max_exact = nb // 2
    ret = jnp.where(rel > 0, nb, 0)
    n = jnp.abs(rel)
    nf = jnp.maximum(n, 1).astype(F32)
    large = max_exact + (jnp.log(nf / max_exact) / math.log(MAX_DISTANCE / max_exact)
                         * (nb - max_exact)).astype(jnp.int32)
    large = jnp.minimum(large, nb - 1)
    return ret + jnp.where(n < max_exact, n, large)


def _attention_bias_table(rel_bias):
    kj = jnp.arange(3 * BLOCK)[:, None]
    qi = jnp.arange(BLOCK)[None, :]
    rel = kj - BLOCK - qi
    onehot = jax.nn.one_hot(_t5_bucket(rel), N_BUCKETS, dtype=F32)
    bias = jnp.einsum("kqn,nh->hkq", onehot, rel_bias.astype(F32), precision=lax.Precision.HIGHEST)
    bias = jnp.where((jnp.abs(rel) <= WINDOW)[None], bias * LOG2E, NEG_INF)
    bias = bias.reshape(A_KV_HEADS, A_GROUP, 3 * BLOCK, BLOCK)
    return jnp.transpose(bias, (0, 2, 1, 3)).reshape(A_KV_HEADS, 3 * BLOCK, A_GROUP * BLOCK)


def _gla_masks():
    c = GLA_BLOCK
    i = np.arange(c)[:, None]
    j = np.arange(c)[None, :]
    fwd = [i == j]
    for s in GLA_LEVELS:
        fwd.append((i // (2 * s) == j // (2 * s)) & ((i // s) % 2 == 1) & ((j // s) % 2 == 0))
    fwd = np.stack(fwd)
    return jnp.asarray(np.stack([fwd, fwd.transpose(0, 2, 1)]).astype(np.float32))


def _gla_block(direction, rows, q_ref, k_ref, v_ref, z_ref, w2_ref, b2_ref, m_ref, st_ref):
    c = GLA_BLOCK
    q = q_ref[rows, :] * (B_KEY_DIM ** -0.5)
    k = k_ref[rows, :]
    v = v_ref[rows, :].astype(BF16)
    gate_cols = slice(direction * B_QK, (direction + 1) * B_QK)
    pre = _dot(z_ref[rows, :].astype(BF16), w2_ref[:, gate_cols]) + b2_ref[:, gate_cols]
    log_a = -_softplus(-pre) * (LOG2E / GATE_TAU)

    row = lax.broadcasted_iota(jnp.int32, (c, B_QK), 0)
    first_head = lax.broadcasted_iota(jnp.int32, (c, LANE), 1) < B_KEY_DIM
    pair = [slice(p * LANE, (p + 1) * LANE) for p in range(GLA_PAIRS)]
    q_b = [q[:, ps].astype(BF16) for ps in pair]
    k_b = [k[:, ps].astype(BF16) for ps in pair]
    k_first = [jnp.where(first_head, kb, jnp.zeros_like(kb)) for kb in k_b]
    k_second = [jnp.where(first_head, jnp.zeros_like(kb), kb) for kb in k_b]

    def exponents(p_s, t_s):
        if direction == 0:
            return p_s, t_s - p_s
        return t_s - p_s + log_a, p_s - log_a

    def scaled(e_q, e_k, p):
        w_q = jnp.exp2(e_q[:, pair[p]]).astype(BF16)
        w_k = jnp.exp2(e_k[:, pair[p]]).astype(BF16)
        return q_b[p] * w_q, jnp.concatenate([k_first[p] * w_k, k_second[p] * w_k], axis=0)

    attn = [None] * GLA_PAIRS

    def add_level(level, q_t, k_t, p):
        mask = m_ref[direction, level]
        contrib = jnp.concatenate([mask, mask], axis=1) * _dot_nt(q_t, k_t)
        attn[p] = contrib if attn[p] is None else attn[p] + contrib

    if direction == 0:
        for p in range(GLA_PAIRS):
            add_level(0, q_b[p], jnp.concatenate([k_first[p], k_second[p]], axis=0), p)
    p_s = log_a
    t_s = log_a
    for li, s in enumerate(GLA_LEVELS):
        e_q, e_k = exponents(p_s, t_s)
        for p in range(GLA_PAIRS):
            q_t, k_t = scaled(e_q, e_k, p)
            add_level(li + 1, q_t, k_t, p)
        upper = (row & s) != 0
        t_before = pltpu.roll(t_s, s, 0)
        t_after = pltpu.roll(t_s, c - s, 0)
        p_s = p_s + jnp.where(upper, t_before, 0.0)
        t_s = t_s + jnp.where(upper, t_before, t_after)

    e_q, e_k = exponents(p_s, t_s)
    state_decay = jnp.exp2(t_s[0:1, :])
    own_block = lax.broadcasted_iota(jnp.int32, (2 * B_VAL_DIM, LANE), 0) < B_VAL_DIM
    own_block = own_block == (lax.broadcasted_iota(jnp.int32, (2 * B_VAL_DIM, LANE), 1) < B_KEY_DIM)
    zero_v = jnp.zeros((c, B_VAL_DIM), BF16)
    outs = []
    for p in range(GLA_PAIRS):
        v_pair = v[:, 2 * p * B_VAL_DIM:(2 * p + 2) * B_VAL_DIM]
        v_diag = jnp.concatenate(
            [jnp.concatenate([v_pair[:, :B_VAL_DIM], zero_v], axis=1),
             jnp.concatenate([zero_v, v_pair[:, B_VAL_DIM:]], axis=1)], axis=0)
        state_t = st_ref[direction, p]
        q_t = q_b[p] * jnp.exp2(e_q[:, pair[p]]).astype(BF16)
        k_t = k_b[p] * jnp.exp2(e_k[:, pair[p]]).astype(BF16)
        o = _dot_nt(q_t, state_t.astype(BF16)) + _dot(attn[p].astype(BF16), v_diag)
        update = jnp.where(own_block, _dot_tn(v_pair, k_t), 0.0)
        st_ref[direction, p] = state_t * state_decay[:, pair[p]] + update
        outs.append(o)
    return jnp.concatenate(outs, axis=1)


def _gla_kernel(q_ref, k_ref, v_ref, g_ref, z_ref, w2_ref, b2_ref, gn_ref, m_ref,
                o_ref, of_ref, ob_ref, st_ref):
    nblk = q_ref.shape[0] // GLA_BLOCK
    st_ref[...] = jnp.zeros(st_ref.shape, F32)
    block = functools.partial(_gla_block, q_ref=q_ref, k_ref=k_ref, v_ref=v_ref, z_ref=z_ref,
                              w2_ref=w2_ref, b2_ref=b2_ref, m_ref=m_ref, st_ref=st_ref)

    def rows_of(n):
        return pl.ds(pl.multiple_of(n * GLA_BLOCK, GLA_BLOCK), GLA_BLOCK)

    def finish(rows, o):
        normed = []
        for h in range(B_HEADS):
            oh = o[:, h * B_VAL_DIM:(h + 1) * B_VAL_DIM]
            normed.append(oh * _rms_scale(oh))
        g = g_ref[rows, :]
        out = jnp.concatenate(normed, axis=1) * gn_ref[...] * (g * jax.nn.sigmoid(g))
        o_ref[rows, :] = out.astype(o_ref.dtype)

    def first_half(n, carry):
        rows_f, rows_b = rows_of(n), rows_of(nblk - 1 - n)
        of_ref[rows_f, :] = block(0, rows_f)
        ob_ref[rows_b, :] = block(1, rows_b)
        return carry

    def second_half(n, carry):
        rows_f, rows_b = rows_of(n), rows_of(nblk - 1 - n)
        finish(rows_f, block(0, rows_f) + ob_ref[rows_f, :])
        finish(rows_b, of_ref[rows_b, :] + block(1, rows_b))
        return carry

    lax.fori_loop(0, nblk // 2, first_half, 0)
    lax.fori_loop(nblk // 2, nblk, second_half, 0)


def _gla(proj, w2, layer, b2, gn, masks, bsz, seq):
    def col(width, start):
        return pl.BlockSpec((seq, width), lambda b: (b, start // width))

    def whole(a):
        return pl.BlockSpec(a.shape, lambda b: (0,) * a.ndim)

    return pl.pallas_call(
        _gla_kernel,
        grid=(bsz,),
        in_specs=[col(B_QK, COL_BQ), col(B_QK, COL_BK), col(B_V, COL_BV), col(B_V, COL_BG),
                  col(LANE, COL_Z), pl.BlockSpec((None,) + w2.shape[1:], lambda b: (layer, 0, 0)),
                  whole(b2), whole(gn), whole(masks)],
        out_specs=pl.BlockSpec((seq, B_V), lambda b: (b, 0)),
        out_shape=jax.ShapeDtypeStruct((bsz * seq, B_V), BF16),
        scratch_shapes=[pltpu.VMEM((seq, B_V), F32), pltpu.VMEM((seq, B_V), F32),
                        pltpu.VMEM((2, GLA_PAIRS, 2 * B_VAL_DIM, LANE), F32)],
        compiler_params=_cparams(("parallel",)),
        name="gla",
    )(proj, proj, proj, proj, proj, w2, b2, gn, masks)


LRU_SEG = 256
SUBLANES = 8
LRU_PITCH = LRU_SEG + SUBLANES


def _lru_kernel(x_ref, y_ref, cw_ref, cb_ref, wa_ref, wx_ref, ba_ref, bx_ref, lam_ref, o_ref,
                xp_ref, af_ref, hf_ref, ab_ref, hb_ref):
    seq = x_ref.shape[0]
    ch = x_ref.shape[1]
    nseg = seq // LRU_SEG
    halo = SUBLANES
    xp_ref[0:halo, :] = jnp.zeros((halo, ch), F32)
    xp_ref[seq + halo:seq + 2 * halo, :] = jnp.zeros((halo, ch), F32)
    xp_ref[halo:seq + halo, :] = x_ref[...]
    decay_rate = _softplus(-lam_ref[...])
    scan_refs = ((af_ref, hf_ref), (ab_ref, hb_ref))

    def gates(c, carry):
        r0 = pl.multiple_of(c * LRU_SEG, LRU_SEG)
        dst = pl.ds(pl.multiple_of(c * LRU_PITCH, SUBLANES), LRU_SEG)
        ext = LRU_SEG + 2 * halo
        xe = xp_ref[pl.ds(r0, ext), :]
        xc = (cw_ref[0:1, :] * pltpu.roll(xe, 2, 0) + cw_ref[1:2, :] * pltpu.roll(xe, 1, 0)
              + cw_ref[2:3, :] * xe + cw_ref[3:4, :] * pltpu.roll(xe, ext - 1, 0))
        xc = xc[halo:halo + LRU_SEG] + cb_ref[...]
        xb = xc.astype(BF16)
        for d in range(2):
            r = _sigmoid(_dot(xb, wa_ref[d, 0]) + ba_ref[d:d + 1, :])
            i = _sigmoid(_dot(xb, wx_ref[d, 0]) + bx_ref[d:d + 1, :])
            log_a = -LRU_C * r * decay_rate[d:d + 1, :]
            t = jnp.tanh(log_a)
            a_ref, u_ref = scan_refs[d]
            a_ref[dst, :] = jnp.exp(log_a)
            u_ref[dst, :] = jnp.sqrt(-2.0 * t / (1.0 - t)) * (i * xc)
        return carry

    lax.fori_loop(0, nseg, gates, 0)

    def scan_step(j, carry):
        hf, pf, hb, pb = carry
        rf = pl.ds(j, nseg, stride=LRU_PITCH)
        a = af_ref[rf, :]
        hf = a * hf + hf_ref[rf, :]
        pf = a * pf
        hf_ref[rf, :] = hf
        af_ref[rf, :] = pf
        rb = pl.ds(LRU_SEG - 1 - j, nseg, stride=LRU_PITCH)
        a = ab_ref[rb, :]
        hb = a * hb + hb_ref[rb, :]
        pb = a * pb
        hb_ref[rb, :] = hb
        ab_ref[rb, :] = pb
        return hf, pf, hb, pb

    zeros = jnp.zeros((nseg, ch), F32)
    ones = jnp.ones((nseg, ch), F32)
    lax.fori_loop(0, LRU_SEG, scan_step, (zeros, ones, zeros, ones), unroll=4)

    enter_f = [None] * nseg
    state = jnp.zeros((1, ch), F32)
    for s in range(nseg):
        enter_f[s] = state
        last = s * LRU_PITCH + LRU_SEG - 1
        state = hf_ref[last:last + 1, :] + af_ref[last:last + 1, :] * state
    enter_b = [None] * nseg
    state = jnp.zeros((1, ch), F32)
    for s in reversed(range(nseg)):
        enter_b[s] = state
        first = s * LRU_PITCH
        state = hb_ref[first:first + 1, :] + ab_ref[first:first + 1, :] * state

    for s in range(nseg):
        seg = slice(s * LRU_PITCH, s * LRU_PITCH + LRU_SEG)
        rows = slice(s * LRU_SEG, (s + 1) * LRU_SEG)
        h = (hf_ref[seg, :] + af_ref[seg, :] * enter_f[s]) + (hb_ref[seg, :] + ab_ref[seg, :] * enter_b[s])
        o_ref[rows, :] = (h * jax.nn.gelu(y_ref[rows, :])).astype(o_ref.dtype)


def _rglru(proj, conv_w, conv_b, w_a, w_x, layer, b_a, b_x, lam, bsz, seq):
    ch = C_BLOCK_DIM

    def col(start):
        return pl.BlockSpec((seq, ch), lambda b, c: (b, start // ch + c))

    def per_block(rows):
        return pl.BlockSpec((rows, ch), lambda b, c: (0, c))

    gate_w = pl.BlockSpec((None, 2, 1, ch, ch), lambda b, c: (layer, 0, c, 0, 0))
    return pl.pallas_call(
        _lru_kernel,
        grid=(bsz, C_BLOCKS),
        in_specs=[col(COL_CX), col(COL_CY), per_block(CONV_WIDTH), per_block(1), gate_w, gate_w,
                  per_block(2), per_block(2), per_block(2)],
        out_specs=pl.BlockSpec((seq, ch), lambda b, c: (b, c)),
        out_shape=jax.ShapeDtypeStruct((bsz * seq, C_WIDTH), BF16),
        scratch_shapes=([pltpu.VMEM((seq + 2 * SUBLANES, ch), F32)]
                        + [pltpu.VMEM((seq // LRU_SEG * LRU_PITCH, ch), F32)] * 4),
        compiler_params=_cparams(("parallel", "parallel")),
        name="rglru",
    )(proj, proj, conv_w, conv_b.reshape(1, C_WIDTH), w_a, w_x, b_a, b_x, lam)


def _out_proj_kernel(oa_ref, ob_ref, oc_ref, w_hbm, g_ref, x_ref, o_ref, w_ref, stage_ref, sem_ref, *, layer):
    @pl.when(pl.program_id(0) == 0)
    def _():
        _fetch_weight(w_hbm, layer, w_ref, stage_ref, sem_ref)

    for r in _row_groups(x_ref.shape[0]):
        mixed = (_dot(oa_ref[r, :], w_ref[0:A_Q, :]) + _dot(ob_ref[r, :], w_ref[A_Q:A_Q + B_V, :])
                 + _dot(oc_ref[r, :], w_ref[A_Q + B_V:A_Q + B_V + C_WIDTH, :]))
        o_ref[r, :] = x_ref[r, :] + mixed * _rms_scale(mixed) * g_ref[...]


def _out_proj(oa, ob, oc, w, layer, g, x, tm):
    m, d = x.shape

    def rows(width):
        return pl.BlockSpec((tm, width), lambda i: (i, 0))

    return pl.pallas_call(
        functools.partial(_out_proj_kernel, layer=layer),
        grid=(m // tm,),
        in_specs=[rows(A_Q), rows(B_V), rows(C_WIDTH),
                  pl.BlockSpec(memory_space=pl.ANY),
                  pl.BlockSpec((1, d), lambda i: (0, 0)),
                  rows(d)],
        out_specs=rows(d),
        out_shape=jax.ShapeDtypeStruct((m, d), F32),
        scratch_shapes=_weight_scratch(w.shape[1], w.shape[2]),
        compiler_params=_cparams(("arbitrary",)),
        name="out_proj",
    )(oa, ob, oc, w, g.reshape(1, d), x)


def _cross_attn_kernel(x_ref, gpre_ref, wq_hbm, k_ref, v_ref, wo_hbm, gpost_ref, o_ref,
                       wq_ref, wo_ref, stage_ref, sem_ref, *, layer):
    @pl.when(jnp.logical_and(pl.program_id(0) == 0, pl.program_id(1) == 0))
    def _():
        _fetch_weight(wq_hbm, layer, wq_ref, stage_ref, sem_ref)
        _fetch_weight(wo_hbm, layer, wo_ref, stage_ref, sem_ref)

    x = x_ref[...]
    hn = (x * _rms_scale(x) * gpre_ref[...]).astype(BF16)
    q = _dot(hn, wq_ref[...]).astype(BF16)
    acc = None
    for h in range(X_HEADS):
        hs = slice(h * X_HEAD_DIM, (h + 1) * X_HEAD_DIM)
        s = _dot_nt(q[:, hs], k_ref[:, hs]) * (X_HEAD_DIM ** -0.5)
        m = jnp.max(s, axis=-1, keepdims=True)
        p = jnp.exp(s - m)
        o = _dot(p.astype(BF16), v_ref[:, hs]) / jnp.sum(p, axis=-1, keepdims=True)
        part = _dot(o.astype(BF16), wo_ref[hs, :])
        acc = part if acc is None else acc + part
    o_ref[...] = x + acc * _rms_scale(acc) * gpost_ref[...]


def _cross_attention(x, gpre, wq, k, v, wo, layer, gpost, bsz, seq, tq):
    d = D_MODEL
    per_row = seq // tq
    mem_spec = pl.BlockSpec((MEM_LEN, d), lambda b, i: (b, 0))
    gain_spec = pl.BlockSpec((1, d), lambda b, i: (0, 0))
    weight_spec = pl.BlockSpec(memory_space=pl.ANY)
    w_bf, stage, sem = _weight_scratch(d, d)
    row_spec = pl.BlockSpec((tq, d), lambda b, i: (b * per_row + i, 0))
    return pl.pallas_call(
        functools.partial(_cross_attn_kernel, layer=layer),
        grid=(bsz, per_row),
        in_specs=[row_spec, gain_spec, weight_spec, mem_spec, mem_spec, weight_spec, gain_spec],
        out_specs=row_spec,
        out_shape=jax.ShapeDtypeStruct((bsz * seq, d), F32),
        scratch_shapes=[w_bf, w_bf, stage, sem],
        compiler_params=_cparams(("arbitrary", "arbitrary")),
        name="cross_attention",
    )(x, gpre.reshape(1, d), wq, k, v, wo, gpost.reshape(1, d))


def _ffn_kernel(x_ref, gpre_ref, wu_ref, wd_ref, gpost_ref, o_ref, hn_ref, acc_ref):
    f = pl.program_id(1)

    @pl.when(f == 0)
    def _():
        x = x_ref[...]
        hn_ref[...] = (x * _rms_scale(x) * gpre_ref[...]).astype(BF16)
        acc_ref[...] = jnp.zeros(acc_ref.shape, F32)

    up = _dot(hn_ref[...], wu_ref[...])
    act = jnp.square(jnp.maximum(up, 0.0)).astype(BF16)
    acc_ref[...] += _dot(act, wd_ref[...])

    @pl.when(f == pl.num_programs(1) - 1)
    def _():
        ff = acc_ref[...]
        o_ref[...] = x_ref[...] + ff * _rms_scale(ff) * gpost_ref[...]


def _ffn(x, gpre, w_up, w_down, layer, gpost, tm, tf):
    m, d = x.shape
    dff = w_up.shape[2]
    return pl.pallas_call(
        _ffn_kernel,
        grid=(m // tm, dff // tf),
        in_specs=[pl.BlockSpec((tm, d), lambda i, f: (i, 0)),
                  pl.BlockSpec((1, d), lambda i, f: (0, 0)),
                  pl.BlockSpec((None, d, tf), lambda i, f: (layer, 0, f)),
                  pl.BlockSpec((None, tf, d), lambda i, f: (layer, f, 0)),
                  pl.BlockSpec((1, d), lambda i, f: (0, 0))],
        out_specs=pl.BlockSpec((tm, d), lambda i, f: (i, 0)),
        out_shape=jax.ShapeDtypeStruct((m, d), F32),
        scratch_shapes=[pltpu.VMEM((tm, d), BF16), pltpu.VMEM((tm, d), F32)],
        compiler_params=_cparams(("parallel", "arbitrary")),
        name="ffn",
    )(x, gpre.reshape(1, d), w_up, w_down, gpost.reshape(1, d))


def _gate_expansion(w2_f, w2_b):
    w = jnp.zeros((w2_f.shape[0], LANE, 2 * B_QK), F32)
    w = w.at[:, 0:GATE_RANK, 0:B_QK].set(w2_f)
    w = w.at[:, GATE_RANK:2 * GATE_RANK, B_QK:].set(w2_b)
    return w.astype(BF16)


def kernel(x, mem, rel_bias, w_in, w_out, attn_sink, gla_w2_f, gla_b2_f, gla_w2_b, gla_b2_b, gla_norm, conv_w, conv_b, lru_wa, lru_ba, lru_wx, lru_bx, lru_lambda, xq, xk, xv, xo, w_up, w_down, norm_mix_pre, norm_mix_post, norm_mem, norm_x_pre, norm_x_post, norm_ff_pre, norm_ff_post):
    bsz, seq, d = x.shape
    assert d == D_MODEL and seq == SUBLANES * LRU_SEG
    xs = x.reshape(bsz * seq, d)
    mems = mem.reshape(bsz * mem.shape[1], d)
    bias_tab = _attention_bias_table(rel_bias)
    masks = _gla_masks()
    w2_b = _gate_expansion(gla_w2_f, gla_w2_b)
    b2 = jnp.concatenate([gla_b2_f, gla_b2_b], axis=-1)
    bf = lambda w: w.astype(BF16)
    wa_b, wx_b = bf(lru_wa), bf(lru_wx)
    w_up_b, w_down_b = bf(w_up), bf(w_down)
    for l in range(DEPTH):
        proj = _in_proj(xs, norm_mix_pre[l], w_in, l, tm=512)
        oa = _windowed_attention(proj, bias_tab, attn_sink[l], bsz, seq)
        ob = _gla(proj, w2_b, l, b2[l].reshape(1, 2 * B_QK), gla_norm[l].reshape(1, B_V), masks, bsz, seq)
        oc = _rglru(proj, conv_w[l], conv_b[l], wa_b, wx_b, l, lru_ba[l], lru_bx[l], lru_lambda[l],
                    bsz, seq)
        xs = _out_proj(oa, ob, oc, w_out, l, norm_mix_post[l], xs, tm=512)

        k, v = _kv_proj(mems, norm_mem[l], xk, xv, l, tm=512)
        xs = _cross_attention(xs, norm_x_pre[l], xq, k, v, xo, l, norm_x_post[l], bsz, seq, tq=512)

        xs = _ffn(xs, norm_ff_pre[l], w_up_b, w_down_b, l, norm_ff_post[l], tm=512, tf=1024)
    return xs.reshape(bsz, seq, d)
```

```python
import functools
import math

import numpy as np
import jax
import jax.numpy as jnp
from jax import lax
from jax.experimental import pallas as pl
from jax.experimental.pallas import tpu as pltpu

F32 = jnp.float32
BF16 = jnp.bfloat16

D_MODEL = 2048
DEPTH = 4
MEM_LEN = 256
A_HEAD_DIM = 128
A_HEADS = 8
A_KV_HEADS = 2
A_GROUP = 4
WINDOW = 128
BLOCK = 128
N_BUCKETS = 32
MAX_DISTANCE = 128
B_HEADS = 4
B_KEY_DIM = 64
B_VAL_DIM = 128
GATE_RANK = 16
GATE_TAU = 16.0
C_WIDTH = 512
C_BLOCKS = 4
C_BLOCK_DIM = 128
CONV_WIDTH = 4
LRU_C = 8.0
X_HEADS = 4
X_HEAD_DIM = 512
D_FF = 8192
EPS = 1e-6
NEG_INF = -1e30
LOG2E = math.log2(math.e)

A_Q = A_HEADS * A_HEAD_DIM
A_KV = A_KV_HEADS * A_HEAD_DIM
B_QK = B_HEADS * B_KEY_DIM
B_V = B_HEADS * B_VAL_DIM

LANE = 128
COL_AQ = 0
COL_AK = COL_AQ + A_Q
COL_AV = COL_AK + A_KV
COL_BQ = COL_AV + A_KV
COL_BK = COL_BQ + B_QK
COL_BV = COL_BK + B_QK
COL_BG = COL_BV + B_V
COL_CX = COL_BG + B_V
COL_CY = COL_CX + C_WIDTH
COL_Z = COL_CY + C_WIDTH
D_PROJ = COL_Z + LANE

GLA_BLOCK = 128
GLA_LEVELS = (1, 2, 4, 8, 16, 32, 64)
GLA_PAIRS = B_HEADS // 2
VMEM_LIMIT_MIB = 56


def _cparams(semantics, vmem_mib=VMEM_LIMIT_MIB):
    return pltpu.CompilerParams(dimension_semantics=semantics,
                                vmem_limit_bytes=vmem_mib * 1024 * 1024)


def _dot(a, b):
    return jnp.dot(a, b, preferred_element_type=F32)


def _dot_nt(a, b):
    return lax.dot_general(a, b, (((1,), (1,)), ((), ())), preferred_element_type=F32)


def _dot_tn(a, b):
    return lax.dot_general(a, b, (((0,), (0,)), ((), ())), preferred_element_type=F32)


def _softplus(x):
    return jnp.maximum(x, 0.0) + jnp.log(1.0 + jnp.exp(-jnp.abs(x)))


def _sigmoid(x):
    return 0.5 * jnp.tanh(0.5 * x) + 0.5


def _rms_scale(x):
    return lax.rsqrt(jnp.mean(jnp.square(x), axis=-1, keepdims=True) + EPS)


ROW_CHAINS = 2


def _row_groups(rows):
    step = rows // ROW_CHAINS
    return [slice(c * step, (c + 1) * step) for c in range(ROW_CHAINS)]


def _resident(shape, index_map):
    return pl.BlockSpec(shape, index_map, pipeline_mode=pl.Buffered(1))


WEIGHT_CHUNK_ROWS = 256


def _weight_scratch(k, n):
    return [pltpu.VMEM((k, n), BF16), pltpu.VMEM((2, WEIGHT_CHUNK_ROWS, n), F32),
            pltpu.SemaphoreType.DMA((2,))]


def _fetch_weight(w_hbm, layer, dst_ref, stage_ref, sem_ref):
    nchunk = dst_ref.shape[0] // WEIGHT_CHUNK_ROWS

    def copy(c):
        rows = pl.ds(c * WEIGHT_CHUNK_ROWS, WEIGHT_CHUNK_ROWS)
        return pltpu.make_async_copy(w_hbm.at[layer, rows, :], stage_ref.at[c % 2], sem_ref.at[c % 2])

    copy(0).start()
    for c in range(nchunk):
        if c + 1 < nchunk:
            copy(c + 1).start()
        copy(c).wait()
        dst_ref[c * WEIGHT_CHUNK_ROWS:(c + 1) * WEIGHT_CHUNK_ROWS, :] = stage_ref[c % 2].astype(BF16)


def _kv_proj_kernel(x_ref, g_ref, wk_hbm, wv_hbm, k_ref, v_ref, wk_ref, wv_ref, stage_ref, sem_ref, *, layer):
    @pl.when(pl.program_id(0) == 0)
    def _():
        _fetch_weight(wk_hbm, layer, wk_ref, stage_ref, sem_ref)
        _fetch_weight(wv_hbm, layer, wv_ref, stage_ref, sem_ref)

    x = x_ref[...]
    hn = (x * _rms_scale(x) * g_ref[...]).astype(BF16)
    k_ref[...] = _dot(hn, wk_ref[...]).astype(BF16)
    v_ref[...] = _dot(hn, wv_ref[...]).astype(BF16)


def _kv_proj(x, g, wk, wv, layer, tm):
    m, d = x.shape
    weight = pl.BlockSpec(memory_space=pl.ANY)
    rows = pl.BlockSpec((tm, d), lambda i: (i, 0))
    w_bf, stage, sem = _weight_scratch(d, d)
    return pl.pallas_call(
        functools.partial(_kv_proj_kernel, layer=layer),
        grid=(m // tm,),
        in_specs=[rows, pl.BlockSpec((1, d), lambda i: (0, 0)), weight, weight],
        out_specs=[rows, rows],
        out_shape=[jax.ShapeDtypeStruct((m, d), BF16)] * 2,
        scratch_shapes=[w_bf, w_bf, stage, sem],
        compiler_params=_cparams(("arbitrary",)),
        name="kv_proj",
    )(x, g.reshape(1, d), wk, wv)


IN_Z0 = A_Q + 2 * A_KV + 2 * B_QK + 2 * B_V
IN_Z1 = IN_Z0 + 2 * GATE_RANK
D_IN = IN_Z1 + 2 * C_WIDTH
IN_CHUNK_ROWS = 512


def _fetch_w_in(wt_hbm, layer, wm_ref, wc_ref, wz_ref, stage_ref, sem_ref):
    wz_ref[...] = jnp.zeros(wz_ref.shape, BF16)
    pieces = []
    for start, stop, dst in ((0, IN_Z0, wm_ref), (IN_Z0, IN_Z1, wz_ref), (IN_Z1, D_IN, wc_ref)):
        for r in range(start, stop, IN_CHUNK_ROWS):
            pieces.append((r, min(IN_CHUNK_ROWS, stop - r), dst, r - start))

    def copy(c):
        src_row, rows, _, _ = pieces[c]
        return pltpu.make_async_copy(wt_hbm.at[layer, pl.ds(src_row, rows), :],
                                     stage_ref.at[c % 2, pl.ds(0, rows), :], sem_ref.at[c % 2])

    copy(0).start()
    for c, (_, rows, dst, dst_row) in enumerate(pieces):
        if c + 1 < len(pieces):
            copy(c + 1).start()
        copy(c).wait()
        dst[dst_row:dst_row + rows, :] = stage_ref[c % 2, 0:rows, :].astype(BF16)


def _in_proj_kernel(x_ref, g_ref, wt_hbm, o_ref, wm_ref, wc_ref, wz_ref, stage_ref, sem_ref, *, layer):
    @pl.when(pl.program_id(0) == 0)
    def _():
        _fetch_w_in(wt_hbm, layer, wm_ref, wc_ref, wz_ref, stage_ref, sem_ref)

    for r in _row_groups(x_ref.shape[0]):
        x = x_ref[r, :]
        hn = (x * _rms_scale(x) * g_ref[...]).astype(BF16)
        o_ref[r, 0:COL_CX] = _dot_nt(hn, wm_ref[...])
        o_ref[r, COL_CX:COL_Z] = _dot_nt(hn, wc_ref[...])
        o_ref[r, COL_Z:D_PROJ] = _dot_nt(hn, wz_ref[...])


def _in_proj(x, g, w_in, layer, tm):
    m, k = x.shape
    assert w_in.shape[1:] == (k, D_IN)
    wt = jnp.swapaxes(w_in, 1, 2)
    return pl.pallas_call(
        functools.partial(_in_proj_kernel, layer=layer),
        grid=(m // tm,),
        in_specs=[pl.BlockSpec((tm, k), lambda i: (i, 0)), pl.BlockSpec((1, k), lambda i: (0, 0)),
                  pl.BlockSpec(memory_space=pl.ANY)],
        out_specs=pl.BlockSpec((tm, D_PROJ), lambda i: (i, 0)),
        out_shape=jax.ShapeDtypeStruct((m, D_PROJ), F32),
        scratch_shapes=[pltpu.VMEM((IN_Z0, k), BF16), pltpu.VMEM((2 * C_WIDTH, k), BF16),
                        pltpu.VMEM((LANE, k), BF16), pltpu.VMEM((2, IN_CHUNK_ROWS, k), F32),
                        pltpu.SemaphoreType.DMA((2,))],
        compiler_params=_cparams(("arbitrary",)),
        name="in_proj",
    )(x, g.reshape(1, k), wt)


def _attn_kernel(sink_ref, q_ref, k_ref, v_ref, bias_ref, o_ref, kb_ref, vt_ref, p_ref, sw_ref):
    seq = q_ref.shape[0]
    nblk = seq // BLOCK
    kb_ref[...] = k_ref[...].astype(BF16)
    for g in range(A_KV_HEADS):
        for n in range(nblk):
            vt_ref[g * nblk + n] = (
                v_ref[n * BLOCK:(n + 1) * BLOCK, g * A_HEAD_DIM:(g + 1) * A_HEAD_DIM].T.astype(BF16))
    sinks = [jnp.concatenate([jnp.full((1, BLOCK), sink_ref[g * A_GROUP + h] * LOG2E, F32)
                              for h in range(A_GROUP)], axis=1) for g in range(A_KV_HEADS)]

    def tiles(first, last):
        return (1 if first else 0), (2 if last else 3)

    def scores(g, n, first, last):
        t0, t1 = tiles(first, last)
        nk = (t1 - t0) * BLOCK
        r0 = pl.multiple_of(n * BLOCK, BLOCK)
        k = kb_ref[pl.ds(pl.multiple_of(r0 + (t0 - 1) * BLOCK, BLOCK), nk),
                   g * A_HEAD_DIM:(g + 1) * A_HEAD_DIM]
        q = jnp.concatenate(
            [q_ref[pl.ds(r0, BLOCK), (g * A_GROUP + h) * A_HEAD_DIM:(g * A_GROUP + h + 1) * A_HEAD_DIM]
             for h in range(A_GROUP)], axis=0).astype(BF16)
        s = _dot_nt(k, q) * (A_HEAD_DIM ** -0.5 * LOG2E) + bias_ref[g, t0 * BLOCK:t1 * BLOCK, :]
        m = jnp.maximum(jnp.max(s, axis=0, keepdims=True), sinks[g])
        p_ref[g, 0:nk, :] = jnp.exp2(s - m).astype(BF16)
        sw_ref[g] = jnp.exp2(sinks[g] - m)

    def output(g, n, first, last):
        t0, t1 = tiles(first, last)
        nk = (t1 - t0) * BLOCK
        r0 = pl.multiple_of(n * BLOCK, BLOCK)
        v_t = jnp.concatenate([vt_ref[g * nblk + n + t - 1] for t in range(t0, t1)], axis=1)
        ones = jnp.ones((2 * SUBLANES, nk), BF16)
        o = _dot(jnp.concatenate([v_t, ones], axis=0), p_ref[g, 0:nk, :])
        denom = o[A_HEAD_DIM:A_HEAD_DIM + 1, :] + sw_ref[g]
        o = o[0:A_HEAD_DIM, :] / denom
        for h in range(A_GROUP):
            col = (g * A_GROUP + h) * A_HEAD_DIM
            o_ref[pl.ds(r0, BLOCK), col:col + A_HEAD_DIM] = (
                o[:, h * BLOCK:(h + 1) * BLOCK].T.astype(o_ref.dtype))

    def both(fn, *args):
        for g in range(A_KV_HEADS):
            fn(g, *args)

    def body(n, carry):
        both(output, n, False, False)
        both(scores, n + 1, False, False)
        return carry

    both(scores, 0, True, False)
    both(output, 0, True, False)
    both(scores, 1, False, False)
    lax.fori_loop(1, nblk - 2, body, 0)
    both(output, nblk - 2, False, False)
    both(scores, nblk - 1, False, True)
    both(output, nblk - 1, False, True)


def _windowed_attention(proj, bias_tab, sink, bsz, seq):
    qw = A_GROUP * A_HEAD_DIM
    nblk = seq // BLOCK
    return pl.pallas_call(
        _attn_kernel,
        grid=(bsz,),
        in_specs=[pl.BlockSpec(memory_space=pltpu.SMEM),
                  pl.BlockSpec((seq, A_Q), lambda b: (b, COL_AQ // A_Q)),
                  pl.BlockSpec((seq, A_KV), lambda b: (b, COL_AK // A_KV)),
                  pl.BlockSpec((seq, A_KV), lambda b: (b, COL_AV // A_KV)),
                  pl.BlockSpec(bias_tab.shape, lambda b: (0, 0, 0))],
        out_specs=pl.BlockSpec((seq, A_Q), lambda b: (b, 0)),
        out_shape=jax.ShapeDtypeStruct((bsz * seq, A_Q), BF16),
        scratch_shapes=[pltpu.VMEM((seq, A_KV), BF16),
                        pltpu.VMEM((A_KV_HEADS * nblk, A_HEAD_DIM, BLOCK), BF16),
                        pltpu.VMEM((A_KV_HEADS, 3 * BLOCK, qw), BF16),
                        pltpu.VMEM((A_KV_HEADS, 1, qw), F32)],
        compiler_params=_cparams(("parallel",)),
        name="windowed_attention",
    )(sink, proj, proj, proj, bias_tab)


def _t5_bucket(rel):
    nb = N_BUCKETS // 2
    max_exact = nb // 2
    ret = jnp.where(rel > 0, nb, 0)
    n = jnp.abs(rel)
    nf = jnp.maximum(n, 1).astype(F32)
    large = max_exact + (jnp.log(nf / max_exact) / math.log(MAX_DISTANCE / max_exact)
                         * (nb - max_exact)).astype(jnp.int32)
    large = jnp.minimum(large, nb - 1)
    return ret + jnp.where(n < max_exact, n, large)


def _attention_bias_table(rel_bias):
    kj = jnp.arange(3 * BLOCK)[:, None]
    qi = jnp.arange(BLOCK)[None, :]
    rel = kj - BLOCK - qi
    onehot = jax.nn.one_hot(_t5_bucket(rel), N_BUCKETS, dtype=F32)
    bias = jnp.einsum("kqn,nh->hkq", onehot, rel_bias.astype(F32), precision=lax.Precision.HIGHEST)
    bias = jnp.where((jnp.abs(rel) <= WINDOW)[None], bias * LOG2E, NEG_INF)
    bias = bias.reshape(A_KV_HEADS, A_GROUP, 3 * BLOCK, BLOCK)
    return jnp.transpose(bias, (0, 2, 1, 3)).reshape(A_KV_HEADS, 3 * BLOCK, A_GROUP * BLOCK)


def _gla_masks():
    c = GLA_BLOCK
    i = np.arange(c)[:, None]
    j = np.arange(c)[None, :]
    fwd = [i == j]
    for s in GLA_LEVELS:
        fwd.append((i // (2 * s) == j // (2 * s)) & ((i // s) % 2 == 1) & ((j // s) % 2 == 0))
    fwd = np.stack(fwd)
    return jnp.asarray(np.stack([fwd, fwd.transpose(0, 2, 1)]).astype(np.float32))


def _gla_block(direction, rows, q_ref, k_ref, v_ref, z_ref, w2_ref, b2_ref, m_ref, st_ref):
    c = GLA_BLOCK
    q = q_ref[rows, :] * (B_KEY_DIM ** -0.5)
    k = k_ref[rows, :]
    v = v_ref[rows, :].astype(BF16)
    gate_cols = slice(direction * B_QK, (direction + 1) * B_QK)
    pre = _dot(z_ref[rows, :].astype(BF16), w2_ref[:, gate_cols]) + b2_ref[:, gate_cols]
    log_a = -_softplus(-pre) * (LOG2E / GATE_TAU)

    row = lax.broadcasted_iota(jnp.int32, (c, B_QK), 0)
    first_head = lax.broadcasted_iota(jnp.int32, (c, LANE), 1) < B_KEY_DIM
    pair = [slice(p * LANE, (p + 1) * LANE) for p in range(GLA_PAIRS)]
    q_b = [q[:, ps].astype(BF16) for ps in pair]
    k_b = [k[:, ps].astype(BF16) for ps in pair]
    k_first = [jnp.where(first_head, kb, jnp.zeros_like(kb)) for kb in k_b]
    k_second = [jnp.where(first_head, jnp.zeros_like(kb), kb) for kb in k_b]

    def exponents(p_s, t_s):
        if direction == 0:
            return p_s, t_s - p_s
        return t_s - p_s + log_a, p_s - log_a

    def scaled(e_q, e_k, p):
        w_q = jnp.exp2(e_q[:, pair[p]]).astype(BF16)
        w_k = jnp.exp2(e_k[:, pair[p]]).astype(BF16)
        return q_b[p] * w_q, jnp.concatenate([k_first[p] * w_k, k_second[p] * w_k], axis=0)

    attn = [None] * GLA_PAIRS

    def add_level(level, q_t, k_t, p):
        mask = m_ref[direction, level]
        contrib = jnp.concatenate([mask, mask], axis=1) * _dot_nt(q_t, k_t)
        attn[p] = contrib if attn[p] is None else attn[p] + contrib

    if direction == 0:
        for p in range(GLA_PAIRS):
            add_level(0, q_b[p], jnp.concatenate([k_first[p], k_second[p]], axis=0), p)
    p_s = log_a
    t_s = log_a
    for li, s in enumerate(GLA_LEVELS):
        e_q, e_k = exponents(p_s, t_s)
        for p in range(GLA_PAIRS):
            q_t, k_t = scaled(e_q, e_k, p)
            add_level(li + 1, q_t, k_t, p)
        upper = (row & s) != 0
        t_before = pltpu.roll(t_s, s, 0)
        t_after = pltpu.roll(t_s, c - s, 0)
        p_s = p_s + jnp.where(upper, t_before, 0.0)
        t_s = t_s + jnp.where(upper, t_before, t_after)

    e_q, e_k = exponents(p_s, t_s)
    state_decay = jnp.exp2(t_s[0:1, :])
    own_block = lax.broadcasted_iota(jnp.int32, (2 * B_VAL_DIM, LANE), 0) < B_VAL_DIM
    own_block = own_block == (lax.broadcasted_iota(jnp.int32, (2 * B_VAL_DIM, LANE), 1) < B_KEY_DIM)
    zero_v = jnp.zeros((c, B_VAL_DIM), BF16)
    outs = []
    for p in range(GLA_PAIRS):
        v_pair = v[:, 2 * p * B_VAL_DIM:(2 * p + 2) * B_VAL_DIM]
        v_diag = jnp.concatenate(
            [jnp.concatenate([v_pair[:, :B_VAL_DIM], zero_v], axis=1),
             jnp.concatenate([zero_v, v_pair[:, B_VAL_DIM:]], axis=1)], axis=0)
        state_t = st_ref[direction, p]
        q_t = q_b[p] * jnp.exp2(e_q[:, pair[p]]).astype(BF16)
        k_t = k_b[p] * jnp.exp2(e_k[:, pair[p]]).astype(BF16)
        o = _dot_nt(q_t, state_t.astype(BF16)) + _dot(attn[p].astype(BF16), v_diag)
        update = jnp.where(own_block, _dot_tn(v_pair, k_t), 0.0)
        st_ref[direction, p] = state_t * state_decay[:, pair[p]] + update
        outs.append(o)
    return jnp.concatenate(outs, axis=1)


def _gla_kernel(q_ref, k_ref, v_ref, g_ref, z_ref, w2_ref, b2_ref, gn_ref, m_ref,
                o_ref, of_ref, ob_ref, st_ref):
    nblk = q_ref.shape[0] // GLA_BLOCK
    st_ref[...] = jnp.zeros(st_ref.shape, F32)
    block = functools.partial(_gla_block, q_ref=q_ref, k_ref=k_ref, v_ref=v_ref, z_ref=z_ref,
                              w2_ref=w2_ref, b2_ref=b2_ref, m_ref=m_ref, st_ref=st_ref)

    def rows_of(n):
        return pl.ds(pl.multiple_of(n * GLA_BLOCK, GLA_BLOCK), GLA_BLOCK)

    def finish(rows, o):
        normed = []
        for h in range(B_HEADS):
            oh = o[:, h * B_VAL_DIM:(h + 1) * B_VAL_DIM]
            normed.append(oh * _rms_scale(oh))
        g = g_ref[rows, :]
        out = jnp.concatenate(normed, axis=1) * gn_ref[...] * (g * jax.nn.sigmoid(g))
        o_ref[rows, :] = out.astype(o_ref.dtype)

    def first_half(n, carry):
        rows_f, rows_b = rows_of(n), rows_of(nblk - 1 - n)
        of_ref[rows_f, :] = block(0, rows_f)
        ob_ref[rows_b, :] = block(1, rows_b)
        return carry

    def second_half(n, carry):
        rows_f, rows_b = rows_of(n), rows_of(nblk - 1 - n)
        finish(rows_f, block(0, rows_f) + ob_ref[rows_f, :])
        finish(rows_b, of_ref[rows_b, :] + block(1, rows_b))
        return carry

    lax.fori_loop(0, nblk // 2, first_half, 0)
    lax.fori_loop(nblk // 2, nblk, second_half, 0)


def _gla(proj, w2, layer, b2, gn, masks, bsz, seq):
    def col(width, start):
        return pl.BlockSpec((seq, width), lambda b: (b, start // width))

    def whole(a):
        return pl.BlockSpec(a.shape, lambda b: (0,) * a.ndim)

    return pl.pallas_call(
        _gla_kernel,
        grid=(bsz,),
        in_specs=[col(B_QK, COL_BQ), col(B_QK, COL_BK), col(B_V, COL_BV), col(B_V, COL_BG),
                  col(LANE, COL_Z), pl.BlockSpec((None,) + w2.shape[1:], lambda b: (layer, 0, 0)),
                  whole(b2), whole(gn), whole(masks)],
        out_specs=pl.BlockSpec((seq, B_V), lambda b: (b, 0)),
        out_shape=jax.ShapeDtypeStruct((bsz * seq, B_V), BF16),
        scratch_shapes=[pltpu.VMEM((seq, B_V), F32), pltpu.VMEM((seq, B_V), F32),
                        pltpu.VMEM((2, GLA_PAIRS, 2 * B_VAL_DIM, LANE), F32)],
        compiler_params=_cparams(("parallel",)),
        name="gla",
    )(proj, proj, proj, proj, proj, w2, b2, gn, masks)


LRU_SEG = 256
SUBLANES = 8
LRU_PITCH = LRU_SEG + SUBLANES


def _lru_kernel(x_ref, y_ref, cw_ref, cb_ref, wa_ref, wx_ref, ba_ref, bx_ref, lam_ref, o_ref,
                xp_ref, af_ref, hf_ref, ab_ref, hb_ref):
    seq = x_ref.shape[0]
    ch = x_ref.shape[1]
    nseg = seq // LRU_SEG
    halo = SUBLANES
    xp_ref[0:halo, :] = jnp.zeros((halo, ch), F32)
    xp_ref[seq + halo:seq + 2 * halo, :] = jnp.zeros((halo, ch), F32)
    xp_ref[halo:seq + halo, :] = x_ref[...]
    decay_rate = _softplus(-lam_ref[...])
    scan_refs = ((af_ref, hf_ref), (ab_ref, hb_ref))

    def gates(c, carry):
        r0 = pl.multiple_of(c * LRU_SEG, LRU_SEG)
        dst = pl.ds(pl.multiple_of(c * LRU_PITCH, SUBLANES), LRU_SEG)
        ext = LRU_SEG + 2 * halo
        xe = xp_ref[pl.ds(r0, ext), :]
        xc = (cw_ref[0:1, :] * pltpu.roll(xe, 2, 0) + cw_ref[1:2, :] * pltpu.roll(xe, 1, 0)
              + cw_ref[2:3, :] * xe + cw_ref[3:4, :] * pltpu.roll(xe, ext - 1, 0))
        xc = xc[halo:halo + LRU_SEG] + cb_ref[...]
        xb = xc.astype(BF16)
        for d in range(2):
            r = _sigmoid(_dot(xb, wa_ref[d, 0]) + ba_ref[d:d + 1, :])
            i = _sigmoid(_dot(xb, wx_ref[d, 0]) + bx_ref[d:d + 1, :])
            log_a = -LRU_C * r * decay_rate[d:d + 1, :]
            t = jnp.tanh(log_a)
            a_ref, u_ref = scan_refs[d]
            a_ref[dst, :] = jnp.exp(log_a)
            u_ref[dst, :] = jnp.sqrt(-2.0 * t / (1.0 - t)) * (i * xc)
        return carry

    lax.fori_loop(0, nseg, gates, 0)

    def scan_step(j, carry):
        hf, pf, hb, pb = carry
        rf = pl.ds(j, nseg, stride=LRU_PITCH)
        a = af_ref[rf, :]
        hf = a * hf + hf_ref[rf, :]
        pf = a * pf
        hf_ref[rf, :] = hf
        af_ref[rf, :] = pf
        rb = pl.ds(LRU_SEG - 1 - j, nseg, stride=LRU_PITCH)
        a = ab_ref[rb, :]
        hb = a * hb + hb_ref[rb, :]
        pb = a * pb
        hb_ref[rb, :] = hb
        ab_ref[rb, :] = pb
        return hf, pf, hb, pb

    zeros = jnp.zeros((nseg, ch), F32)
    ones = jnp.ones((nseg, ch), F32)
    lax.fori_loop(0, LRU_SEG, scan_step, (zeros, ones, zeros, ones), unroll=4)

    enter_f = [None] * nseg
    state = jnp.zeros((1, ch), F32)
    for s in range(nseg):
        enter_f[s] = state
        last = s * LRU_PITCH + LRU_SEG - 1
        state = hf_ref[last:last + 1, :] + af_ref[last:last + 1, :] * state
    enter_b = [None] * nseg
    state = jnp.zeros((1, ch), F32)
    for s in reversed(range(nseg)):
        enter_b[s] = state
        first = s * LRU_PITCH
        state = hb_ref[first:first + 1, :] + ab_ref[first:first + 1, :] * state

    for s in range(nseg):
        seg = slice(s * LRU_PITCH, s * LRU_PITCH + LRU_SEG)
        rows = slice(s * LRU_SEG, (s + 1) * LRU_SEG)
        h = (hf_ref[seg, :] + af_ref[seg, :] * enter_f[s]) + (hb_ref[seg, :] + ab_ref[seg, :] * enter_b[s])
        o_ref[rows, :] = (h * jax.nn.gelu(y_ref[rows, :])).astype(o_ref.dtype)


def _rglru(proj, conv_w, conv_b, w_a, w_x, layer, b_a, b_x, lam, bsz, seq):
    ch = C_BLOCK_DIM

    def col(start):
        return pl.BlockSpec((seq, ch), lambda b, c: (b, start // ch + c))

    def per_block(rows):
        return pl.BlockSpec((rows, ch), lambda b, c: (0, c))

    gate_w = pl.BlockSpec((None, 2, 1, ch, ch), lambda b, c: (layer, 0, c, 0, 0))
    return pl.pallas_call(
        _lru_kernel,
        grid=(bsz, C_BLOCKS),
        in_specs=[col(COL_CX), col(COL_CY), per_block(CONV_WIDTH), per_block(1), gate_w, gate_w,
                  per_block(2), per_block(2), per_block(2)],
        out_specs=pl.BlockSpec((seq, ch), lambda b, c: (b, c)),
        out_shape=jax.ShapeDtypeStruct((bsz * seq, C_WIDTH), BF16),
        scratch_shapes=([pltpu.VMEM((seq + 2 * SUBLANES, ch), F32)]
                        + [pltpu.VMEM((seq // LRU_SEG * LRU_PITCH, ch), F32)] * 4),
        compiler_params=_cparams(("parallel", "parallel")),
        name="rglru",
    )(proj, proj, conv_w, conv_b.reshape(1, C_WIDTH), w_a, w_x, b_a, b_x, lam)


def _out_proj_kernel(oa_ref, ob_ref, oc_ref, w_hbm, g_ref, x_ref, o_ref, w_ref, stage_ref, sem_ref, *, layer):
    @pl.when(pl.program_id(0) == 0)
    def _():
        _fetch_weight(w_hbm, layer, w_ref, stage_ref, sem_ref)

    for r in _row_groups(x_ref.shape[0]):
        mixed = (_dot(oa_ref[r, :], w_ref[0:A_Q, :]) + _dot(ob_ref[r, :], w_ref[A_Q:A_Q + B_V, :])
                 + _dot(oc_ref[r, :], w_ref[A_Q + B_V:A_Q + B_V + C_WIDTH, :]))
        o_ref[r, :] = x_ref[r, :] + mixed * _rms_scale(mixed) * g_ref[...]


def _out_proj(oa, ob, oc, w, layer, g, x, tm):
    m, d = x.shape

    def rows(width):
        return pl.BlockSpec((tm, width), lambda i: (i, 0))

    return pl.pallas_call(
        functools.partial(_out_proj_kernel, layer=layer),
        grid=(m // tm,),
        in_specs=[rows(A_Q), rows(B_V), rows(C_WIDTH),
                  pl.BlockSpec(memory_space=pl.ANY),
                  pl.BlockSpec((1, d), lambda i: (0, 0)),
                  rows(d)],
        out_specs=rows(d),
        out_shape=jax.ShapeDtypeStruct((m, d), F32),
        scratch_shapes=_weight_scratch(w.shape[1], w.shape[2]),
        compiler_params=_cparams(("arbitrary",)),
        name="out_proj",
    )(oa, ob, oc, w, g.reshape(1, d), x)


def _cross_attn_kernel(x_ref, gpre_ref, wq_hbm, k_ref, v_ref, wo_hbm, gpost_ref, o_ref,
                       wq_ref, wo_ref, stage_ref, sem_ref, *, layer):
    @pl.when(jnp.logical_and(pl.program_id(0) == 0, pl.program_id(1) == 0))
    def _():
        _fetch_weight(wq_hbm, layer, wq_ref, stage_ref, sem_ref)
        _fetch_weight(wo_hbm, layer, wo_ref, stage_ref, sem_ref)

    x = x_ref[...]
    hn = (x * _rms_scale(x) * gpre_ref[...]).astype(BF16)
    q = _dot(hn, wq_ref[...]).astype(BF16)
    acc = None
    for h in range(X_HEADS):
        hs = slice(h * X_HEAD_DIM, (h + 1) * X_HEAD_DIM)
        s = _dot_nt(q[:, hs], k_ref[:, hs]) * (X_HEAD_DIM ** -0.5)
        m = jnp.max(s, axis=-1, keepdims=True)
        p = jnp.exp(s - m)
        o = _dot(p.astype(BF16), v_ref[:, hs]) / jnp.sum(p, axis=-1, keepdims=True)
        part = _dot(o.astype(BF16), wo_ref[hs, :])
        acc = part if acc is None else acc + part
    o_ref[...] = x + acc * _rms_scale(acc) * gpost_ref[...]


def _cross_attention(x, gpre, wq, k, v, wo, layer, gpost, bsz, seq, tq):
    d = D_MODEL
    per_row = seq // tq
    mem_spec = pl.BlockSpec((MEM_LEN, d), lambda b, i: (b, 0))
    gain_spec = pl.BlockSpec((1, d), lambda b, i: (0, 0))
    weight_spec = pl.BlockSpec(memory_space=pl.ANY)
    w_bf, stage, sem = _weight_scratch(d, d)
    row_spec = pl.BlockSpec((tq, d), lambda b, i: (b * per_row + i, 0))
    return pl.pallas_call(
        functools.partial(_cross_attn_kernel, layer=layer),
        grid=(bsz, per_row),
        in_specs=[row_spec, gain_spec, weight_spec, mem_spec, mem_spec, weight_spec, gain_spec],
        out_specs=row_spec,
        out_shape=jax.ShapeDtypeStruct((bsz * seq, d), F32),
        scratch_shapes=[w_bf, w_bf, stage, sem],
        compiler_params=_cparams(("arbitrary", "arbitrary")),
        name="cross_attention",
    )(x, gpre.reshape(1, d), wq, k, v, wo, gpost.reshape(1, d))


def _ffn_kernel(x_ref, gpre_ref, wu_ref, wd_ref, gpost_ref, o_ref, hn_ref, acc_ref):
    f = pl.program_id(1)
    last = pl.num_programs(1) - 1

    def step(is_first, is_last):
        for r in _row_groups(x_ref.shape[0]):
            if is_first:
                x = x_ref[r, :]
                hn_ref[r, :] = (x * _rms_scale(x) * gpre_ref[...]).astype(BF16)
            up = _dot(hn_ref[r, :], wu_ref[...])
            act = jnp.square(jnp.maximum(up, 0.0)).astype(BF16)
            part = _dot(act, wd_ref[...])
            ff = part if is_first else acc_ref[r, :] + part
            if is_last:
                o_ref[r, :] = x_ref[r, :] + ff * _rms_scale(ff) * gpost_ref[...]
            else:
                acc_ref[r, :] = ff

    pl.when(f == 0)(functools.partial(step, True, False))
    pl.when(jnp.logical_and(f > 0, f < last))(functools.partial(step, False, False))
    pl.when(f == last)(functools.partial(step, False, True))


def _ffn(x, gpre, w_up, w_down, layer, gpost, tm, tf):
    m, d = x.shape
    dff = w_up.shape[2]
    return pl.pallas_call(
        _ffn_kernel,
        grid=(m // tm, dff // tf),
        in_specs=[pl.BlockSpec((tm, d), lambda i, f: (i, 0)),
                  pl.BlockSpec((1, d), lambda i, f: (0, 0)),
                  pl.BlockSpec((None, d, tf), lambda i, f: (layer, 0, f)),
                  pl.BlockSpec((None, tf, d), lambda i, f: (layer, f, 0)),
                  pl.BlockSpec((1, d), lambda i, f: (0, 0))],
        out_specs=pl.BlockSpec((tm, d), lambda i, f: (i, 0)),
        out_shape=jax.ShapeDtypeStruct((m, d), F32),
        scratch_shapes=[pltpu.VMEM((tm, d), BF16), pltpu.VMEM((tm, d), F32)],
        compiler_params=_cparams(("parallel", "arbitrary")),
        name="ffn",
    )(x, gpre.reshape(1, d), w_up, w_down, gpost.reshape(1, d))


def _gate_expansion(w2_f, w2_b):
    w = jnp.zeros((w2_f.shape[0], LANE, 2 * B_QK), F32)
    w = w.at[:, 0:GATE_RANK, 0:B_QK].set(w2_f)
    w = w.at[:, GATE_RANK:2 * GATE_RANK, B_QK:].set(w2_b)
    return w.astype(BF16)


def kernel(x, mem, rel_bias, w_in, w_out, attn_sink, gla_w2_f, gla_b2_f, gla_w2_b, gla_b2_b, gla_norm, conv_w, conv_b, lru_wa, lru_ba, lru_wx, lru_bx, lru_lambda, xq, xk, xv, xo, w_up, w_down, norm_mix_pre, norm_mix_post, norm_mem, norm_x_pre, norm_x_post, norm_ff_pre, norm_ff_post):
    bsz, seq, d = x.shape
    assert d == D_MODEL and seq == SUBLANES * LRU_SEG
    xs = x.reshape(bsz * seq, d)
    mems = mem.reshape(bsz * mem.shape[1], d)
    bias_tab = _attention_bias_table(rel_bias)
    masks = _gla_masks()
    w2_b = _gate_expansion(gla_w2_f, gla_w2_b)
    b2 = jnp.concatenate([gla_b2_f, gla_b2_b], axis=-1)
    bf = lambda w: w.astype(BF16)
    wa_b, wx_b = bf(lru_wa), bf(lru_wx)
    w_up_b, w_down_b = bf(w_up), bf(w_down)
    for l in range(DEPTH):
        proj = _in_proj(xs, norm_mix_pre[l], w_in, l, tm=512)
        oa = _windowed_attention(proj, bias_tab, attn_sink[l], bsz, seq)
        ob = _gla(proj, w2_b, l, b2[l].reshape(1, 2 * B_QK), gla_norm[l].reshape(1, B_V), masks, bsz, seq)
        oc = _rglru(proj, conv_w[l], conv_b[l], wa_b, wx_b, l, lru_ba[l], lru_bx[l], lru_lambda[l],
                    bsz, seq)
        xs = _out_proj(oa, ob, oc, w_out, l, norm_mix_post[l], xs, tm=512)

        k, v = _kv_proj(mems, norm_mem[l], xk, xv, l, tm=512)
        xs = _cross_attention(xs, norm_x_pre[l], xq, k, v, xo, l, norm_x_post[l], bsz, seq, tq=512)

        xs = _ffn(xs, norm_ff_pre[l], w_up_b, w_down_b, l, norm_ff_post[l], tm=512, tf=1024)
    return xs.reshape(bsz, seq, d)
```

```python
import functools
import math

import numpy as np
import jax
import jax.numpy as jnp
from jax import lax
from jax.experimental import pallas as pl
from jax.experimental.pallas import tpu as pltpu

F32 = jnp.float32
BF16 = jnp.bfloat16

D_MODEL = 2048
DEPTH = 4
MEM_LEN = 256
A_HEAD_DIM = 128
A_HEADS = 8
A_KV_HEADS = 2
A_GROUP = 4
WINDOW = 128
BLOCK = 128
N_BUCKETS = 32
MAX_DISTANCE = 128
B_HEADS = 4
B_KEY_DIM = 64
B_VAL_DIM = 128
GATE_RANK = 16
GATE_TAU = 16.0
C_WIDTH = 512
C_BLOCKS = 4
C_BLOCK_DIM = 128
CONV_WIDTH = 4
LRU_C = 8.0
X_HEADS = 4
X_HEAD_DIM = 512
D_FF = 8192
EPS = 1e-6
NEG_INF = -1e30
LOG2E = math.log2(math.e)

A_Q = A_HEADS * A_HEAD_DIM
A_KV = A_KV_HEADS * A_HEAD_DIM
B_QK = B_HEADS * B_KEY_DIM
B_V = B_HEADS * B_VAL_DIM

LANE = 128
COL_AQ = 0
COL_AK = COL_AQ + A_Q
COL_AV = COL_AK + A_KV
COL_BQ = COL_AV + A_KV
COL_BK = COL_BQ + B_QK
COL_BV = COL_BK + B_QK
COL_BG = COL_BV + B_V
COL_CX = COL_BG + B_V
COL_CY = COL_CX + C_WIDTH
COL_Z = COL_CY + C_WIDTH
D_PROJ = COL_Z + LANE

GLA_BLOCK = 128
GLA_LEVELS = (1, 2, 4, 8, 16, 32, 64)
GLA_PAIRS = B_HEADS // 2
VMEM_LIMIT_MIB = 56


def _cparams(semantics, vmem_mib=VMEM_LIMIT_MIB):
    return pltpu.CompilerParams(dimension_semantics=semantics,
                                vmem_limit_bytes=vmem_mib * 1024 * 1024)


def _dot(a, b):
    return jnp.dot(a, b, preferred_element_type=F32)


def _dot_nt(a, b):
    return lax.dot_general(a, b, (((1,), (1,)), ((), ())), preferred_element_type=F32)


def _dot_tn(a, b):
    return lax.dot_general(a, b, (((0,), (0,)), ((), ())), preferred_element_type=F32)


def _softplus(x):
    return jnp.maximum(x, 0.0) + jnp.log(1.0 + jnp.exp(-jnp.abs(x)))


def _sigmoid(x):
    return 0.5 * jnp.tanh(0.5 * x) + 0.5


def _rms_scale(x):
    return lax.rsqrt(jnp.mean(jnp.square(x), axis=-1, keepdims=True) + EPS)


ROW_CHAINS = 2


def _row_groups(rows):
    step = rows // ROW_CHAINS
    return [slice(c * step, (c + 1) * step) for c in range(ROW_CHAINS)]


def _resident(shape, index_map):
    return pl.BlockSpec(shape, index_map, pipeline_mode=pl.Buffered(1))


WEIGHT_CHUNK_ROWS = 256


def _weight_scratch(k, n):
    return [pltpu.VMEM((k, n), BF16), pltpu.VMEM((2, WEIGHT_CHUNK_ROWS, n), F32),
            pltpu.SemaphoreType.DMA((2,))]


def _fetch_weight(w_hbm, layer, dst_ref, stage_ref, sem_ref):
    nchunk = dst_ref.shape[0] // WEIGHT_CHUNK_ROWS

    def copy(c):
        rows = pl.ds(c * WEIGHT_CHUNK_ROWS, WEIGHT_CHUNK_ROWS)
        return pltpu.make_async_copy(w_hbm.at[layer, rows, :], stage_ref.at[c % 2], sem_ref.at[c % 2])

    copy(0).start()
    for c in range(nchunk):
        if c + 1 < nchunk:
            copy(c + 1).start()
        copy(c).wait()
        dst_ref[c * WEIGHT_CHUNK_ROWS:(c + 1) * WEIGHT_CHUNK_ROWS, :] = stage_ref[c % 2].astype(BF16)


def _kv_proj_kernel(x_ref, g_ref, wk_hbm, wv_hbm, k_ref, v_ref, wk_ref, wv_ref, stage_ref, sem_ref, *, layer):
    @pl.when(pl.program_id(0) == 0)
    def _():
        _fetch_weight(wk_hbm, layer, wk_ref, stage_ref, sem_ref)
        _fetch_weight(wv_hbm, layer, wv_ref, stage_ref, sem_ref)

    x = x_ref[...]
    hn = (x * _rms_scale(x) * g_ref[...]).astype(BF16)
    k_ref[...] = _dot(hn, wk_ref[...]).astype(BF16)
    v_ref[...] = _dot(hn, wv_ref[...]).astype(BF16)


def _kv_proj(x, g, wk, wv, layer, tm):
    m, d = x.shape
    weight = pl.BlockSpec(memory_space=pl.ANY)
    rows = pl.BlockSpec((tm, d), lambda i: (i, 0))
    w_bf, stage, sem = _weight_scratch(d, d)
    return pl.pallas_call(
        functools.partial(_kv_proj_kernel, layer=layer),
        grid=(m // tm,),
        in_specs=[rows, pl.BlockSpec((1, d), lambda i: (0, 0)), weight, weight],
        out_specs=[rows, rows],
        out_shape=[jax.ShapeDtypeStruct((m, d), BF16)] * 2,
        scratch_shapes=[w_bf, w_bf, stage, sem],
        compiler_params=_cparams(("arbitrary",)),
        name="kv_proj",
    )(x, g.reshape(1, d), wk, wv)


IN_Z0 = A_Q + 2 * A_KV + 2 * B_QK + 2 * B_V
IN_Z1 = IN_Z0 + 2 * GATE_RANK
D_IN = IN_Z1 + 2 * C_WIDTH
IN_CHUNK_ROWS = 512


def _fetch_w_in(wt_hbm, layer, wm_ref, wc_ref, wz_ref, stage_ref, sem_ref):
    wz_ref[...] = jnp.zeros(wz_ref.shape, BF16)
    pieces = []
    for start, stop, dst in ((0, IN_Z0, wm_ref), (IN_Z0, IN_Z1, wz_ref), (IN_Z1, D_IN, wc_ref)):
        for r in range(start, stop, IN_CHUNK_ROWS):
            pieces.append((r, min(IN_CHUNK_ROWS, stop - r), dst, r - start))

    def copy(c):
        src_row, rows, _, _ = pieces[c]
        return pltpu.make_async_copy(wt_hbm.at[layer, pl.ds(src_row, rows), :],
                                     stage_ref.at[c % 2, pl.ds(0, rows), :], sem_ref.at[c % 2])

    copy(0).start()
    for c, (_, rows, dst, dst_row) in enumerate(pieces):
        if c + 1 < len(pieces):
            copy(c + 1).start()
        copy(c).wait()
        dst[dst_row:dst_row + rows, :] = stage_ref[c % 2, 0:rows, :].astype(BF16)


def _in_proj_kernel(x_ref, g_ref, wt_hbm, o_ref, wm_ref, wc_ref, wz_ref, stage_ref, sem_ref, *, layer):
    @pl.when(pl.program_id(0) == 0)
    def _():
        _fetch_w_in(wt_hbm, layer, wm_ref, wc_ref, wz_ref, stage_ref, sem_ref)

    for r in _row_groups(x_ref.shape[0]):
        x = x_ref[r, :]
        hn = (x * _rms_scale(x) * g_ref[...]).astype(BF16)
        o_ref[r, 0:COL_CX] = _dot_nt(hn, wm_ref[...])
        o_ref[r, COL_CX:COL_Z] = _dot_nt(hn, wc_ref[...])
        o_ref[r, COL_Z:D_PROJ] = _dot_nt(hn, wz_ref[...])


def _in_proj(x, g, w_in, layer, tm):
    m, k = x.shape
    assert w_in.shape[1:] == (k, D_IN)
    wt = jnp.swapaxes(w_in, 1, 2)
    return pl.pallas_call(
        functools.partial(_in_proj_kernel, layer=layer),
        grid=(m // tm,),
        in_specs=[pl.BlockSpec((tm, k), lambda i: (i, 0)), pl.BlockSpec((1, k), lambda i: (0, 0)),
                  pl.BlockSpec(memory_space=pl.ANY)],
        out_specs=pl.BlockSpec((tm, D_PROJ), lambda i: (i, 0)),
        out_shape=jax.ShapeDtypeStruct((m, D_PROJ), F32),
        scratch_shapes=[pltpu.VMEM((IN_Z0, k), BF16), pltpu.VMEM((2 * C_WIDTH, k), BF16),
                        pltpu.VMEM((LANE, k), BF16), pltpu.VMEM((2, IN_CHUNK_ROWS, k), F32),
                        pltpu.SemaphoreType.DMA((2,))],
        compiler_params=_cparams(("arbitrary",)),
        name="in_proj",
    )(x, g.reshape(1, k), wt)


def _attn_kernel(sink_ref, q_ref, k_ref, v_ref, bias_ref, o_ref, kb_ref, vt_ref, p_ref, sw_ref):
    seq = q_ref.shape[0]
    nblk = seq // BLOCK
    kb_ref[...] = k_ref[...].astype(BF16)
    for g in range(A_KV_HEADS):
        for n in range(nblk):
            vt_ref[g * nblk + n] = (
                v_ref[n * BLOCK:(n + 1) * BLOCK, g * A_HEAD_DIM:(g + 1) * A_HEAD_DIM].T.astype(BF16))
    sinks = [jnp.concatenate([jnp.full((1, BLOCK), sink_ref[g * A_GROUP + h] * LOG2E, F32)
                              for h in range(A_GROUP)], axis=1) for g in range(A_KV_HEADS)]

    def tiles(first, last):
        return (1 if first else 0), (2 if last else 3)

    def scores(g, n, first, last):
        t0, t1 = tiles(first, last)
        nk = (t1 - t0) * BLOCK
        r0 = pl.multiple_of(n * BLOCK, BLOCK)
        k = kb_ref[pl.ds(pl.multiple_of(r0 + (t0 - 1) * BLOCK, BLOCK), nk),
                   g * A_HEAD_DIM:(g + 1) * A_HEAD_DIM]
        q = jnp.concatenate(
            [q_ref[pl.ds(r0, BLOCK), (g * A_GROUP + h) * A_HEAD_DIM:(g * A_GROUP + h + 1) * A_HEAD_DIM]
             for h in range(A_GROUP)], axis=0).astype(BF16)
        s = _dot_nt(k, q) * (A_HEAD_DIM ** -0.5 * LOG2E) + bias_ref[g, t0 * BLOCK:t1 * BLOCK, :]
        m = jnp.maximum(jnp.max(s, axis=0, keepdims=True), sinks[g])
        p_ref[g, 0:nk, :] = jnp.exp2(s - m).astype(BF16)
        sw_ref[g] = jnp.exp2(sinks[g] - m)

    def output(g, n, first, last):
        t0, t1 = tiles(first, last)
        nk = (t1 - t0) * BLOCK
        r0 = pl.multiple_of(n * BLOCK, BLOCK)
        v_t = jnp.concatenate([vt_ref[g * nblk + n + t - 1] for t in range(t0, t1)], axis=1)
        ones = jnp.ones((2 * SUBLANES, nk), BF16)
        o = _dot(jnp.concatenate([v_t, ones], axis=0), p_ref[g, 0:nk, :])
        denom = o[A_HEAD_DIM:A_HEAD_DIM + 1, :] + sw_ref[g]
        o = o[0:A_HEAD_DIM, :] / denom
        for h in range(A_GROUP):
            col = (g * A_GROUP + h) * A_HEAD_DIM
            o_ref[pl.ds(r0, BLOCK), col:col + A_HEAD_DIM] = (
                o[:, h * BLOCK:(h + 1) * BLOCK].T.astype(o_ref.dtype))

    def both(fn, *args):
        for g in range(A_KV_HEADS):
            fn(g, *args)

    def body(n, carry):
        both(output, n, False, False)
        both(scores, n + 1, False, False)
        return carry

    both(scores, 0, True, False)
    both(output, 0, True, False)
    both(scores, 1, False, False)
    lax.fori_loop(1, nblk - 2, body, 0)
    both(output, nblk - 2, False, False)
    both(scores, nblk - 1, False, True)
    both(output, nblk - 1, False, True)


def _windowed_attention(proj, bias_tab, sink, bsz, seq):
    qw = A_GROUP * A_HEAD_DIM
    nblk = seq // BLOCK
    return pl.pallas_call(
        _attn_kernel,
        grid=(bsz,),
        in_specs=[pl.BlockSpec(memory_space=pltpu.SMEM),
                  pl.BlockSpec((seq, A_Q), lambda b: (b, COL_AQ // A_Q)),
                  pl.BlockSpec((seq, A_KV), lambda b: (b, COL_AK // A_KV)),
                  pl.BlockSpec((seq, A_KV), lambda b: (b, COL_AV // A_KV)),
                  pl.BlockSpec(bias_tab.shape, lambda b: (0, 0, 0))],
        out_specs=pl.BlockSpec((seq, A_Q), lambda b: (b, 0)),
        out_shape=jax.ShapeDtypeStruct((bsz * seq, A_Q), BF16),
        scratch_shapes=[pltpu.VMEM((seq, A_KV), BF16),
                        pltpu.VMEM((A_KV_HEADS * nblk, A_HEAD_DIM, BLOCK), BF16),
                        pltpu.VMEM((A_KV_HEADS, 3 * BLOCK, qw), BF16),
                        pltpu.VMEM((A_KV_HEADS, 1, qw), F32)],
        compiler_params=_cparams(("parallel",)),
        name="windowed_attention",
    )(sink, proj, proj, proj, bias_tab)


def _t5_bucket(rel):
    nb = N_BUCKETS // 2
    max_exact = nb // 2
    ret = jnp.where(rel > 0, nb, 0)
    n = jnp.abs(rel)
    nf = jnp.maximum(n, 1).astype(F32)
    large = max_exact + (jnp.log(nf / max_exact) / math.log(MAX_DISTANCE / max_exact)
                         * (nb - max_exact)).astype(jnp.int32)
    large = jnp.minimum(large, nb - 1)
    return ret + jnp.where(n < max_exact, n, large)


def _attention_bias_table(rel_bias):
    kj = jnp.arange(3 * BLOCK)[:, None]
    qi = jnp.arange(BLOCK)[None, :]
    rel = kj - BLOCK - qi
    onehot = jax.nn.one_hot(_t5_bucket(rel), N_BUCKETS, dtype=F32)
    bias = jnp.einsum("kqn,nh->hkq", onehot, rel_bias.astype(F32), precision=lax.Precision.HIGHEST)
    bias = jnp.where((jnp.abs(rel) <= WINDOW)[None], bias * LOG2E, NEG_INF)
    bias = bias.reshape(A_KV_HEADS, A_GROUP, 3 * BLOCK, BLOCK)
    return jnp.transpose(bias, (0, 2, 1, 3)).reshape(A_KV_HEADS, 3 * BLOCK, A_GROUP * BLOCK)


def _gla_masks():
    c = GLA_BLOCK
    i = np.arange(c)[:, None]
    j = np.arange(c)[None, :]
    fwd = [i == j]
    for s in GLA_LEVELS:
        fwd.append((i // (2 * s) == j // (2 * s)) & ((i // s) % 2 == 1) & ((j // s) % 2 == 0))
    fwd = np.stack(fwd)
    return jnp.asarray(np.stack([fwd, fwd.transpose(0, 2, 1)]).astype(np.float32))


def _gla_block(direction, rows, q_ref, k_ref, v_ref, z_ref, w2_ref, b2_ref, m_ref, st_ref):
    c = GLA_BLOCK
    q = q_ref[rows, :] * (B_KEY_DIM ** -0.5)
    k = k_ref[rows, :]
    v = v_ref[rows, :].astype(BF16)
    gate_cols = slice(direction * B_QK, (direction + 1) * B_QK)
    pre = _dot(z_ref[rows, :].astype(BF16), w2_ref[:, gate_cols]) + b2_ref[:, gate_cols]
    log_a = -_softplus(-pre) * (LOG2E / GATE_TAU)

    row = lax.broadcasted_iota(jnp.int32, (c, B_QK), 0)
    first_head = lax.broadcasted_iota(jnp.int32, (c, LANE), 1) < B_KEY_DIM
    pair = [slice(p * LANE, (p + 1) * LANE) for p in range(GLA_PAIRS)]
    q_b = [q[:, ps].astype(BF16) for ps in pair]
    k_b = [k[:, ps].astype(BF16) for ps in pair]
    k_first = [jnp.where(first_head, kb, jnp.zeros_like(kb)) for kb in k_b]
    k_second = [jnp.where(first_head, jnp.zeros_like(kb), kb) for kb in k_b]

    def exponents(p_s, t_s):
        if direction == 0:
            return p_s, t_s - p_s
        return t_s - p_s + log_a, p_s - log_a

    def scaled(e_q, e_k, p):
        w_q = jnp.exp2(e_q[:, pair[p]]).astype(BF16)
        w_k = jnp.exp2(e_k[:, pair[p]]).astype(BF16)
        return q_b[p] * w_q, jnp.concatenate([k_first[p] * w_k, k_second[p] * w_k], axis=0)

    attn = [None] * GLA_PAIRS

    def add_level(level, q_t, k_t, p):
        mask = m_ref[direction, level]
        contrib = jnp.concatenate([mask, mask], axis=1) * _dot_nt(q_t, k_t)
        attn[p] = contrib if attn[p] is None else attn[p] + contrib

    if direction == 0:
        for p in range(GLA_PAIRS):
            add_level(0, q_b[p], jnp.concatenate([k_first[p], k_second[p]], axis=0), p)
    p_s = log_a
    t_s = log_a
    for li, s in enumerate(GLA_LEVELS):
        e_q, e_k = exponents(p_s, t_s)
        for p in range(GLA_PAIRS):
            q_t, k_t = scaled(e_q, e_k, p)
            add_level(li + 1, q_t, k_t, p)
        upper = (row & s) != 0
        t_before = pltpu.roll(t_s, s, 0)
        t_after = pltpu.roll(t_s, c - s, 0)
        p_s = p_s + jnp.where(upper, t_before, 0.0)
        t_s = t_s + jnp.where(upper, t_before, t_after)

    e_q, e_k = exponents(p_s, t_s)
    state_decay = jnp.exp2(t_s[0:1, :])
    own_block = lax.broadcasted_iota(jnp.int32, (2 * B_VAL_DIM, LANE), 0) < B_VAL_DIM
    own_block = own_block == (lax.broadcasted_iota(jnp.int32, (2 * B_VAL_DIM, LANE), 1) < B_KEY_DIM)
    zero_v = jnp.zeros((c, B_VAL_DIM), BF16)
    outs = []
    for p in range(GLA_PAIRS):
        v_pair = v[:, 2 * p * B_VAL_DIM:(2 * p + 2) * B_VAL_DIM]
        v_diag = jnp.concatenate(
            [jnp.concatenate([v_pair[:, :B_VAL_DIM], zero_v], axis=1),
             jnp.concatenate([zero_v, v_pair[:, B_VAL_DIM:]], axis=1)], axis=0)
        state_t = st_ref[direction, p]
        q_t = q_b[p] * jnp.exp2(e_q[:, pair[p]]).astype(BF16)
        k_t = k_b[p] * jnp.exp2(e_k[:, pair[p]]).astype(BF16)
        o = _dot_nt(q_t, state_t.astype(BF16)) + _dot(attn[p].astype(BF16), v_diag)
        update = jnp.where(own_block, _dot_tn(v_pair, k_t), 0.0)
        st_ref[direction, p] = state_t * state_decay[:, pair[p]] + update
        outs.append(o)
    return jnp.concatenate(outs, axis=1)


def _gla_kernel(q_ref, k_ref, v_ref, g_ref, z_ref, w2_ref, b2_ref, gn_ref, m_ref,
                o_ref, of_ref, ob_ref, st_ref):
    nblk = q_ref.shape[0] // GLA_BLOCK
    st_ref[...] = jnp.zeros(st_ref.shape, F32)
    block = functools.partial(_gla_block, q_ref=q_ref, k_ref=k_ref, v_ref=v_ref, z_ref=z_ref,
                              w2_ref=w2_ref, b2_ref=b2_ref, m_ref=m_ref, st_ref=st_ref)

    def rows_of(n):
        return pl.ds(pl.multiple_of(n * GLA_BLOCK, GLA_BLOCK), GLA_BLOCK)

    def finish(rows, o):
        normed = []
        for h in range(B_HEADS):
            oh = o[:, h * B_VAL_DIM:(h + 1) * B_VAL_DIM]
            normed.append(oh * _rms_scale(oh))
        g = g_ref[rows, :]
        out = jnp.concatenate(normed, axis=1) * gn_ref[...] * (g * jax.nn.sigmoid(g))
        o_ref[rows, :] = out.astype(o_ref.dtype)

    def first_half(n, carry):
        rows_f, rows_b = rows_of(n), rows_of(nblk - 1 - n)
        of_ref[rows_f, :] = block(0, rows_f)
        ob_ref[rows_b, :] = block(1, rows_b)
        return carry

    def second_half(n, carry):
        rows_f, rows_b = rows_of(n), rows_of(nblk - 1 - n)
        finish(rows_f, block(0, rows_f) + ob_ref[rows_f, :])
        finish(rows_b, of_ref[rows_b, :] + block(1, rows_b))
        return carry

    lax.fori_loop(0, nblk // 2, first_half, 0)
    lax.fori_loop(nblk // 2, nblk, second_half, 0)


def _gla(proj, w2, layer, b2, gn, masks, bsz, seq):
    def col(width, start):
        return pl.BlockSpec((seq, width), lambda b: (b, start // width))

    def whole(a):
        return pl.BlockSpec(a.shape, lambda b: (0,) * a.ndim)

    return pl.pallas_call(
        _gla_kernel,
        grid=(bsz,),
        in_specs=[col(B_QK, COL_BQ), col(B_QK, COL_BK), col(B_V, COL_BV), col(B_V, COL_BG),
                  col(LANE, COL_Z), pl.BlockSpec((None,) + w2.shape[1:], lambda b: (layer, 0, 0)),
                  whole(b2), whole(gn), whole(masks)],
        out_specs=pl.BlockSpec((seq, B_V), lambda b: (b, 0)),
        out_shape=jax.ShapeDtypeStruct((bsz * seq, B_V), BF16),
        scratch_shapes=[pltpu.VMEM((seq, B_V), F32), pltpu.VMEM((seq, B_V), F32),
                        pltpu.VMEM((2, GLA_PAIRS, 2 * B_VAL_DIM, LANE), F32)],
        compiler_params=_cparams(("parallel",)),
        name="gla",
    )(proj, proj, proj, proj, proj, w2, b2, gn, masks)


LRU_SEG = 256
SUBLANES = 8
LRU_PITCH = LRU_SEG + SUBLANES


def _lru_kernel(x_ref, y_ref, cw_ref, cb_ref, wa_ref, wx_ref, ba_ref, bx_ref, lam_ref, o_ref,
                xp_ref, af_ref, hf_ref, ab_ref, hb_ref):
    seq = x_ref.shape[0]
    ch = x_ref.shape[1]
    nseg = seq // LRU_SEG
    halo = SUBLANES
    xp_ref[0:halo, :] = jnp.zeros((halo, ch), F32)
    xp_ref[seq + halo:seq + 2 * halo, :] = jnp.zeros((halo, ch), F32)
    xp_ref[halo:seq + halo, :] = x_ref[...]
    decay_rate = _softplus(-lam_ref[...])
    scan_refs = ((af_ref, hf_ref), (ab_ref, hb_ref))

    def gates(c, carry):
        r0 = pl.multiple_of(c * LRU_SEG, LRU_SEG)
        dst = pl.ds(pl.multiple_of(c * LRU_PITCH, SUBLANES), LRU_SEG)
        ext = LRU_SEG + 2 * halo
        xe = xp_ref[pl.ds(r0, ext), :]
        xc = (cw_ref[0:1, :] * pltpu.roll(xe, 2, 0) + cw_ref[1:2, :] * pltpu.roll(xe, 1, 0)
              + cw_ref[2:3, :] * xe + cw_ref[3:4, :] * pltpu.roll(xe, ext - 1, 0))
        xc = xc[halo:halo + LRU_SEG] + cb_ref[...]
        xb = xc.astype(BF16)
        for d in range(2):
            r = _sigmoid(_dot(xb, wa_ref[d, 0]) + ba_ref[d:d + 1, :])
            i = _sigmoid(_dot(xb, wx_ref[d, 0]) + bx_ref[d:d + 1, :])
            log_a = -LRU_C * r * decay_rate[d:d + 1, :]
            t = jnp.tanh(log_a)
            a_ref, u_ref = scan_refs[d]
            a_ref[dst, :] = jnp.exp(log_a)
            u_ref[dst, :] = jnp.sqrt(-2.0 * t / (1.0 - t)) * (i * xc)
        return carry

    lax.fori_loop(0, nseg, gates, 0)

    def scan_step(j, carry):
        hf, pf, hb, pb = carry
        rf = pl.ds(j, nseg, stride=LRU_PITCH)
        a = af_ref[rf, :]
        hf = a * hf + hf_ref[rf, :]
        pf = a * pf
        hf_ref[rf, :] = hf
        af_ref[rf, :] = pf
        rb = pl.ds(LRU_SEG - 1 - j, nseg, stride=LRU_PITCH)
        a = ab_ref[rb, :]
        hb = a * hb + hb_ref[rb, :]
        pb = a * pb
        hb_ref[rb, :] = hb
        ab_ref[rb, :] = pb
        return hf, pf, hb, pb

    zeros = jnp.zeros((nseg, ch), F32)
    ones = jnp.ones((nseg, ch), F32)
    lax.fori_loop(0, LRU_SEG, scan_step, (zeros, ones, zeros, ones), unroll=4)

    enter_f = [None] * nseg
    state = jnp.zeros((1, ch), F32)
    for s in range(nseg):
        enter_f[s] = state
        last = s * LRU_PITCH + LRU_SEG - 1
        state = hf_ref[last:last + 1, :] + af_ref[last:last + 1, :] * state
    enter_b = [None] * nseg
    state = jnp.zeros((1, ch), F32)
    for s in reversed(range(nseg)):
        enter_b[s] = state
        first = s * LRU_PITCH
        state = hb_ref[first:first + 1, :] + ab_ref[first:first + 1, :] * state

    for s in range(nseg):
        seg = slice(s * LRU_PITCH, s * LRU_PITCH + LRU_SEG)
        rows = slice(s * LRU_SEG, (s + 1) * LRU_SEG)
        h = (hf_ref[seg, :] + af_ref[seg, :] * enter_f[s]) + (hb_ref[seg, :] + ab_ref[seg, :] * enter_b[s])
        o_ref[rows, :] = (h * jax.nn.gelu(y_ref[rows, :])).astype(o_ref.dtype)


def _rglru(proj, conv_w, conv_b, w_a, w_x, layer, b_a, b_x, lam, bsz, seq):
    ch = C_BLOCK_DIM

    def col(start):
        return pl.BlockSpec((seq, ch), lambda b, c: (b, start // ch + c))

    def per_block(rows):
        return pl.BlockSpec((rows, ch), lambda b, c: (0, c))

    gate_w = pl.BlockSpec((None, 2, 1, ch, ch), lambda b, c: (layer, 0, c, 0, 0))
    return pl.pallas_call(
        _lru_kernel,
        grid=(bsz, C_BLOCKS),
        in_specs=[col(COL_CX), col(COL_CY), per_block(CONV_WIDTH), per_block(1), gate_w, gate_w,
                  per_block(2), per_block(2), per_block(2)],
        out_specs=pl.BlockSpec((seq, ch), lambda b, c: (b, c)),
        out_shape=jax.ShapeDtypeStruct((bsz * seq, C_WIDTH), BF16),
        scratch_shapes=([pltpu.VMEM((seq + 2 * SUBLANES, ch), F32)]
                        + [pltpu.VMEM((seq // LRU_SEG * LRU_PITCH, ch), F32)] * 4),
        compiler_params=_cparams(("parallel", "parallel")),
        name="rglru",
    )(proj, proj, conv_w, conv_b.reshape(1, C_WIDTH), w_a, w_x, b_a, b_x, lam)


def _out_proj_kernel(oa_ref, ob_ref, oc_ref, w_hbm, g_ref, x_ref, o_ref, w_ref, stage_ref, sem_ref, *, layer):
    @pl.when(pl.program_id(0) == 0)
    def _():
        _fetch_weight(w_hbm, layer, w_ref, stage_ref, sem_ref)

    for r in _row_groups(x_ref.shape[0]):
        mixed = (_dot(oa_ref[r, :], w_ref[0:A_Q, :]) + _dot(ob_ref[r, :], w_ref[A_Q:A_Q + B_V, :])
                 + _dot(oc_ref[r, :], w_ref[A_Q + B_V:A_Q + B_V + C_WIDTH, :]))
        o_ref[r, :] = x_ref[r, :] + mixed * _rms_scale(mixed) * g_ref[...]


def _out_proj(oa, ob, oc, w, layer, g, x, tm):
    m, d = x.shape

    def rows(width):
        return pl.BlockSpec((tm, width), lambda i: (i, 0))

    return pl.pallas_call(
        functools.partial(_out_proj_kernel, layer=layer),
        grid=(m // tm,),
        in_specs=[rows(A_Q), rows(B_V), rows(C_WIDTH),
                  pl.BlockSpec(memory_space=pl.ANY),
                  pl.BlockSpec((1, d), lambda i: (0, 0)),
                  rows(d)],
        out_specs=rows(d),
        out_shape=jax.ShapeDtypeStruct((m, d), F32),
        scratch_shapes=_weight_scratch(w.shape[1], w.shape[2]),
        compiler_params=_cparams(("arbitrary",)),
        name="out_proj",
    )(oa, ob, oc, w, g.reshape(1, d), x)


def _cross_attn_kernel(x_ref, gpre_ref, wq_hbm, k_ref, v_ref, wo_hbm, gpost_ref, o_ref,
                       wq_ref, wo_ref, stage_ref, sem_ref, *, layer):
    @pl.when(jnp.logical_and(pl.program_id(0) == 0, pl.program_id(1) == 0))
    def _():
        _fetch_weight(wq_hbm, layer, wq_ref, stage_ref, sem_ref)
        _fetch_weight(wo_hbm, layer, wo_ref, stage_ref, sem_ref)

    x = x_ref[...]
    hn = (x * _rms_scale(x) * gpre_ref[...]).astype(BF16)
    q = _dot(hn, wq_ref[...]).astype(BF16)
    acc = None
    for h in range(X_HEADS):
        hs = slice(h * X_HEAD_DIM, (h + 1) * X_HEAD_DIM)
        s = _dot_nt(q[:, hs], k_ref[:, hs]) * (X_HEAD_DIM ** -0.5)
        m = jnp.max(s, axis=-1, keepdims=True)
        p = jnp.exp(s - m)
        o = _dot(p.astype(BF16), v_ref[:, hs]) / jnp.sum(p, axis=-1, keepdims=True)
        part = _dot(o.astype(BF16), wo_ref[hs, :])
        acc = part if acc is None else acc + part
    o_ref[...] = x + acc * _rms_scale(acc) * gpost_ref[...]


def _cross_attention(x, gpre, wq, k, v, wo, layer, gpost, bsz, seq, tq):
    d = D_MODEL
    per_row = seq // tq
    mem_spec = pl.BlockSpec((MEM_LEN, d), lambda b, i: (b, 0))
    gain_spec = pl.BlockSpec((1, d), lambda b, i: (0, 0))
    weight_spec = pl.BlockSpec(memory_space=pl.ANY)
    w_bf, stage, sem = _weight_scratch(d, d)
    row_spec = pl.BlockSpec((tq, d), lambda b, i: (b * per_row + i, 0))
    return pl.pallas_call(
        functools.partial(_cross_attn_kernel, layer=layer),
        grid=(bsz, per_row),
        in_specs=[row_spec, gain_spec, weight_spec, mem_spec, mem_spec, weight_spec, gain_spec],
        out_specs=row_spec,
        out_shape=jax.ShapeDtypeStruct((bsz * seq, d), F32),
        scratch_shapes=[w_bf, w_bf, stage, sem],
        compiler_params=_cparams(("arbitrary", "arbitrary")),
        name="cross_attention",
    )(x, gpre.reshape(1, d), wq, k, v, wo, gpost.reshape(1, d))


def _ffn_kernel(x_ref, gpre_ref, wu_ref, wd_ref, gpost_ref, o_ref, hn_ref, acc_ref):
    f = pl.program_id(1)
    last = pl.num_programs(1) - 1

    def step(is_first, is_last):
        for r in _row_groups(x_ref.shape[0]):
            if is_first:
                x = x_ref[r, :]
                hn_ref[r, :] = (x * _rms_scale(x) * gpre_ref[...]).astype(BF16)
            up = _dot(hn_ref[r, :], wu_ref[...])
            act = jnp.square(jnp.maximum(up, 0.0)).astype(BF16)
            part = _dot(act, wd_ref[...])
            ff = part if is_first else acc_ref[r, :] + part
            if is_last:
                o_ref[r, :] = x_ref[r, :] + ff * _rms_scale(ff) * gpost_ref[...]
            else:
                acc_ref[r, :] = ff

    pl.when(f == 0)(functools.partial(step, True, False))
    pl.when(jnp.logical_and(f > 0, f < last))(functools.partial(step, False, False))
    pl.when(f == last)(functools.partial(step, False, True))


def _ffn_rest(x, gpre, w_up, w_down, gpost, tm, tf):
    m, d = x.shape
    dff = w_up.shape[1]
    return pl.pallas_call(
        _ffn_kernel,
        grid=(m // tm - 1, dff // tf),
        in_specs=[pl.BlockSpec((tm, d), lambda i, f: (i + 1, 0)),
                  pl.BlockSpec((1, d), lambda i, f: (0, 0)),
                  pl.BlockSpec((d, tf), lambda i, f: (0, f)),
                  pl.BlockSpec((tf, d), lambda i, f: (f, 0)),
                  pl.BlockSpec((1, d), lambda i, f: (0, 0))],
        out_specs=pl.BlockSpec((tm, d), lambda i, f: (i + 1, 0)),
        out_shape=jax.ShapeDtypeStruct((m, d), F32),
        input_output_aliases={0: 0},
        scratch_shapes=[pltpu.VMEM((tm, d), BF16), pltpu.VMEM((tm, d), F32)],
        compiler_params=_cparams(("parallel", "arbitrary")),
        name="ffn",
    )(x, gpre.reshape(1, d), w_up, w_down, gpost.reshape(1, d))


def _ffn_first_kernel(x_ref, gpre_ref, wu_ref, wd_ref, gpost_ref, o_ref, wub_ref, wdb_ref, hn_ref, acc_ref):
    f = pl.program_id(0)
    wu = wu_ref[...].astype(BF16)
    wd = wd_ref[...].astype(BF16)
    wub_ref[...] = wu
    wdb_ref[...] = wd

    @pl.when(f == 0)
    def _():
        x = x_ref[...]
        hn_ref[...] = (x * _rms_scale(x) * gpre_ref[...]).astype(BF16)
        acc_ref[...] = jnp.zeros(acc_ref.shape, F32)

    up = _dot(hn_ref[...], wu)
    act = jnp.square(jnp.maximum(up, 0.0)).astype(BF16)
    acc_ref[...] += _dot(act, wd)

    @pl.when(f == pl.num_programs(0) - 1)
    def _():
        ff = acc_ref[...]
        o_ref[...] = x_ref[...] + ff * _rms_scale(ff) * gpost_ref[...]


def _ffn_first(x, gpre, w_up, w_down, layer, gpost, tm, tf):
    m, d = x.shape
    dff = w_up.shape[2]
    return pl.pallas_call(
        _ffn_first_kernel,
        grid=(dff // tf,),
        in_specs=[pl.BlockSpec((tm, d), lambda f: (0, 0)),
                  pl.BlockSpec((1, d), lambda f: (0, 0)),
                  pl.BlockSpec((None, d, tf), lambda f: (layer, 0, f)),
                  pl.BlockSpec((None, tf, d), lambda f: (layer, f, 0)),
                  pl.BlockSpec((1, d), lambda f: (0, 0))],
        out_specs=[pl.BlockSpec((tm, d), lambda f: (0, 0)),
                   pl.BlockSpec((d, tf), lambda f: (0, f)),
                   pl.BlockSpec((tf, d), lambda f: (f, 0))],
        out_shape=[jax.ShapeDtypeStruct((tm, d), F32), jax.ShapeDtypeStruct((d, dff), BF16),
                   jax.ShapeDtypeStruct((dff, d), BF16)],
        scratch_shapes=[pltpu.VMEM((tm, d), BF16), pltpu.VMEM((tm, d), F32)],
        compiler_params=_cparams(("arbitrary",)),
        name="ffn_first",
    )(x, gpre.reshape(1, d), w_up, w_down, gpost.reshape(1, d))


def _ffn(x, gpre, w_up, w_down, layer, gpost, tm):
    first, w_up_b, w_down_b = _ffn_first(x, gpre, w_up, w_down, layer, gpost, tm, tf=512)
    rest = _ffn_rest(x, gpre, w_up_b, w_down_b, gpost, tm, tf=1024)
    return lax.dynamic_update_slice(rest, first, (0, 0))


def _gate_expansion(w2_f, w2_b):
    w = jnp.zeros((w2_f.shape[0], LANE, 2 * B_QK), F32)
    w = w.at[:, 0:GATE_RANK, 0:B_QK].set(w2_f)
    w = w.at[:, GATE_RANK:2 * GATE_RANK, B_QK:].set(w2_b)
    return w.astype(BF16)


def kernel(x, mem, rel_bias, w_in, w_out, attn_sink, gla_w2_f, gla_b2_f, gla_w2_b, gla_b2_b, gla_norm, conv_w, conv_b, lru_wa, lru_ba, lru_wx, lru_bx, lru_lambda, xq, xk, xv, xo, w_up, w_down, norm_mix_pre, norm_mix_post, norm_mem, norm_x_pre, norm_x_post, norm_ff_pre, norm_ff_post):
    bsz, seq, d = x.shape
    assert d == D_MODEL and seq == SUBLANES * LRU_SEG
    xs = x.reshape(bsz * seq, d)
    mems = mem.reshape(bsz * mem.shape[1], d)
    bias_tab = _attention_bias_table(rel_bias)
    masks = _gla_masks()
    w2_b = _gate_expansion(gla_w2_f, gla_w2_b)
    b2 = jnp.concatenate([gla_b2_f, gla_b2_b], axis=-1)
    bf = lambda w: w.astype(BF16)
    wa_b, wx_b = bf(lru_wa), bf(lru_wx)
    for l in range(DEPTH):
        proj = _in_proj(xs, norm_mix_pre[l], w_in, l, tm=512)
        oa = _windowed_attention(proj, bias_tab, attn_sink[l], bsz, seq)
        ob = _gla(proj, w2_b, l, b2[l].reshape(1, 2 * B_QK), gla_norm[l].reshape(1, B_V), masks, bsz, seq)
        oc = _rglru(proj, conv_w[l], conv_b[l], wa_b, wx_b, l, lru_ba[l], lru_bx[l], lru_lambda[l],
                    bsz, seq)
        xs = _out_proj(oa, ob, oc, w_out, l, norm_mix_post[l], xs, tm=512)

        k, v = _kv_proj(mems, norm_mem[l], xk, xv, l, tm=512)
        xs = _cross_attention(xs, norm_x_pre[l], xq, k, v, xo, l, norm_x_post[l], bsz, seq, tq=512)

        xs = _ffn(xs, norm_ff_pre[l], w_up, w_down, l, norm_ff_post[l], tm=512)
    return xs.reshape(bsz, seq, d)
```

```python
import functools
import math

import numpy as np
import jax
import jax.numpy as jnp
from jax import lax
from jax.experimental import pallas as pl
from jax.experimental.pallas import tpu as pltpu

F32 = jnp.float32
BF16 = jnp.bfloat16

D_MODEL = 2048
DEPTH = 4
MEM_LEN = 256
A_HEAD_DIM = 128
A_HEADS = 8
A_KV_HEADS = 2
A_GROUP = 4
WINDOW = 128
BLOCK = 128
N_BUCKETS = 32
MAX_DISTANCE = 128
B_HEADS = 4
B_KEY_DIM = 64
B_VAL_DIM = 128
GATE_RANK = 16
GATE_TAU = 16.0
C_WIDTH = 512
C_BLOCKS = 4
C_BLOCK_DIM = 128
CONV_WIDTH = 4
LRU_C = 8.0
X_HEADS = 4
X_HEAD_DIM = 512
D_FF = 8192
EPS = 1e-6
NEG_INF = -1e30
LOG2E = math.log2(math.e)

A_Q = A_HEADS * A_HEAD_DIM
A_KV = A_KV_HEADS * A_HEAD_DIM
B_QK = B_HEADS * B_KEY_DIM
B_V = B_HEADS * B_VAL_DIM

LANE = 128
COL_AQ = 0
COL_AK = COL_AQ + A_Q
COL_AV = COL_AK + A_KV
COL_BQ = COL_AV + A_KV
COL_BK = COL_BQ + B_QK
COL_BV = COL_BK + B_QK
COL_BG = COL_BV + B_V
COL_CX = COL_BG + B_V
COL_CY = COL_CX + C_WIDTH
COL_Z = COL_CY + C_WIDTH
D_PROJ = COL_Z + LANE

GLA_BLOCK = 128
GLA_LEVELS = (1, 2, 4, 8, 16, 32, 64)
GLA_PAIRS = B_HEADS // 2
VMEM_LIMIT_MIB = 56


def _cparams(semantics, vmem_mib=VMEM_LIMIT_MIB):
    return pltpu.CompilerParams(dimension_semantics=semantics,
                                vmem_limit_bytes=vmem_mib * 1024 * 1024)


def _dot(a, b):
    return jnp.dot(a, b, preferred_element_type=F32)


def _dot_nt(a, b):
    return lax.dot_general(a, b, (((1,), (1,)), ((), ())), preferred_element_type=F32)


def _dot_tn(a, b):
    return lax.dot_general(a, b, (((0,), (0,)), ((), ())), preferred_element_type=F32)


def _softplus(x):
    return jnp.maximum(x, 0.0) + jnp.log(1.0 + jnp.exp(-jnp.abs(x)))


def _sigmoid(x):
    return 0.5 * jnp.tanh(0.5 * x) + 0.5


def _rms_scale(x):
    return lax.rsqrt(jnp.mean(jnp.square(x), axis=-1, keepdims=True) + EPS)


ROW_CHAINS = 2


def _row_groups(rows):
    step = rows // ROW_CHAINS
    return [slice(c * step, (c + 1) * step) for c in range(ROW_CHAINS)]


def _resident(shape, index_map):
    return pl.BlockSpec(shape, index_map, pipeline_mode=pl.Buffered(1))


WEIGHT_CHUNK_ROWS = 256


def _weight_scratch(k, n):
    return [pltpu.VMEM((k, n), BF16), pltpu.VMEM((2, WEIGHT_CHUNK_ROWS, n), F32),
            pltpu.SemaphoreType.DMA((2,))]


def _fetch_weight(w_hbm, layer, dst_ref, stage_ref, sem_ref):
    nchunk = dst_ref.shape[0] // WEIGHT_CHUNK_ROWS

    def copy(c):
        rows = pl.ds(c * WEIGHT_CHUNK_ROWS, WEIGHT_CHUNK_ROWS)
        return pltpu.make_async_copy(w_hbm.at[layer, rows, :], stage_ref.at[c % 2], sem_ref.at[c % 2])

    copy(0).start()
    for c in range(nchunk):
        if c + 1 < nchunk:
            copy(c + 1).start()
        copy(c).wait()
        dst_ref[c * WEIGHT_CHUNK_ROWS:(c + 1) * WEIGHT_CHUNK_ROWS, :] = stage_ref[c % 2].astype(BF16)


def _kv_proj_kernel(x_ref, g_ref, wk_hbm, wv_hbm, k_ref, v_ref, wk_ref, wv_ref, stage_ref, sem_ref, *, layer):
    @pl.when(pl.program_id(0) == 0)
    def _():
        _fetch_weight(wk_hbm, layer, wk_ref, stage_ref, sem_ref)
        _fetch_weight(wv_hbm, layer, wv_ref, stage_ref, sem_ref)

    x = x_ref[...]
    hn = (x * _rms_scale(x) * g_ref[...]).astype(BF16)
    k_ref[...] = _dot(hn, wk_ref[...]).astype(BF16)
    v_ref[...] = _dot(hn, wv_ref[...]).astype(BF16)


def _kv_proj(x, g, wk, wv, layer, tm):
    m, d = x.shape
    weight = pl.BlockSpec(memory_space=pl.ANY)
    rows = pl.BlockSpec((tm, d), lambda i: (i, 0))
    w_bf, stage, sem = _weight_scratch(d, d)
    return pl.pallas_call(
        functools.partial(_kv_proj_kernel, layer=layer),
        grid=(m // tm,),
        in_specs=[rows, pl.BlockSpec((1, d), lambda i: (0, 0)), weight, weight],
        out_specs=[rows, rows],
        out_shape=[jax.ShapeDtypeStruct((m, d), BF16)] * 2,
        scratch_shapes=[w_bf, w_bf, stage, sem],
        compiler_params=_cparams(("arbitrary",)),
        name="kv_proj",
    )(x, g.reshape(1, d), wk, wv)


IN_Z0 = A_Q + 2 * A_KV + 2 * B_QK + 2 * B_V
IN_Z1 = IN_Z0 + 2 * GATE_RANK
D_IN = IN_Z1 + 2 * C_WIDTH
IN_CHUNK_ROWS = 512


def _fetch_w_in(wt_hbm, layer, wm_ref, wc_ref, wz_ref, stage_ref, sem_ref):
    wz_ref[...] = jnp.zeros(wz_ref.shape, BF16)
    pieces = []
    for start, stop, dst in ((0, IN_Z0, wm_ref), (IN_Z0, IN_Z1, wz_ref), (IN_Z1, D_IN, wc_ref)):
        for r in range(start, stop, IN_CHUNK_ROWS):
            pieces.append((r, min(IN_CHUNK_ROWS, stop - r), dst, r - start))

    def copy(c):
        src_row, rows, _, _ = pieces[c]
        return pltpu.make_async_copy(wt_hbm.at[layer, pl.ds(src_row, rows), :],
                                     stage_ref.at[c % 2, pl.ds(0, rows), :], sem_ref.at[c % 2])

    copy(0).start()
    for c, (_, rows, dst, dst_row) in enumerate(pieces):
        if c + 1 < len(pieces):
            copy(c + 1).start()
        copy(c).wait()
        dst[dst_row:dst_row + rows, :] = stage_ref[c % 2, 0:rows, :].astype(BF16)


def _in_proj_kernel(x_ref, g_ref, wt_hbm, o_ref, wm_ref, wc_ref, wz_ref, stage_ref, sem_ref, *, layer):
    @pl.when(pl.program_id(0) == 0)
    def _():
        _fetch_w_in(wt_hbm, layer, wm_ref, wc_ref, wz_ref, stage_ref, sem_ref)

    for r in _row_groups(x_ref.shape[0]):
        x = x_ref[r, :]
        hn = (x * _rms_scale(x) * g_ref[...]).astype(BF16)
        o_ref[r, 0:COL_CX] = _dot_nt(hn, wm_ref[...])
        o_ref[r, COL_CX:COL_Z] = _dot_nt(hn, wc_ref[...])
        o_ref[r, COL_Z:D_PROJ] = _dot_nt(hn, wz_ref[...])


def _in_proj(x, g, w_in, layer, tm):
    m, k = x.shape
    assert w_in.shape[1:] == (k, D_IN)
    wt = jnp.swapaxes(w_in, 1, 2)
    return pl.pallas_call(
        functools.partial(_in_proj_kernel, layer=layer),
        grid=(m // tm,),
        in_specs=[pl.BlockSpec((tm, k), lambda i: (i, 0)), pl.BlockSpec((1, k), lambda i: (0, 0)),
                  pl.BlockSpec(memory_space=pl.ANY)],
        out_specs=pl.BlockSpec((tm, D_PROJ), lambda i: (i, 0)),
        out_shape=jax.ShapeDtypeStruct((m, D_PROJ), F32),
        scratch_shapes=[pltpu.VMEM((IN_Z0, k), BF16), pltpu.VMEM((2 * C_WIDTH, k), BF16),
                        pltpu.VMEM((LANE, k), BF16), pltpu.VMEM((2, IN_CHUNK_ROWS, k), F32),
                        pltpu.SemaphoreType.DMA((2,))],
        compiler_params=_cparams(("arbitrary",)),
        name="in_proj",
    )(x, g.reshape(1, k), wt)


def _attn_kernel(sink_ref, q_ref, k_ref, v_ref, bias_ref, o_ref, kb_ref, vt_ref, p_ref, sw_ref):
    seq = q_ref.shape[0]
    nblk = seq // BLOCK
    kb_ref[...] = k_ref[...].astype(BF16)
    for g in range(A_KV_HEADS):
        for n in range(nblk):
            vt_ref[g * nblk + n] = (
                v_ref[n * BLOCK:(n + 1) * BLOCK, g * A_HEAD_DIM:(g + 1) * A_HEAD_DIM].T.astype(BF16))
    sinks = [jnp.concatenate([jnp.full((1, BLOCK), sink_ref[g * A_GROUP + h] * LOG2E, F32)
                              for h in range(A_GROUP)], axis=1) for g in range(A_KV_HEADS)]

    def tiles(first, last):
        return (1 if first else 0), (2 if last else 3)

    def scores(g, n, first, last):
        t0, t1 = tiles(first, last)
        nk = (t1 - t0) * BLOCK
        r0 = pl.multiple_of(n * BLOCK, BLOCK)
        k = kb_ref[pl.ds(pl.multiple_of(r0 + (t0 - 1) * BLOCK, BLOCK), nk),
                   g * A_HEAD_DIM:(g + 1) * A_HEAD_DIM]
        q = jnp.concatenate(
            [q_ref[pl.ds(r0, BLOCK), (g * A_GROUP + h) * A_HEAD_DIM:(g * A_GROUP + h + 1) * A_HEAD_DIM]
             for h in range(A_GROUP)], axis=0).astype(BF16)
        s = _dot_nt(k, q) * (A_HEAD_DIM ** -0.5 * LOG2E) + bias_ref[g, t0 * BLOCK:t1 * BLOCK, :]
        m = jnp.maximum(jnp.max(s, axis=0, keepdims=True), sinks[g])
        p_ref[g, 0:nk, :] = jnp.exp2(s - m).astype(BF16)
        sw_ref[g] = jnp.exp2(sinks[g] - m)

    def output(g, n, first, last):
        t0, t1 = tiles(first, last)
        nk = (t1 - t0) * BLOCK
        r0 = pl.multiple_of(n * BLOCK, BLOCK)
        v_t = jnp.concatenate([vt_ref[g * nblk + n + t - 1] for t in range(t0, t1)], axis=1)
        ones = jnp.ones((2 * SUBLANES, nk), BF16)
        o = _dot(jnp.concatenate([v_t, ones], axis=0), p_ref[g, 0:nk, :])
        denom = o[A_HEAD_DIM:A_HEAD_DIM + 1, :] + sw_ref[g]
        o = o[0:A_HEAD_DIM, :] / denom
        for h in range(A_GROUP):
            col = (g * A_GROUP + h) * A_HEAD_DIM
            o_ref[pl.ds(r0, BLOCK), col:col + A_HEAD_DIM] = (
                o[:, h * BLOCK:(h + 1) * BLOCK].T.astype(o_ref.dtype))

    def both(fn, *args):
        for g in range(A_KV_HEADS):
            fn(g, *args)

    def body(n, carry):
        both(output, n, False, False)
        both(scores, n + 1, False, False)
        return carry

    both(scores, 0, True, False)
    both(output, 0, True, False)
    both(scores, 1, False, False)
    lax.fori_loop(1, nblk - 2, body, 0)
    both(output, nblk - 2, False, False)
    both(scores, nblk - 1, False, True)
    both(output, nblk - 1, False, True)


def _windowed_attention(proj, bias_tab, sink, bsz, seq):
    qw = A_GROUP * A_HEAD_DIM
    nblk = seq // BLOCK
    return pl.pallas_call(
        _attn_kernel,
        grid=(bsz,),
        in_specs=[pl.BlockSpec(memory_space=pltpu.SMEM),
                  pl.BlockSpec((seq, A_Q), lambda b: (b, COL_AQ // A_Q)),
                  pl.BlockSpec((seq, A_KV), lambda b: (b, COL_AK // A_KV)),
                  pl.BlockSpec((seq, A_KV), lambda b: (b, COL_AV // A_KV)),
                  pl.BlockSpec(bias_tab.shape, lambda b: (0, 0, 0))],
        out_specs=pl.BlockSpec((seq, A_Q), lambda b: (b, 0)),
        out_shape=jax.ShapeDtypeStruct((bsz * seq, A_Q), BF16),
        scratch_shapes=[pltpu.VMEM((seq, A_KV), BF16),
                        pltpu.VMEM((A_KV_HEADS * nblk, A_HEAD_DIM, BLOCK), BF16),
                        pltpu.VMEM((A_KV_HEADS, 3 * BLOCK, qw), BF16),
                        pltpu.VMEM((A_KV_HEADS, 1, qw), F32)],
        compiler_params=_cparams(("parallel",)),
        name="windowed_attention",
    )(sink, proj, proj, proj, bias_tab)


def _t5_bucket(rel):
    nb = N_BUCKETS // 2
    max_exact = nb // 2
    ret = jnp.where(rel > 0, nb, 0)
    n = jnp.abs(rel)
    nf = jnp.maximum(n, 1).astype(F32)
    large = max_exact + (jnp.log(nf / max_exact) / math.log(MAX_DISTANCE / max_exact)
                         * (nb - max_exact)).astype(jnp.int32)
    large = jnp.minimum(large, nb - 1)
    return ret + jnp.where(n < max_exact, n, large)


def _attention_bias_table(rel_bias):
    kj = jnp.arange(3 * BLOCK)[:, None]
    qi = jnp.arange(BLOCK)[None, :]
    rel = kj - BLOCK - qi
    onehot = jax.nn.one_hot(_t5_bucket(rel), N_BUCKETS, dtype=F32)
    bias = jnp.einsum("kqn,nh->hkq", onehot, rel_bias.astype(F32), precision=lax.Precision.HIGHEST)
    bias = jnp.where((jnp.abs(rel) <= WINDOW)[None], bias * LOG2E, NEG_INF)
    bias = bias.reshape(A_KV_HEADS, A_GROUP, 3 * BLOCK, BLOCK)
    return jnp.transpose(bias, (0, 2, 1, 3)).reshape(A_KV_HEADS, 3 * BLOCK, A_GROUP * BLOCK)


def _gla_masks():
    c = GLA_BLOCK
    i = np.arange(c)[:, None]
    j = np.arange(c)[None, :]
    fwd = [i == j]
    for s in GLA_LEVELS:
        fwd.append((i // (2 * s) == j // (2 * s)) & ((i // s) % 2 == 1) & ((j // s) % 2 == 0))
    fwd = np.stack(fwd)
    return jnp.asarray(np.stack([fwd, fwd.transpose(0, 2, 1)]).astype(np.float32), dtype=BF16)


def _gla_block(direction, rows, q_ref, k_ref, v_ref, z_ref, w2_ref, b2_ref, m_ref, st_ref):
    c = GLA_BLOCK
    q = q_ref[rows, :] * (B_KEY_DIM ** -0.5)
    k = k_ref[rows, :]
    v = v_ref[rows, :].astype(BF16)
    gate_cols = slice(direction * B_QK, (direction + 1) * B_QK)
    pre = _dot(z_ref[rows, :].astype(BF16), w2_ref[:, gate_cols]) + b2_ref[:, gate_cols]
    log_a = -_softplus(-pre) * (LOG2E / GATE_TAU)

    row = lax.broadcasted_iota(jnp.int32, (c, B_QK), 0)
    first_head = lax.broadcasted_iota(jnp.int32, (c, LANE), 1) < B_KEY_DIM
    pair = [slice(p * LANE, (p + 1) * LANE) for p in range(GLA_PAIRS)]
    q_b = [q[:, ps].astype(BF16) for ps in pair]
    k_b = [k[:, ps].astype(BF16) for ps in pair]
    k_first = [jnp.where(first_head, kb, jnp.zeros_like(kb)) for kb in k_b]
    k_second = [jnp.where(first_head, jnp.zeros_like(kb), kb) for kb in k_b]

    def exponents(p_s, t_s):
        if direction == 0:
            return p_s, t_s - p_s
        return t_s - p_s + log_a, p_s - log_a

    def scaled(e_q, e_k, p):
        w_q = jnp.exp2(e_q[:, pair[p]]).astype(BF16)
        w_k = jnp.exp2(e_k[:, pair[p]]).astype(BF16)
        return q_b[p] * w_q, jnp.concatenate([k_first[p] * w_k, k_second[p] * w_k], axis=0)

    attn = [None] * GLA_PAIRS

    def add_level(level, q_t, k_t, p):
        mask = m_ref[direction, level]
        contrib = jnp.concatenate([mask, mask], axis=1) * _dot_nt(q_t, k_t).astype(BF16)
        attn[p] = contrib if attn[p] is None else attn[p] + contrib

    if direction == 0:
        for p in range(GLA_PAIRS):
            add_level(0, q_b[p], jnp.concatenate([k_first[p], k_second[p]], axis=0), p)
    p_s = log_a
    t_s = log_a
    for li, s in enumerate(GLA_LEVELS):
        e_q, e_k = exponents(p_s, t_s)
        for p in range(GLA_PAIRS):
            q_t, k_t = scaled(e_q, e_k, p)
            add_level(li + 1, q_t, k_t, p)
        upper = (row & s) != 0
        t_before = pltpu.roll(t_s, s, 0)
        t_after = pltpu.roll(t_s, c - s, 0)
        p_s = p_s + jnp.where(upper, t_before, 0.0)
        t_s = t_s + jnp.where(upper, t_before, t_after)

    e_q, e_k = exponents(p_s, t_s)
    state_decay = jnp.exp2(t_s[0:1, :])
    own_block = lax.broadcasted_iota(jnp.int32, (2 * B_VAL_DIM, LANE), 0) < B_VAL_DIM
    own_block = own_block == (lax.broadcasted_iota(jnp.int32, (2 * B_VAL_DIM, LANE), 1) < B_KEY_DIM)
    zero_v = jnp.zeros((c, B_VAL_DIM), BF16)
    outs = []
    for p in range(GLA_PAIRS):
        v_pair = v[:, 2 * p * B_VAL_DIM:(2 * p + 2) * B_VAL_DIM]
        v_diag = jnp.concatenate(
            [jnp.concatenate([v_pair[:, :B_VAL_DIM], zero_v], axis=1),
             jnp.concatenate([zero_v, v_pair[:, B_VAL_DIM:]], axis=1)], axis=0)
        state_t = st_ref[direction, p]
        q_t = q_b[p] * jnp.exp2(e_q[:, pair[p]]).astype(BF16)
        k_t = k_b[p] * jnp.exp2(e_k[:, pair[p]]).astype(BF16)
        o = _dot_nt(q_t, state_t.astype(BF16)) + _dot(attn[p], v_diag)
        update = jnp.where(own_block, _dot_tn(v_pair, k_t), 0.0)
        st_ref[direction, p] = state_t * state_decay[:, pair[p]] + update
        outs.append(o)
    return jnp.concatenate(outs, axis=1)


def _gla_kernel(q_ref, k_ref, v_ref, g_ref, z_ref, w2_ref, b2_ref, gn_ref, m_ref,
                o_ref, of_ref, ob_ref, st_ref):
    nblk = q_ref.shape[0] // GLA_BLOCK
    st_ref[...] = jnp.zeros(st_ref.shape, F32)
    block = functools.partial(_gla_block, q_ref=q_ref, k_ref=k_ref, v_ref=v_ref, z_ref=z_ref,
                              w2_ref=w2_ref, b2_ref=b2_ref, m_ref=m_ref, st_ref=st_ref)

    def rows_of(n):
        return pl.ds(pl.multiple_of(n * GLA_BLOCK, GLA_BLOCK), GLA_BLOCK)

    def finish(rows, o):
        normed = []
        for h in range(B_HEADS):
            oh = o[:, h * B_VAL_DIM:(h + 1) * B_VAL_DIM]
            normed.append(oh * _rms_scale(oh))
        g = g_ref[rows, :]
        out = jnp.concatenate(normed, axis=1) * gn_ref[...] * (g * jax.nn.sigmoid(g))
        o_ref[rows, :] = out.astype(o_ref.dtype)

    def first_half(n, carry):
        rows_f, rows_b = rows_of(n), rows_of(nblk - 1 - n)
        of_ref[rows_f, :] = block(0, rows_f)
        ob_ref[rows_b, :] = block(1, rows_b)
        return carry

    def second_half(n, carry):
        rows_f, rows_b = rows_of(n), rows_of(nblk - 1 - n)
        finish(rows_f, block(0, rows_f) + ob_ref[rows_f, :])
        finish(rows_b, of_ref[rows_b, :] + block(1, rows_b))
        return carry

    lax.fori_loop(0, nblk // 2, first_half, 0)
    lax.fori_loop(nblk // 2, nblk, second_half, 0)


def _gla(proj, w2, layer, b2, gn, masks, bsz, seq):
    def col(width, start):
        return pl.BlockSpec((seq, width), lambda b: (b, start // width))

    def whole(a):
        return pl.BlockSpec(a.shape, lambda b: (0,) * a.ndim)

    return pl.pallas_call(
        _gla_kernel,
        grid=(bsz,),
        in_specs=[col(B_QK, COL_BQ), col(B_QK, COL_BK), col(B_V, COL_BV), col(B_V, COL_BG),
                  col(LANE, COL_Z), pl.BlockSpec((None,) + w2.shape[1:], lambda b: (layer, 0, 0)),
                  whole(b2), whole(gn), whole(masks)],
        out_specs=pl.BlockSpec((seq, B_V), lambda b: (b, 0)),
        out_shape=jax.ShapeDtypeStruct((bsz * seq, B_V), BF16),
        scratch_shapes=[pltpu.VMEM((seq, B_V), F32), pltpu.VMEM((seq, B_V), F32),
                        pltpu.VMEM((2, GLA_PAIRS, 2 * B_VAL_DIM, LANE), F32)],
        compiler_params=_cparams(("parallel",)),
        name="gla",
    )(proj, proj, proj, proj, proj, w2, b2, gn, masks)


LRU_SEG = 256
SUBLANES = 8
LRU_PITCH = LRU_SEG + SUBLANES


def _lru_kernel(x_ref, y_ref, cw_ref, cb_ref, wa_ref, wx_ref, ba_ref, bx_ref, lam_ref, o_ref,
                xp_ref, af_ref, hf_ref, ab_ref, hb_ref):
    seq = x_ref.shape[0]
    ch = x_ref.shape[1]
    nseg = seq // LRU_SEG
    halo = SUBLANES
    xp_ref[0:halo, :] = jnp.zeros((halo, ch), F32)
    xp_ref[seq + halo:seq + 2 * halo, :] = jnp.zeros((halo, ch), F32)
    xp_ref[halo:seq + halo, :] = x_ref[...]
    decay_rate = _softplus(-lam_ref[...])
    scan_refs = ((af_ref, hf_ref), (ab_ref, hb_ref))

    def gates(c, carry):
        r0 = pl.multiple_of(c * LRU_SEG, LRU_SEG)
        dst = pl.ds(pl.multiple_of(c * LRU_PITCH, SUBLANES), LRU_SEG)
        ext = LRU_SEG + 2 * halo
        xe = xp_ref[pl.ds(r0, ext), :]
        xc = (cw_ref[0:1, :] * pltpu.roll(xe, 2, 0) + cw_ref[1:2, :] * pltpu.roll(xe, 1, 0)
              + cw_ref[2:3, :] * xe + cw_ref[3:4, :] * pltpu.roll(xe, ext - 1, 0))
        xc = xc[halo:halo + LRU_SEG] + cb_ref[...]
        xb = xc.astype(BF16)
        for d in range(2):
            r = _sigmoid(_dot(xb, wa_ref[d, 0]) + ba_ref[d:d + 1, :])
            i = _sigmoid(_dot(xb, wx_ref[d, 0]) + bx_ref[d:d + 1, :])
            log_a = -LRU_C * r * decay_rate[d:d + 1, :]
            t = jnp.tanh(log_a)
            a_ref, u_ref = scan_refs[d]
            a_ref[dst, :] = jnp.exp(log_a)
            u_ref[dst, :] = jnp.sqrt(-2.0 * t / (1.0 - t)) * (i * xc)
        return carry

    lax.fori_loop(0, nseg, gates, 0)

    def scan_step(j, carry):
        hf, pf, hb, pb = carry
        rf = pl.ds(j, nseg, stride=LRU_PITCH)
        a = af_ref[rf, :]
        hf = a * hf + hf_ref[rf, :]
        pf = a * pf
        hf_ref[rf, :] = hf
        af_ref[rf, :] = pf
        rb = pl.ds(LRU_SEG - 1 - j, nseg, stride=LRU_PITCH)
        a = ab_ref[rb, :]
        hb = a * hb + hb_ref[rb, :]
        pb = a * pb
        hb_ref[rb, :] = hb
        ab_ref[rb, :] = pb
        return hf, pf, hb, pb

    zeros = jnp.zeros((nseg, ch), F32)
    ones = jnp.ones((nseg, ch), F32)
    lax.fori_loop(0, LRU_SEG, scan_step, (zeros, ones, zeros, ones), unroll=4)

    enter_f = [None] * nseg
    state = jnp.zeros((1, ch), F32)
    for s in range(nseg):
        enter_f[s] = state
        last = s * LRU_PITCH + LRU_SEG - 1
        state = hf_ref[last:last + 1, :] + af_ref[last:last + 1, :] * state
    enter_b = [None] * nseg
    state = jnp.zeros((1, ch), F32)
    for s in reversed(range(nseg)):
        enter_b[s] = state
        first = s * LRU_PITCH
        state = hb_ref[first:first + 1, :] + ab_ref[first:first + 1, :] * state

    for s in range(nseg):
        seg = slice(s * LRU_PITCH, s * LRU_PITCH + LRU_SEG)
        rows = slice(s * LRU_SEG, (s + 1) * LRU_SEG)
        h = (hf_ref[seg, :] + af_ref[seg, :] * enter_f[s]) + (hb_ref[seg, :] + ab_ref[seg, :] * enter_b[s])
        o_ref[rows, :] = (h * jax.nn.gelu(y_ref[rows, :])).astype(o_ref.dtype)


def _rglru(proj, conv_w, conv_b, w_a, w_x, layer, b_a, b_x, lam, bsz, seq):
    ch = C_BLOCK_DIM

    def col(start):
        return pl.BlockSpec((seq, ch), lambda b, c: (b, start // ch + c))

    def per_block(rows):
        return pl.BlockSpec((rows, ch), lambda b, c: (0, c))

    gate_w = pl.BlockSpec((None, 2, 1, ch, ch), lambda b, c: (layer, 0, c, 0, 0))
    return pl.pallas_call(
        _lru_kernel,
        grid=(bsz, C_BLOCKS),
        in_specs=[col(COL_CX), col(COL_CY), per_block(CONV_WIDTH), per_block(1), gate_w, gate_w,
                  per_block(2), per_block(2), per_block(2)],
        out_specs=pl.BlockSpec((seq, ch), lambda b, c: (b, c)),
        out_shape=jax.ShapeDtypeStruct((bsz * seq, C_WIDTH), BF16),
        scratch_shapes=([pltpu.VMEM((seq + 2 * SUBLANES, ch), F32)]
                        + [pltpu.VMEM((seq // LRU_SEG * LRU_PITCH, ch), F32)] * 4),
        compiler_params=_cparams(("parallel", "parallel")),
        name="rglru",
    )(proj, proj, conv_w, conv_b.reshape(1, C_WIDTH), w_a, w_x, b_a, b_x, lam)


def _out_proj_kernel(oa_ref, ob_ref, oc_ref, w_hbm, g_ref, x_ref, o_ref, w_ref, stage_ref, sem_ref, *, layer):
    @pl.when(pl.program_id(0) == 0)
    def _():
        _fetch_weight(w_hbm, layer, w_ref, stage_ref, sem_ref)

    for r in _row_groups(x_ref.shape[0]):
        mixed = (_dot(oa_ref[r, :], w_ref[0:A_Q, :]) + _dot(ob_ref[r, :], w_ref[A_Q:A_Q + B_V, :])
                 + _dot(oc_ref[r, :], w_ref[A_Q + B_V:A_Q + B_V + C_WIDTH, :]))
        o_ref[r, :] = x_ref[r, :] + mixed * _rms_scale(mixed) * g_ref[...]


def _out_proj(oa, ob, oc, w, layer, g, x, tm):
    m, d = x.shape

    def rows(width):
        return pl.BlockSpec((tm, width), lambda i: (i, 0))

    return pl.pallas_call(
        functools.partial(_out_proj_kernel, layer=layer),
        grid=(m // tm,),
        in_specs=[rows(A_Q), rows(B_V), rows(C_WIDTH),
                  pl.BlockSpec(memory_space=pl.ANY),
                  pl.BlockSpec((1, d), lambda i: (0, 0)),
                  rows(d)],
        out_specs=rows(d),
        out_shape=jax.ShapeDtypeStruct((m, d), F32),
        scratch_shapes=_weight_scratch(w.shape[1], w.shape[2]),
        compiler_params=_cparams(("arbitrary",)),
        name="out_proj",
    )(oa, ob, oc, w, g.reshape(1, d), x)


def _cross_attn_kernel(x_ref, gpre_ref, wq_hbm, k_ref, v_ref, wo_hbm, gpost_ref, o_ref,
                       wq_ref, wo_ref, stage_ref, sem_ref, *, layer):
    @pl.when(jnp.logical_and(pl.program_id(0) == 0, pl.program_id(1) == 0))
    def _():
        _fetch_weight(wq_hbm, layer, wq_ref, stage_ref, sem_ref)
        _fetch_weight(wo_hbm, layer, wo_ref, stage_ref, sem_ref)

    x = x_ref[...]
    hn = (x * _rms_scale(x) * gpre_ref[...]).astype(BF16)
    q = _dot(hn, wq_ref[...]).astype(BF16)
    acc = None
    for h in range(X_HEADS):
        hs = slice(h * X_HEAD_DIM, (h + 1) * X_HEAD_DIM)
        s = _dot_nt(q[:, hs], k_ref[:, hs]) * (X_HEAD_DIM ** -0.5)
        m = jnp.max(s, axis=-1, keepdims=True)
        p = jnp.exp(s - m)
        o = _dot(p.astype(BF16), v_ref[:, hs]) / jnp.sum(p, axis=-1, keepdims=True)
        part = _dot(o.astype(BF16), wo_ref[hs, :])
        acc = part if acc is None else acc + part
    o_ref[...] = x + acc * _rms_scale(acc) * gpost_ref[...]


def _cross_attention(x, gpre, wq, k, v, wo, layer, gpost, bsz, seq, tq):
    d = D_MODEL
    per_row = seq // tq
    mem_spec = pl.BlockSpec((MEM_LEN, d), lambda b, i: (b, 0))
    gain_spec = pl.BlockSpec((1, d), lambda b, i: (0, 0))
    weight_spec = pl.BlockSpec(memory_space=pl.ANY)
    w_bf, stage, sem = _weight_scratch(d, d)
    row_spec = pl.BlockSpec((tq, d), lambda b, i: (b * per_row + i, 0))
    return pl.pallas_call(
        functools.partial(_cross_attn_kernel, layer=layer),
        grid=(bsz, per_row),
        in_specs=[row_spec, gain_spec, weight_spec, mem_spec, mem_spec, weight_spec, gain_spec],
        out_specs=row_spec,
        out_shape=jax.ShapeDtypeStruct((bsz * seq, d), F32),
        scratch_shapes=[w_bf, w_bf, stage, sem],
        compiler_params=_cparams(("arbitrary", "arbitrary")),
        name="cross_attention",
    )(x, gpre.reshape(1, d), wq, k, v, wo, gpost.reshape(1, d))


def _ffn_kernel(x_ref, gpre_ref, wu_ref, wd_ref, gpost_ref, o_ref, hn_ref, acc_ref):
    f = pl.program_id(1)
    last = pl.num_programs(1) - 1

    def step(is_first, is_last):
        for r in _row_groups(x_ref.shape[0]):
            if is_first:
                x = x_ref[r, :]
                hn_ref[r, :] = (x * _rms_scale(x) * gpre_ref[...]).astype(BF16)
            up = _dot(hn_ref[r, :], wu_ref[...])
            act = jnp.square(jnp.maximum(up, 0.0)).astype(BF16)
            part = _dot(act, wd_ref[...])
            ff = part if is_first else acc_ref[r, :] + part
            if is_last:
                o_ref[r, :] = x_ref[r, :] + ff * _rms_scale(ff) * gpost_ref[...]
            else:
                acc_ref[r, :] = ff

    pl.when(f == 0)(functools.partial(step, True, False))
    pl.when(jnp.logical_and(f > 0, f < last))(functools.partial(step, False, False))
    pl.when(f == last)(functools.partial(step, False, True))


def _ffn_rest(x, gpre, w_up, w_down, gpost, tm, tf):
    m, d = x.shape
    dff = w_up.shape[1]
    return pl.pallas_call(
        _ffn_kernel,
        grid=(m // tm - 1, dff // tf),
        in_specs=[pl.BlockSpec((tm, d), lambda i, f: (i + 1, 0)),
                  pl.BlockSpec((1, d), lambda i, f: (0, 0)),
                  pl.BlockSpec((d, tf), lambda i, f: (0, f)),
                  pl.BlockSpec((tf, d), lambda i, f: (f, 0)),
                  pl.BlockSpec((1, d), lambda i, f: (0, 0))],
        out_specs=pl.BlockSpec((tm, d), lambda i, f: (i + 1, 0)),
        out_shape=jax.ShapeDtypeStruct((m, d), F32),
        input_output_aliases={0: 0},
        scratch_shapes=[pltpu.VMEM((tm, d), BF16), pltpu.VMEM((tm, d), F32)],
        compiler_params=_cparams(("parallel", "arbitrary")),
        name="ffn",
    )(x, gpre.reshape(1, d), w_up, w_down, gpost.reshape(1, d))


def _ffn_first_kernel(x_ref, gpre_ref, wu_ref, wd_ref, gpost_ref, o_ref, wub_ref, wdb_ref, hn_ref, acc_ref):
    f = pl.program_id(0)
    wu = wu_ref[...].astype(BF16)
    wd = wd_ref[...].astype(BF16)
    wub_ref[...] = wu
    wdb_ref[...] = wd

    @pl.when(f == 0)
    def _():
        x = x_ref[...]
        hn_ref[...] = (x * _rms_scale(x) * gpre_ref[...]).astype(BF16)
        acc_ref[...] = jnp.zeros(acc_ref.shape, F32)

    up = _dot(hn_ref[...], wu)
    act = jnp.square(jnp.maximum(up, 0.0)).astype(BF16)
    acc_ref[...] += _dot(act, wd)

    @pl.when(f == pl.num_programs(0) - 1)
    def _():
        ff = acc_ref[...]
        o_ref[...] = x_ref[...] + ff * _rms_scale(ff) * gpost_ref[...]


def _ffn_first(x, gpre, w_up, w_down, layer, gpost, tm, tf):
    m, d = x.shape
    dff = w_up.shape[2]
    return pl.pallas_call(
        _ffn_first_kernel,
        grid=(dff // tf,),
        in_specs=[pl.BlockSpec((tm, d), lambda f: (0, 0)),
                  pl.BlockSpec((1, d), lambda f: (0, 0)),
                  pl.BlockSpec((None, d, tf), lambda f: (layer, 0, f)),
                  pl.BlockSpec((None, tf, d), lambda f: (layer, f, 0)),
                  pl.BlockSpec((1, d), lambda f: (0, 0))],
        out_specs=[pl.BlockSpec((tm, d), lambda f: (0, 0)),
                   pl.BlockSpec((d, tf), lambda f: (0, f)),
                   pl.BlockSpec((tf, d), lambda f: (f, 0))],
        out_shape=[jax.ShapeDtypeStruct((tm, d), F32), jax.ShapeDtypeStruct((d, dff), BF16),
                   jax.ShapeDtypeStruct((dff, d), BF16)],
        scratch_shapes=[pltpu.VMEM((tm, d), BF16), pltpu.VMEM((tm, d), F32)],
        compiler_params=_cparams(("arbitrary",)),
        name="ffn_first",
    )(x, gpre.reshape(1, d), w_up, w_down, gpost.reshape(1, d))


def _ffn(x, gpre, w_up, w_down, layer, gpost, tm):
    first, w_up_b, w_down_b = _ffn_first(x, gpre, w_up, w_down, layer, gpost, tm, tf=512)
    rest = _ffn_rest(x, gpre, w_up_b, w_down_b, gpost, tm, tf=1024)
    return lax.dynamic_update_slice(rest, first, (0, 0))


def _gate_expansion(w2_f, w2_b):
    w = jnp.zeros((w2_f.shape[0], LANE, 2 * B_QK), F32)
    w = w.at[:, 0:GATE_RANK, 0:B_QK].set(w2_f)
    w = w.at[:, GATE_RANK:2 * GATE_RANK, B_QK:].set(w2_b)
    return w.astype(BF16)


def kernel(x, mem, rel_bias, w_in, w_out, attn_sink, gla_w2_f, gla_b2_f, gla_w2_b, gla_b2_b, gla_norm, conv_w, conv_b, lru_wa, lru_ba, lru_wx, lru_bx, lru_lambda, xq, xk, xv, xo, w_up, w_down, norm_mix_pre, norm_mix_post, norm_mem, norm_x_pre, norm_x_post, norm_ff_pre, norm_ff_post):
    bsz, seq, d = x.shape
    assert d == D_MODEL and seq == SUBLANES * LRU_SEG
    xs = x.reshape(bsz * seq, d)
    mems = mem.reshape(bsz * mem.shape[1], d)
    bias_tab = _attention_bias_table(rel_bias)
    masks = _gla_masks()
    w2_b = _gate_expansion(gla_w2_f, gla_w2_b)
    b2 = jnp.concatenate([gla_b2_f, gla_b2_b], axis=-1)
    bf = lambda w: w.astype(BF16)
    wa_b, wx_b = bf(lru_wa), bf(lru_wx)
    for l in range(DEPTH):
        proj = _in_proj(xs, norm_mix_pre[l], w_in, l, tm=512)
        oa = _windowed_attention(proj, bias_tab, attn_sink[l], bsz, seq)
        ob = _gla(proj, w2_b, l, b2[l].reshape(1, 2 * B_QK), gla_norm[l].reshape(1, B_V), masks, bsz, seq)
        oc = _rglru(proj, conv_w[l], conv_b[l], wa_b, wx_b, l, lru_ba[l], lru_bx[l], lru_lambda[l],
                    bsz, seq)
        xs = _out_proj(oa, ob, oc, w_out, l, norm_mix_post[l], xs, tm=512)

        k, v = _kv_proj(mems, norm_mem[l], xk, xv, l, tm=512)
        xs = _cross_attention(xs, norm_x_pre[l], xq, k, v, xo, l, norm_x_post[l], bsz, seq, tq=512)

        xs = _ffn(xs, norm_ff_pre[l], w_up, w_down, l, norm_ff_post[l], tm=512)
    return xs.reshape(bsz, seq, d)
```

```python
import functools
import math

import numpy as np
import jax
import jax.numpy as jnp
from jax import lax
from jax.experimental import pallas as pl
from jax.experimental.pallas import tpu as pltpu

F32 = jnp.float32
BF16 = jnp.bfloat16

D_MODEL = 2048
DEPTH = 4
MEM_LEN = 256
A_HEAD_DIM = 128
A_HEADS = 8
A_KV_HEADS = 2
A_GROUP = 4
WINDOW = 128
BLOCK = 128
N_BUCKETS = 32
MAX_DISTANCE = 128
B_HEADS = 4
B_KEY_DIM = 64
B_VAL_DIM = 128
GATE_RANK = 16
GATE_TAU = 16.0
C_WIDTH = 512
C_BLOCKS = 4
C_BLOCK_DIM = 128
CONV_WIDTH = 4
LRU_C = 8.0
X_HEADS = 4
X_HEAD_DIM = 512
D_FF = 8192
EPS = 1e-6
NEG_INF = -1e30
LOG2E = math.log2(math.e)

A_Q = A_HEADS * A_HEAD_DIM
A_KV = A_KV_HEADS * A_HEAD_DIM
B_QK = B_HEADS * B_KEY_DIM
B_V = B_HEADS * B_VAL_DIM

LANE = 128
COL_AQ = 0
COL_AK = COL_AQ + A_Q
COL_AV = COL_AK + A_KV
COL_BQ = COL_AV + A_KV
COL_BK = COL_BQ + B_QK
COL_BV = COL_BK + B_QK
COL_BG = COL_BV + B_V
COL_CX = COL_BG + B_V
COL_CY = COL_CX + C_WIDTH
COL_Z = COL_CY + C_WIDTH
D_PROJ = COL_Z + LANE

GLA_BLOCK = 128
GLA_LEVELS = (1, 2, 4, 8, 16, 32, 64)
GLA_PAIRS = B_HEADS // 2
VMEM_LIMIT_MIB = 56


def _cparams(semantics, vmem_mib=VMEM_LIMIT_MIB):
    return pltpu.CompilerParams(dimension_semantics=semantics,
                                vmem_limit_bytes=vmem_mib * 1024 * 1024)


def _dot(a, b):
    return jnp.dot(a, b, preferred_element_type=F32)


def _dot_nt(a, b):
    return lax.dot_general(a, b, (((1,), (1,)), ((), ())), preferred_element_type=F32)


def _dot_tn(a, b):
    return lax.dot_general(a, b, (((0,), (0,)), ((), ())), preferred_element_type=F32)


def _softplus(x):
    return jnp.maximum(x, 0.0) + jnp.log(1.0 + jnp.exp(-jnp.abs(x)))


def _sigmoid(x):
    return 0.5 * jnp.tanh(0.5 * x) + 0.5


def _rms_scale(x):
    return lax.rsqrt(jnp.mean(jnp.square(x), axis=-1, keepdims=True) + EPS)


ROW_CHAINS = 2


def _row_groups(rows):
    step = rows // ROW_CHAINS
    return [slice(c * step, (c + 1) * step) for c in range(ROW_CHAINS)]


def _resident(shape, index_map):
    return pl.BlockSpec(shape, index_map, pipeline_mode=pl.Buffered(1))


WEIGHT_CHUNK_ROWS = 512


def _weight_scratch(k, n):
    return [pltpu.VMEM((k, n), BF16), pltpu.VMEM((2, WEIGHT_CHUNK_ROWS, n), F32),
            pltpu.SemaphoreType.DMA((2,))]


def _fetch_weight(w_hbm, layer, dst_ref, stage_ref, sem_ref):
    nchunk = dst_ref.shape[0] // WEIGHT_CHUNK_ROWS

    def copy(c):
        rows = pl.ds(c * WEIGHT_CHUNK_ROWS, WEIGHT_CHUNK_ROWS)
        return pltpu.make_async_copy(w_hbm.at[layer, rows, :], stage_ref.at[c % 2], sem_ref.at[c % 2])

    copy(0).start()
    for c in range(nchunk):
        if c + 1 < nchunk:
            copy(c + 1).start()
        copy(c).wait()
        dst_ref[c * WEIGHT_CHUNK_ROWS:(c + 1) * WEIGHT_CHUNK_ROWS, :] = stage_ref[c % 2].astype(BF16)


def _kv_proj_kernel(x_ref, g_ref, wk_hbm, wv_hbm, k_ref, v_ref, wk_ref, wv_ref, stage_ref, sem_ref, *, layer):
    @pl.when(pl.program_id(0) == 0)
    def _():
        _fetch_weight(wk_hbm, layer, wk_ref, stage_ref, sem_ref)
        _fetch_weight(wv_hbm, layer, wv_ref, stage_ref, sem_ref)

    x = x_ref[...]
    hn = (x * _rms_scale(x) * g_ref[...]).astype(BF16)
    k_ref[...] = _dot(hn, wk_ref[...]).astype(BF16)
    v_ref[...] = _dot(hn, wv_ref[...]).astype(BF16)


def _kv_proj(x, g, wk, wv, layer, tm):
    m, d = x.shape
    weight = pl.BlockSpec(memory_space=pl.ANY)
    rows = pl.BlockSpec((tm, d), lambda i: (i, 0))
    w_bf, stage, sem = _weight_scratch(d, d)
    return pl.pallas_call(
        functools.partial(_kv_proj_kernel, layer=layer),
        grid=(m // tm,),
        in_specs=[rows, pl.BlockSpec((1, d), lambda i: (0, 0)), weight, weight],
        out_specs=[rows, rows],
        out_shape=[jax.ShapeDtypeStruct((m, d), BF16)] * 2,
        scratch_shapes=[w_bf, w_bf, stage, sem],
        compiler_params=_cparams(("arbitrary",)),
        name="kv_proj",
    )(x, g.reshape(1, d), wk, wv)


IN_Z0 = A_Q + 2 * A_KV + 2 * B_QK + 2 * B_V
IN_Z1 = IN_Z0 + 2 * GATE_RANK
D_IN = IN_Z1 + 2 * C_WIDTH
IN_CHUNK_ROWS = 512


def _fetch_w_in(wt_hbm, layer, wm_ref, wc_ref, wz_ref, stage_ref, sem_ref):
    wz_ref[...] = jnp.zeros(wz_ref.shape, BF16)
    pieces = []
    for start, stop, dst in ((0, IN_Z0, wm_ref), (IN_Z0, IN_Z1, wz_ref), (IN_Z1, D_IN, wc_ref)):
        for r in range(start, stop, IN_CHUNK_ROWS):
            pieces.append((r, min(IN_CHUNK_ROWS, stop - r), dst, r - start))

    def copy(c):
        src_row, rows, _, _ = pieces[c]
        return pltpu.make_async_copy(wt_hbm.at[layer, pl.ds(src_row, rows), :],
                                     stage_ref.at[c % 2, pl.ds(0, rows), :], sem_ref.at[c % 2])

    copy(0).start()
    for c, (_, rows, dst, dst_row) in enumerate(pieces):
        if c + 1 < len(pieces):
            copy(c + 1).start()
        copy(c).wait()
        dst[dst_row:dst_row + rows, :] = stage_ref[c % 2, 0:rows, :].astype(BF16)


def _in_proj_kernel(x_ref, g_ref, wt_hbm, o_ref, wm_ref, wc_ref, wz_ref, stage_ref, sem_ref, *, layer):
    @pl.when(pl.program_id(0) == 0)
    def _():
        _fetch_w_in(wt_hbm, layer, wm_ref, wc_ref, wz_ref, stage_ref, sem_ref)

    for r in _row_groups(x_ref.shape[0]):
        x = x_ref[r, :]
        hn = (x * _rms_scale(x) * g_ref[...]).astype(BF16)
        o_ref[r, 0:COL_CX] = _dot_nt(hn, wm_ref[...])
        o_ref[r, COL_CX:COL_Z] = _dot_nt(hn, wc_ref[...])
        o_ref[r, COL_Z:D_PROJ] = _dot_nt(hn, wz_ref[...])


def _in_proj(x, g, w_in, layer, tm):
    m, k = x.shape
    assert w_in.shape[1:] == (k, D_IN)
    wt = jnp.swapaxes(w_in, 1, 2)
    return pl.pallas_call(
        functools.partial(_in_proj_kernel, layer=layer),
        grid=(m // tm,),
        in_specs=[pl.BlockSpec((tm, k), lambda i: (i, 0)), pl.BlockSpec((1, k), lambda i: (0, 0)),
                  pl.BlockSpec(memory_space=pl.ANY)],
        out_specs=pl.BlockSpec((tm, D_PROJ), lambda i: (i, 0)),
        out_shape=jax.ShapeDtypeStruct((m, D_PROJ), F32),
        scratch_shapes=[pltpu.VMEM((IN_Z0, k), BF16), pltpu.VMEM((2 * C_WIDTH, k), BF16),
                        pltpu.VMEM((LANE, k), BF16), pltpu.VMEM((2, IN_CHUNK_ROWS, k), F32),
                        pltpu.SemaphoreType.DMA((2,))],
        compiler_params=_cparams(("arbitrary",)),
        name="in_proj",
    )(x, g.reshape(1, k), wt)


def _attn_kernel(sink_ref, q_ref, k_ref, v_ref, bias_ref, o_ref, kb_ref, vt_ref, p_ref, sw_ref):
    seq = q_ref.shape[0]
    nblk = seq // BLOCK
    kb_ref[...] = k_ref[...].astype(BF16)
    for g in range(A_KV_HEADS):
        for n in range(nblk):
            vt_ref[g * nblk + n] = (
                v_ref[n * BLOCK:(n + 1) * BLOCK, g * A_HEAD_DIM:(g + 1) * A_HEAD_DIM].T.astype(BF16))
    sinks = [jnp.concatenate([jnp.full((1, BLOCK), sink_ref[g * A_GROUP + h] * LOG2E, F32)
                              for h in range(A_GROUP)], axis=1) for g in range(A_KV_HEADS)]

    def tiles(first, last):
        return (1 if first else 0), (2 if last else 3)

    def scores(g, n, first, last):
        t0, t1 = tiles(first, last)
        nk = (t1 - t0) * BLOCK
        r0 = pl.multiple_of(n * BLOCK, BLOCK)
        k = kb_ref[pl.ds(pl.multiple_of(r0 + (t0 - 1) * BLOCK, BLOCK), nk),
                   g * A_HEAD_DIM:(g + 1) * A_HEAD_DIM]
        q = jnp.concatenate(
            [q_ref[pl.ds(r0, BLOCK), (g * A_GROUP + h) * A_HEAD_DIM:(g * A_GROUP + h + 1) * A_HEAD_DIM]
             for h in range(A_GROUP)], axis=0).astype(BF16)
        s = _dot_nt(k, q) * (A_HEAD_DIM ** -0.5 * LOG2E) + bias_ref[g, t0 * BLOCK:t1 * BLOCK, :]
        m = jnp.maximum(jnp.max(s, axis=0, keepdims=True), sinks[g])
        p_ref[g, 0:nk, :] = jnp.exp2(s - m).astype(BF16)
        sw_ref[g] = jnp.exp2(sinks[g] - m)

    def output(g, n, first, last):
        t0, t1 = tiles(first, last)
        nk = (t1 - t0) * BLOCK
        r0 = pl.multiple_of(n * BLOCK, BLOCK)
        v_t = jnp.concatenate([vt_ref[g * nblk + n + t - 1] for t in range(t0, t1)], axis=1)
        ones = jnp.ones((2 * SUBLANES, nk), BF16)
        o = _dot(jnp.concatenate([v_t, ones], axis=0), p_ref[g, 0:nk, :])
        denom = o[A_HEAD_DIM:A_HEAD_DIM + 1, :] + sw_ref[g]
        o = o[0:A_HEAD_DIM, :] / denom
        for h in range(A_GROUP):
            col = (g * A_GROUP + h) * A_HEAD_DIM
            o_ref[pl.ds(r0, BLOCK), col:col + A_HEAD_DIM] = (
                o[:, h * BLOCK:(h + 1) * BLOCK].T.astype(o_ref.dtype))

    def both(fn, *args):
        for g in range(A_KV_HEADS):
            fn(g, *args)

    def body(n, carry):
        both(output, n, False, False)
        both(scores, n + 1, False, False)
        return carry

    both(scores, 0, True, False)
    both(output, 0, True, False)
    both(scores, 1, False, False)
    lax.fori_loop(1, nblk - 2, body, 0)
    both(output, nblk - 2, False, False)
    both(scores, nblk - 1, False, True)
    both(output, nblk - 1, False, True)


def _windowed_attention(proj, bias_tab, sink, bsz, seq):
    qw = A_GROUP * A_HEAD_DIM
    nblk = seq // BLOCK
    return pl.pallas_call(
        _attn_kernel,
        grid=(bsz,),
        in_specs=[pl.BlockSpec(memory_space=pltpu.SMEM),
                  pl.BlockSpec((seq, A_Q), lambda b: (b, COL_AQ // A_Q)),
                  pl.BlockSpec((seq, A_KV), lambda b: (b, COL_AK // A_KV)),
                  pl.BlockSpec((seq, A_KV), lambda b: (b, COL_AV // A_KV)),
                  pl.BlockSpec(bias_tab.shape, lambda b: (0, 0, 0))],
        out_specs=pl.BlockSpec((seq, A_Q), lambda b: (b, 0)),
        out_shape=jax.ShapeDtypeStruct((bsz * seq, A_Q), BF16),
        scratch_shapes=[pltpu.VMEM((seq, A_KV), BF16),
                        pltpu.VMEM((A_KV_HEADS * nblk, A_HEAD_DIM, BLOCK), BF16),
                        pltpu.VMEM((A_KV_HEADS, 3 * BLOCK, qw), BF16),
                        pltpu.VMEM((A_KV_HEADS, 1, qw), F32)],
        compiler_params=_cparams(("parallel",)),
        name="windowed_attention",
    )(sink, proj, proj, proj, bias_tab)


def _t5_bucket(rel):
    nb = N_BUCKETS // 2
    max_exact = nb // 2
    ret = jnp.where(rel > 0, nb, 0)
    n = jnp.abs(rel)
    nf = jnp.maximum(n, 1).astype(F32)
    large = max_exact + (jnp.log(nf / max_exact) / math.log(MAX_DISTANCE / max_exact)
                         * (nb - max_exact)).astype(jnp.int32)
    large = jnp.minimum(large, nb - 1)
    return ret + jnp.where(n < max_exact, n, large)


def _attention_bias_table(rel_bias):
    kj = jnp.arange(3 * BLOCK)[:, None]
    qi = jnp.arange(BLOCK)[None, :]
    rel = kj - BLOCK - qi
    onehot = jax.nn.one_hot(_t5_bucket(rel), N_BUCKETS, dtype=F32)
    bias = jnp.einsum("kqn,nh->hkq", onehot, rel_bias.astype(F32), precision=lax.Precision.HIGHEST)
    bias = jnp.where((jnp.abs(rel) <= WINDOW)[None], bias * LOG2E, NEG_INF)
    bias = bias.reshape(A_KV_HEADS, A_GROUP, 3 * BLOCK, BLOCK)
    return jnp.transpose(bias, (0, 2, 1, 3)).reshape(A_KV_HEADS, 3 * BLOCK, A_GROUP * BLOCK)


def _gla_masks():
    c = GLA_BLOCK
    i = np.arange(c)[:, None]
    j = np.arange(c)[None, :]
    fwd = [i == j]
    for s in GLA_LEVELS:
        fwd.append((i // (2 * s) == j // (2 * s)) & ((i // s) % 2 == 1) & ((j // s) % 2 == 0))
    fwd = np.stack(fwd)
    return jnp.asarray(np.stack([fwd, fwd.transpose(0, 2, 1)]).astype(np.float32), dtype=BF16)


def _gla_block(direction, rows, q_ref, k_ref, v_ref, z_ref, w2_ref, b2_ref, m_ref, st_ref):
    c = GLA_BLOCK
    q = q_ref[rows, :] * (B_KEY_DIM ** -0.5)
    k = k_ref[rows, :]
    v = v_ref[rows, :].astype(BF16)
    gate_cols = slice(direction * B_QK, (direction + 1) * B_QK)
    pre = _dot(z_ref[rows, :].astype(BF16), w2_ref[:, gate_cols]) + b2_ref[:, gate_cols]
    log_a = -_softplus(-pre) * (LOG2E / GATE_TAU)

    row = lax.broadcasted_iota(jnp.int32, (c, B_QK), 0)
    first_head = lax.broadcasted_iota(jnp.int32, (c, LANE), 1) < B_KEY_DIM
    pair = [slice(p * LANE, (p + 1) * LANE) for p in range(GLA_PAIRS)]
    q_b = [q[:, ps].astype(BF16) for ps in pair]
    k_b = [k[:, ps].astype(BF16) for ps in pair]
    k_first = [jnp.where(first_head, kb, jnp.zeros_like(kb)) for kb in k_b]
    k_second = [jnp.where(first_head, jnp.zeros_like(kb), kb) for kb in k_b]

    def exponents(p_s, t_s):
        if direction == 0:
            return p_s, t_s - p_s
        return t_s - p_s + log_a, p_s - log_a

    def scaled(e_q, e_k, p):
        w_q = jnp.exp2(e_q[:, pair[p]]).astype(BF16)
        w_k = jnp.exp2(e_k[:, pair[p]]).astype(BF16)
        return q_b[p] * w_q, jnp.concatenate([k_first[p] * w_k, k_second[p] * w_k], axis=0)

    attn = [None] * GLA_PAIRS

    def add_level(level, q_t, k_t, p):
        mask = m_ref[direction, level]
        contrib = jnp.concatenate([mask, mask], axis=1) * _dot_nt(q_t, k_t).astype(BF16)
        attn[p] = contrib if attn[p] is None else attn[p] + contrib

    if direction == 0:
        for p in range(GLA_PAIRS):
            add_level(0, q_b[p], jnp.concatenate([k_first[p], k_second[p]], axis=0), p)
    p_s = log_a
    t_s = log_a
    for li, s in enumerate(GLA_LEVELS):
        e_q, e_k = exponents(p_s, t_s)
        for p in range(GLA_PAIRS):
            q_t, k_t = scaled(e_q, e_k, p)
            add_level(li + 1, q_t, k_t, p)
        upper = (row & s) != 0
        t_before = pltpu.roll(t_s, s, 0)
        t_after = pltpu.roll(t_s, c - s, 0)
        p_s = p_s + jnp.where(upper, t_before, 0.0)
        t_s = t_s + jnp.where(upper, t_before, t_after)

    e_q, e_k = exponents(p_s, t_s)
    state_decay = jnp.exp2(t_s[0:1, :])
    own_block = lax.broadcasted_iota(jnp.int32, (2 * B_VAL_DIM, LANE), 0) < B_VAL_DIM
    own_block = own_block == (lax.broadcasted_iota(jnp.int32, (2 * B_VAL_DIM, LANE), 1) < B_KEY_DIM)
    zero_v = jnp.zeros((c, B_VAL_DIM), BF16)
    outs = []
    for p in range(GLA_PAIRS):
        v_pair = v[:, 2 * p * B_VAL_DIM:(2 * p + 2) * B_VAL_DIM]
        v_diag = jnp.concatenate(
            [jnp.concatenate([v_pair[:, :B_VAL_DIM], zero_v], axis=1),
             jnp.concatenate([zero_v, v_pair[:, B_VAL_DIM:]], axis=1)], axis=0)
        state_t = st_ref[direction, p]
        q_t = q_b[p] * jnp.exp2(e_q[:, pair[p]]).astype(BF16)
        k_t = k_b[p] * jnp.exp2(e_k[:, pair[p]]).astype(BF16)
        o = _dot_nt(q_t, state_t.astype(BF16)) + _dot(attn[p], v_diag)
        update = jnp.where(own_block, _dot_tn(v_pair, k_t), 0.0)
        st_ref[direction, p] = state_t * state_decay[:, pair[p]] + update
        outs.append(o)
    return jnp.concatenate(outs, axis=1)


def _gla_kernel(q_ref, k_ref, v_ref, g_ref, z_ref, w2_ref, b2_ref, gn_ref, m_ref,
                o_ref, of_ref, ob_ref, st_ref):
    nblk = q_ref.shape[0] // GLA_BLOCK
    st_ref[...] = jnp.zeros(st_ref.shape, F32)
    block = functools.partial(_gla_block, q_ref=q_ref, k_ref=k_ref, v_ref=v_ref, z_ref=z_ref,
                              w2_ref=w2_ref, b2_ref=b2_ref, m_ref=m_ref, st_ref=st_ref)

    def rows_of(n):
        return pl.ds(pl.multiple_of(n * GLA_BLOCK, GLA_BLOCK), GLA_BLOCK)

    def finish(rows, o):
        normed = []
        for h in range(B_HEADS):
            oh = o[:, h * B_VAL_DIM:(h + 1) * B_VAL_DIM]
            normed.append(oh * _rms_scale(oh))
        g = g_ref[rows, :]
        out = jnp.concatenate(normed, axis=1) * gn_ref[...] * (g * jax.nn.sigmoid(g))
        o_ref[rows, :] = out.astype(o_ref.dtype)

    def first_half(n, carry):
        rows_f, rows_b = rows_of(n), rows_of(nblk - 1 - n)
        of_ref[rows_f, :] = block(0, rows_f)
        ob_ref[rows_b, :] = block(1, rows_b)
        return carry

    def second_half(n, carry):
        rows_f, rows_b = rows_of(n), rows_of(nblk - 1 - n)
        finish(rows_f, block(0, rows_f) + ob_ref[rows_f, :])
        finish(rows_b, of_ref[rows_b, :] + block(1, rows_b))
        return carry

    lax.fori_loop(0, nblk // 2, first_half, 0)
    lax.fori_loop(nblk // 2, nblk, second_half, 0)


def _gla(proj, w2, layer, b2, gn, masks, bsz, seq):
    def col(width, start):
        return pl.BlockSpec((seq, width), lambda b: (b, start // width))

    def whole(a):
        return pl.BlockSpec(a.shape, lambda b: (0,) * a.ndim)

    return pl.pallas_call(
        _gla_kernel,
        grid=(bsz,),
        in_specs=[col(B_QK, COL_BQ), col(B_QK, COL_BK), col(B_V, COL_BV), col(B_V, COL_BG),
                  col(LANE, COL_Z), pl.BlockSpec((None,) + w2.shape[1:], lambda b: (layer, 0, 0)),
                  whole(b2), whole(gn), whole(masks)],
        out_specs=pl.BlockSpec((seq, B_V), lambda b: (b, 0)),
        out_shape=jax.ShapeDtypeStruct((bsz * seq, B_V), BF16),
        scratch_shapes=[pltpu.VMEM((seq, B_V), F32), pltpu.VMEM((seq, B_V), F32),
                        pltpu.VMEM((2, GLA_PAIRS, 2 * B_VAL_DIM, LANE), F32)],
        compiler_params=_cparams(("parallel",)),
        name="gla",
    )(proj, proj, proj, proj, proj, w2, b2, gn, masks)


LRU_SEG = 256
SUBLANES = 8
LRU_PITCH = LRU_SEG + SUBLANES


def _lru_kernel(x_ref, y_ref, cw_ref, cb_ref, wa_ref, wx_ref, ba_ref, bx_ref, lam_ref, o_ref,
                xp_ref, af_ref, hf_ref, ab_ref, hb_ref):
    seq = x_ref.shape[0]
    ch = x_ref.shape[1]
    nseg = seq // LRU_SEG
    halo = SUBLANES
    xp_ref[0:halo, :] = jnp.zeros((halo, ch), F32)
    xp_ref[seq + halo:seq + 2 * halo, :] = jnp.zeros((halo, ch), F32)
    xp_ref[halo:seq + halo, :] = x_ref[...]
    decay_rate = _softplus(-lam_ref[...])
    scan_refs = ((af_ref, hf_ref), (ab_ref, hb_ref))

    def gates(c, carry):
        r0 = pl.multiple_of(c * LRU_SEG, LRU_SEG)
        dst = pl.ds(pl.multiple_of(c * LRU_PITCH, SUBLANES), LRU_SEG)
        ext = LRU_SEG + 2 * halo
        xe = xp_ref[pl.ds(r0, ext), :]
        xc = (cw_ref[0:1, :] * pltpu.roll(xe, 2, 0) + cw_ref[1:2, :] * pltpu.roll(xe, 1, 0)
              + cw_ref[2:3, :] * xe + cw_ref[3:4, :] * pltpu.roll(xe, ext - 1, 0))
        xc = xc[halo:halo + LRU_SEG] + cb_ref[...]
        xb = xc.astype(BF16)
        for d in range(2):
            r = _sigmoid(_dot(xb, wa_ref[d, 0]) + ba_ref[d:d + 1, :])
            i = _sigmoid(_dot(xb, wx_ref[d, 0]) + bx_ref[d:d + 1, :])
            log_a = -LRU_C * r * decay_rate[d:d + 1, :]
            t = jnp.tanh(log_a)
            a_ref, u_ref = scan_refs[d]
            a_ref[dst, :] = jnp.exp(log_a)
            u_ref[dst, :] = jnp.sqrt(-2.0 * t / (1.0 - t)) * (i * xc)
        return carry

    lax.fori_loop(0, nseg, gates, 0)

    def scan_step(j, carry):
        hf, pf, hb, pb = carry
        rf = pl.ds(j, nseg, stride=LRU_PITCH)
        a = af_ref[rf, :]
        hf = a * hf + hf_ref[rf, :]
        pf = a * pf
        hf_ref[rf, :] = hf
        af_ref[rf, :] = pf
        rb = pl.ds(LRU_SEG - 1 - j, nseg, stride=LRU_PITCH)
        a = ab_ref[rb, :]
        hb = a * hb + hb_ref[rb, :]
        pb = a * pb
        hb_ref[rb, :] = hb
        ab_ref[rb, :] = pb
        return hf, pf, hb, pb

    zeros = jnp.zeros((nseg, ch), F32)
    ones = jnp.ones((nseg, ch), F32)
    lax.fori_loop(0, LRU_SEG, scan_step, (zeros, ones, zeros, ones), unroll=4)

    enter_f = [None] * nseg
    state = jnp.zeros((1, ch), F32)
    for s in range(nseg):
        enter_f[s] = state
        last = s * LRU_PITCH + LRU_SEG - 1
        state = hf_ref[last:last + 1, :] + af_ref[last:last + 1, :] * state
    enter_b = [None] * nseg
    state = jnp.zeros((1, ch), F32)
    for s in reversed(range(nseg)):
        enter_b[s] = state
        first = s * LRU_PITCH
        state = hb_ref[first:first + 1, :] + ab_ref[first:first + 1, :] * state

    for s in range(nseg):
        seg = slice(s * LRU_PITCH, s * LRU_PITCH + LRU_SEG)
        rows = slice(s * LRU_SEG, (s + 1) * LRU_SEG)
        h = (hf_ref[seg, :] + af_ref[seg, :] * enter_f[s]) + (hb_ref[seg, :] + ab_ref[seg, :] * enter_b[s])
        o_ref[rows, :] = (h * jax.nn.gelu(y_ref[rows, :])).astype(o_ref.dtype)


def _rglru(proj, conv_w, conv_b, w_a, w_x, layer, b_a, b_x, lam, bsz, seq):
    ch = C_BLOCK_DIM

    def col(start):
        return pl.BlockSpec((seq, ch), lambda b, c: (b, start // ch + c))

    def per_block(rows):
        return pl.BlockSpec((rows, ch), lambda b, c: (0, c))

    gate_w = pl.BlockSpec((None, 2, 1, ch, ch), lambda b, c: (layer, 0, c, 0, 0))
    return pl.pallas_call(
        _lru_kernel,
        grid=(bsz, C_BLOCKS),
        in_specs=[col(COL_CX), col(COL_CY), per_block(CONV_WIDTH), per_block(1), gate_w, gate_w,
                  per_block(2), per_block(2), per_block(2)],
        out_specs=pl.BlockSpec((seq, ch), lambda b, c: (b, c)),
        out_shape=jax.ShapeDtypeStruct((bsz * seq, C_WIDTH), BF16),
        scratch_shapes=([pltpu.VMEM((seq + 2 * SUBLANES, ch), F32)]
                        + [pltpu.VMEM((seq // LRU_SEG * LRU_PITCH, ch), F32)] * 4),
        compiler_params=_cparams(("parallel", "parallel")),
        name="rglru",
    )(proj, proj, conv_w, conv_b.reshape(1, C_WIDTH), w_a, w_x, b_a, b_x, lam)


def _out_proj_kernel(oa_ref, ob_ref, oc_ref, w_hbm, g_ref, x_ref, o_ref, w_ref, stage_ref, sem_ref, *, layer):
    @pl.when(pl.program_id(0) == 0)
    def _():
        _fetch_weight(w_hbm, layer, w_ref, stage_ref, sem_ref)

    for r in _row_groups(x_ref.shape[0]):
        mixed = (_dot(oa_ref[r, :], w_ref[0:A_Q, :]) + _dot(ob_ref[r, :], w_ref[A_Q:A_Q + B_V, :])
                 + _dot(oc_ref[r, :], w_ref[A_Q + B_V:A_Q + B_V + C_WIDTH, :]))
        o_ref[r, :] = x_ref[r, :] + mixed * _rms_scale(mixed) * g_ref[...]


def _out_proj(oa, ob, oc, w, layer, g, x, tm):
    m, d = x.shape

    def rows(width):
        return pl.BlockSpec((tm, width), lambda i: (i, 0))

    return pl.pallas_call(
        functools.partial(_out_proj_kernel, layer=layer),
        grid=(m // tm,),
        in_specs=[rows(A_Q), rows(B_V), rows(C_WIDTH),
                  pl.BlockSpec(memory_space=pl.ANY),
                  pl.BlockSpec((1, d), lambda i: (0, 0)),
                  rows(d)],
        out_specs=rows(d),
        out_shape=jax.ShapeDtypeStruct((m, d), F32),
        scratch_shapes=_weight_scratch(w.shape[1], w.shape[2]),
        compiler_params=_cparams(("arbitrary",)),
        name="out_proj",
    )(oa, ob, oc, w, g.reshape(1, d), x)


def _cross_attn_kernel(x_ref, gpre_ref, wq_hbm, k_ref, v_ref, wo_hbm, gpost_ref, o_ref,
                       wq_ref, wo_ref, stage_ref, sem_ref, *, layer):
    @pl.when(jnp.logical_and(pl.program_id(0) == 0, pl.program_id(1) == 0))
    def _():
        _fetch_weight(wq_hbm, layer, wq_ref, stage_ref, sem_ref)
        _fetch_weight(wo_hbm, layer, wo_ref, stage_ref, sem_ref)

    x = x_ref[...]
    hn = (x * _rms_scale(x) * gpre_ref[...]).astype(BF16)
    q = _dot(hn, wq_ref[...]).astype(BF16)
    acc = None
    for h in range(X_HEADS):
        hs = slice(h * X_HEAD_DIM, (h + 1) * X_HEAD_DIM)
        s = _dot_nt(q[:, hs], k_ref[:, hs]) * (X_HEAD_DIM ** -0.5)
        m = jnp.max(s, axis=-1, keepdims=True)
        p = jnp.exp(s - m)
        o = _dot(p.astype(BF16), v_ref[:, hs]) / jnp.sum(p, axis=-1, keepdims=True)
        part = _dot(o.astype(BF16), wo_ref[hs, :])
        acc = part if acc is None else acc + part
    o_ref[...] = x + acc * _rms_scale(acc) * gpost_ref[...]


def _cross_attention(x, gpre, wq, k, v, wo, layer, gpost, bsz, seq, tq):
    d = D_MODEL
    per_row = seq // tq
    mem_spec = pl.BlockSpec((MEM_LEN, d), lambda b, i: (b, 0))
    gain_spec = pl.BlockSpec((1, d), lambda b, i: (0, 0))
    weight_spec = pl.BlockSpec(memory_space=pl.ANY)
    w_bf, stage, sem = _weight_scratch(d, d)
    row_spec = pl.BlockSpec((tq, d), lambda b, i: (b * per_row + i, 0))
    return pl.pallas_call(
        functools.partial(_cross_attn_kernel, layer=layer),
        grid=(bsz, per_row),
        in_specs=[row_spec, gain_spec, weight_spec, mem_spec, mem_spec, weight_spec, gain_spec],
        out_specs=row_spec,
        out_shape=jax.ShapeDtypeStruct((bsz * seq, d), F32),
        scratch_shapes=[w_bf, w_bf, stage, sem],
        compiler_params=_cparams(("arbitrary", "arbitrary")),
        name="cross_attention",
    )(x, gpre.reshape(1, d), wq, k, v, wo, gpost.reshape(1, d))


def _ffn_kernel(x_ref, gpre_ref, wu_ref, wd_ref, gpost_ref, o_ref, hn_ref, acc_ref):
    f = pl.program_id(1)
    last = pl.num_programs(1) - 1

    def step(is_first, is_last):
        for r in _row_groups(x_ref.shape[0]):
            if is_first:
                x = x_ref[r, :]
                hn_ref[r, :] = (x * _rms_scale(x) * gpre_ref[...]).astype(BF16)
            up = _dot(hn_ref[r, :], wu_ref[...])
            act = jnp.square(jnp.maximum(up, 0.0)).astype(BF16)
            part = _dot(act, wd_ref[...])
            ff = part if is_first else acc_ref[r, :] + part
            if is_last:
                o_ref[r, :] = x_ref[r, :] + ff * _rms_scale(ff) * gpost_ref[...]
            else:
                acc_ref[r, :] = ff

    pl.when(f == 0)(functools.partial(step, True, False))
    pl.when(jnp.logical_and(f > 0, f < last))(functools.partial(step, False, False))
    pl.when(f == last)(functools.partial(step, False, True))


def _ffn_rest(x, gpre, w_up, w_down, gpost, tm, tf):
    m, d = x.shape
    dff = w_up.shape[1]
    return pl.pallas_call(
        _ffn_kernel,
        grid=(m // tm - 1, dff // tf),
        in_specs=[pl.BlockSpec((tm, d), lambda i, f: (i + 1, 0)),
                  pl.BlockSpec((1, d), lambda i, f: (0, 0)),
                  pl.BlockSpec((d, tf), lambda i, f: (0, f)),
                  pl.BlockSpec((tf, d), lambda i, f: (f, 0)),
                  pl.BlockSpec((1, d), lambda i, f: (0, 0))],
        out_specs=pl.BlockSpec((tm, d), lambda i, f: (i + 1, 0)),
        out_shape=jax.ShapeDtypeStruct((m, d), F32),
        input_output_aliases={0: 0},
        scratch_shapes=[pltpu.VMEM((tm, d), BF16), pltpu.VMEM((tm, d), F32)],
        compiler_params=_cparams(("parallel", "arbitrary")),
        name="ffn",
    )(x, gpre.reshape(1, d), w_up, w_down, gpost.reshape(1, d))


def _ffn_first_kernel(x_ref, gpre_ref, wu_ref, wd_ref, gpost_ref, o_ref, wub_ref, wdb_ref, hn_ref, acc_ref):
    f = pl.program_id(0)
    wu = wu_ref[...].astype(BF16)
    wd = wd_ref[...].astype(BF16)
    wub_ref[...] = wu
    wdb_ref[...] = wd

    @pl.when(f == 0)
    def _():
        x = x_ref[...]
        hn_ref[...] = (x * _rms_scale(x) * gpre_ref[...]).astype(BF16)
        acc_ref[...] = jnp.zeros(acc_ref.shape, F32)

    up = _dot(hn_ref[...], wu)
    act = jnp.square(jnp.maximum(up, 0.0)).astype(BF16)
    acc_ref[...] += _dot(act, wd)

    @pl.when(f == pl.num_programs(0) - 1)
    def _():
        ff = acc_ref[...]
        o_ref[...] = x_ref[...] + ff * _rms_scale(ff) * gpost_ref[...]


def _ffn_first(x, gpre, w_up, w_down, layer, gpost, tm, tf):
    m, d = x.shape
    dff = w_up.shape[2]
    return pl.pallas_call(
        _ffn_first_kernel,
        grid=(dff // tf,),
        in_specs=[pl.BlockSpec((tm, d), lambda f: (0, 0)),
                  pl.BlockSpec((1, d), lambda f: (0, 0)),
                  pl.BlockSpec((None, d, tf), lambda f: (layer, 0, f)),
                  pl.BlockSpec((None, tf, d), lambda f: (layer, f, 0)),
                  pl.BlockSpec((1, d), lambda f: (0, 0))],
        out_specs=[pl.BlockSpec((tm, d), lambda f: (0, 0)),
                   pl.BlockSpec((d, tf), lambda f: (0, f)),
                   pl.BlockSpec((tf, d), lambda f: (f, 0))],
        out_shape=[jax.ShapeDtypeStruct((tm, d), F32), jax.ShapeDtypeStruct((d, dff), BF16),
                   jax.ShapeDtypeStruct((dff, d), BF16)],
        scratch_shapes=[pltpu.VMEM((tm, d), BF16), pltpu.VMEM((tm, d), F32)],
        compiler_params=_cparams(("arbitrary",)),
        name="ffn_first",
    )(x, gpre.reshape(1, d), w_up, w_down, gpost.reshape(1, d))


def _ffn(x, gpre, w_up, w_down, layer, gpost, tm):
    first, w_up_b, w_down_b = _ffn_first(x, gpre, w_up, w_down, layer, gpost, tm, tf=512)
    rest = _ffn_rest(x, gpre, w_up_b, w_down_b, gpost, tm, tf=1024)
    return lax.dynamic_update_slice(rest, first, (0, 0))


def _gate_expansion(w2_f, w2_b):
    w = jnp.zeros((w2_f.shape[0], LANE, 2 * B_QK), F32)
    w = w.at[:, 0:GATE_RANK, 0:B_QK].set(w2_f)
    w = w.at[:, GATE_RANK:2 * GATE_RANK, B_QK:].set(w2_b)
    return w.astype(BF16)


def kernel(x, mem, rel_bias, w_in, w_out, attn_sink, gla_w2_f, gla_b2_f, gla_w2_b, gla_b2_b, gla_norm, conv_w, conv_b, lru_wa, lru_ba, lru_wx, lru_bx, lru_lambda, xq, xk, xv, xo, w_up, w_down, norm_mix_pre, norm_mix_post, norm_mem, norm_x_pre, norm_x_post, norm_ff_pre, norm_ff_post):
    bsz, seq, d = x.shape
    assert d == D_MODEL and seq == SUBLANES * LRU_SEG
    xs = x.reshape(bsz * seq, d)
    mems = mem.reshape(bsz * mem.shape[1], d)
    bias_tab = _attention_bias_table(rel_bias)
    masks = _gla_masks()
    w2_b = _gate_expansion(gla_w2_f, gla_w2_b)
    b2 = jnp.concatenate([gla_b2_f, gla_b2_b], axis=-1)
    bf = lambda w: w.astype(BF16)
    wa_b, wx_b = bf(lru_wa), bf(lru_wx)
    for l in range(DEPTH):
        proj = _in_proj(xs, norm_mix_pre[l], w_in, l, tm=512)
        oa = _windowed_attention(proj, bias_tab, attn_sink[l], bsz, seq)
        ob = _gla(proj, w2_b, l, b2[l].reshape(1, 2 * B_QK), gla_norm[l].reshape(1, B_V), masks, bsz, seq)
        oc = _rglru(proj, conv_w[l], conv_b[l], wa_b, wx_b, l, lru_ba[l], lru_bx[l], lru_lambda[l],
                    bsz, seq)
        xs = _out_proj(oa, ob, oc, w_out, l, norm_mix_post[l], xs, tm=512)

        k, v = _kv_proj(mems, norm_mem[l], xk, xv, l, tm=512)
        xs = _cross_attention(xs, norm_x_pre[l], xq, k, v, xo, l, norm_x_post[l], bsz, seq, tq=512)

        xs = _ffn(xs, norm_ff_pre[l], w_up, w_down, l, norm_ff_post[l], tm=512)
    return xs.reshape(bsz, seq, d)
```

```python
import functools
import math

import numpy as np
import jax
import jax.numpy as jnp
from jax import lax
from jax.experimental import pallas as pl
from jax.experimental.pallas import tpu as pltpu

F32 = jnp.float32
BF16 = jnp.bfloat16

D_MODEL = 2048
DEPTH = 4
MEM_LEN = 256
A_HEAD_DIM = 128
A_HEADS = 8
A_KV_HEADS = 2
A_GROUP = 4
WINDOW = 128
BLOCK = 128
N_BUCKETS = 32
MAX_DISTANCE = 128
B_HEADS = 4
B_KEY_DIM = 64
B_VAL_DIM = 128
GATE_RANK = 16
GATE_TAU = 16.0
C_WIDTH = 512
C_BLOCKS = 4
C_BLOCK_DIM = 128
CONV_WIDTH = 4
LRU_C = 8.0
X_HEADS = 4
X_HEAD_DIM = 512
D_FF = 8192
EPS = 1e-6
NEG_INF = -1e30
LOG2E = math.log2(math.e)

A_Q = A_HEADS * A_HEAD_DIM
A_KV = A_KV_HEADS * A_HEAD_DIM
B_QK = B_HEADS * B_KEY_DIM
B_V = B_HEADS * B_VAL_DIM

LANE = 128
COL_AQ = 0
COL_AK = COL_AQ + A_Q
COL_AV = COL_AK + A_KV
COL_BQ = COL_AV + A_KV
COL_BK = COL_BQ + B_QK
COL_BV = COL_BK + B_QK
COL_BG = COL_BV + B_V
COL_CX = COL_BG + B_V
COL_CY = COL_CX + C_WIDTH
COL_Z = COL_CY + C_WIDTH
D_PROJ = COL_Z + LANE

GLA_BLOCK = 128
GLA_LEVELS = (1, 2, 4, 8, 16, 32, 64)
GLA_PAIRS = B_HEADS // 2
VMEM_LIMIT_MIB = 56


def _cparams(semantics, vmem_mib=VMEM_LIMIT_MIB):
    return pltpu.CompilerParams(dimension_semantics=semantics,
                                vmem_limit_bytes=vmem_mib * 1024 * 1024)


def _dot(a, b):
    return jnp.dot(a, b, preferred_element_type=F32)


def _dot_nt(a, b):
    return lax.dot_general(a, b, (((1,), (1,)), ((), ())), preferred_element_type=F32)


def _dot_tn(a, b):
    return lax.dot_general(a, b, (((0,), (0,)), ((), ())), preferred_element_type=F32)


def _softplus(x):
    return jnp.maximum(x, 0.0) + jnp.log(1.0 + jnp.exp(-jnp.abs(x)))


def _sigmoid(x):
    return 0.5 * jnp.tanh(0.5 * x) + 0.5


def _rms_scale(x):
    return lax.rsqrt(jnp.mean(jnp.square(x), axis=-1, keepdims=True) + EPS)


ROW_CHAINS = 2


def _row_groups(rows):
    step = rows // ROW_CHAINS
    return [slice(c * step, (c + 1) * step) for c in range(ROW_CHAINS)]


WEIGHT_CHUNK_ROWS = 512


def _weight_scratch(k, n):
    return [pltpu.VMEM((k, n), BF16), pltpu.VMEM((2, WEIGHT_CHUNK_ROWS, n), F32),
            pltpu.SemaphoreType.DMA((2,))]


def _fetch_weight(w_hbm, layer, dst_ref, stage_ref, sem_ref):
    nchunk = dst_ref.shape[0] // WEIGHT_CHUNK_ROWS

    def copy(c):
        rows = pl.ds(c * WEIGHT_CHUNK_ROWS, WEIGHT_CHUNK_ROWS)
        return pltpu.make_async_copy(w_hbm.at[layer, rows, :], stage_ref.at[c % 2], sem_ref.at[c % 2])

    copy(0).start()
    for c in range(nchunk):
        if c + 1 < nchunk:
            copy(c + 1).start()
        copy(c).wait()
        dst_ref[c * WEIGHT_CHUNK_ROWS:(c + 1) * WEIGHT_CHUNK_ROWS, :] = stage_ref[c % 2].astype(BF16)


def _kv_proj_kernel(x_ref, g_ref, wk_hbm, wv_hbm, k_ref, v_ref, wk_ref, wv_ref, stage_ref, sem_ref, *, layer):
    @pl.when(pl.program_id(0) == 0)
    def _():
        _fetch_weight(wk_hbm, layer, wk_ref, stage_ref, sem_ref)
        _fetch_weight(wv_hbm, layer, wv_ref, stage_ref, sem_ref)

    x = x_ref[...]
    hn = (x * _rms_scale(x) * g_ref[...]).astype(BF16)
    k_ref[...] = _dot(hn, wk_ref[...]).astype(BF16)
    v_ref[...] = _dot(hn, wv_ref[...]).astype(BF16)


def _kv_proj(x, g, wk, wv, layer, tm):
    m, d = x.shape
    weight = pl.BlockSpec(memory_space=pl.ANY)
    rows = pl.BlockSpec((tm, d), lambda i: (i, 0))
    w_bf, stage, sem = _weight_scratch(d, d)
    return pl.pallas_call(
        functools.partial(_kv_proj_kernel, layer=layer),
        grid=(m // tm,),
        in_specs=[rows, pl.BlockSpec((1, d), lambda i: (0, 0)), weight, weight],
        out_specs=[rows, rows],
        out_shape=[jax.ShapeDtypeStruct((m, d), BF16)] * 2,
        scratch_shapes=[w_bf, w_bf, stage, sem],
        compiler_params=_cparams(("arbitrary",)),
        name="kv_proj",
    )(x, g.reshape(1, d), wk, wv)


IN_Z0 = A_Q + 2 * A_KV + 2 * B_QK + 2 * B_V
IN_Z1 = IN_Z0 + 2 * GATE_RANK
D_IN = IN_Z1 + 2 * C_WIDTH
IN_CHUNK_ROWS = 512


def _fetch_w_in(wt_hbm, layer, wm_ref, wc_ref, wz_ref, stage_ref, sem_ref):
    wz_ref[...] = jnp.zeros(wz_ref.shape, BF16)
    pieces = []
    for start, stop, dst in ((0, IN_Z0, wm_ref), (IN_Z0, IN_Z1, wz_ref), (IN_Z1, D_IN, wc_ref)):
        for r in range(start, stop, IN_CHUNK_ROWS):
            pieces.append((r, min(IN_CHUNK_ROWS, stop - r), dst, r - start))

    def copy(c):
        src_row, rows, _, _ = pieces[c]
        return pltpu.make_async_copy(wt_hbm.at[layer, pl.ds(src_row, rows), :],
                                     stage_ref.at[c % 2, pl.ds(0, rows), :], sem_ref.at[c % 2])

    copy(0).start()
    for c, (_, rows, dst, dst_row) in enumerate(pieces):
        if c + 1 < len(pieces):
            copy(c + 1).start()
        copy(c).wait()
        dst[dst_row:dst_row + rows, :] = stage_ref[c % 2, 0:rows, :].astype(BF16)


def _in_proj_kernel(x_ref, g_ref, wt_hbm, o_ref, wm_ref, wc_ref, wz_ref, stage_ref, sem_ref, *, layer):
    @pl.when(pl.program_id(0) == 0)
    def _():
        _fetch_w_in(wt_hbm, layer, wm_ref, wc_ref, wz_ref, stage_ref, sem_ref)

    for r in _row_groups(x_ref.shape[0]):
        x = x_ref[r, :]
        hn = (x * _rms_scale(x) * g_ref[...]).astype(BF16)
        o_ref[r, 0:COL_CX] = _dot_nt(hn, wm_ref[...])
        o_ref[r, COL_CX:COL_Z] = _dot_nt(hn, wc_ref[...])
        o_ref[r, COL_Z:D_PROJ] = _dot_nt(hn, wz_ref[...])


def _in_proj(x, g, w_in, layer, tm):
    m, k = x.shape
    assert w_in.shape[1:] == (k, D_IN)
    wt = jnp.swapaxes(w_in, 1, 2)
    return pl.pallas_call(
        functools.partial(_in_proj_kernel, layer=layer),
        grid=(m // tm,),
        in_specs=[pl.BlockSpec((tm, k), lambda i: (i, 0)), pl.BlockSpec((1, k), lambda i: (0, 0)),
                  pl.BlockSpec(memory_space=pl.ANY)],
        out_specs=pl.BlockSpec((tm, D_PROJ), lambda i: (i, 0)),
        out_shape=jax.ShapeDtypeStruct((m, D_PROJ), F32),
        scratch_shapes=[pltpu.VMEM((IN_Z0, k), BF16), pltpu.VMEM((2 * C_WIDTH, k), BF16),
                        pltpu.VMEM((LANE, k), BF16), pltpu.VMEM((2, IN_CHUNK_ROWS, k), F32),
                        pltpu.SemaphoreType.DMA((2,))],
        compiler_params=_cparams(("arbitrary",)),
        name="in_proj",
    )(x, g.reshape(1, k), wt)


def _attn_kernel(sink_ref, q_ref, k_ref, v_ref, bias_ref, o_ref, kb_ref, vt_ref, p_ref, sw_ref):
    seq = q_ref.shape[0]
    nblk = seq // BLOCK
    kb_ref[...] = k_ref[...].astype(BF16)
    for g in range(A_KV_HEADS):
        for n in range(nblk):
            vt_ref[g * nblk + n] = (
                v_ref[n * BLOCK:(n + 1) * BLOCK, g * A_HEAD_DIM:(g + 1) * A_HEAD_DIM].T.astype(BF16))
    sinks = [jnp.concatenate([jnp.full((1, BLOCK), sink_ref[g * A_GROUP + h] * LOG2E, F32)
                              for h in range(A_GROUP)], axis=1) for g in range(A_KV_HEADS)]

    def tiles(first, last):
        return (1 if first else 0), (2 if last else 3)

    def scores(g, n, first, last):
        t0, t1 = tiles(first, last)
        nk = (t1 - t0) * BLOCK
        r0 = pl.multiple_of(n * BLOCK, BLOCK)
        k = kb_ref[pl.ds(pl.multiple_of(r0 + (t0 - 1) * BLOCK, BLOCK), nk),
                   g * A_HEAD_DIM:(g + 1) * A_HEAD_DIM]
        q = jnp.concatenate(
            [q_ref[pl.ds(r0, BLOCK), (g * A_GROUP + h) * A_HEAD_DIM:(g * A_GROUP + h + 1) * A_HEAD_DIM]
             for h in range(A_GROUP)], axis=0).astype(BF16)
        s = _dot_nt(k, q) * (A_HEAD_DIM ** -0.5 * LOG2E) + bias_ref[g, t0 * BLOCK:t1 * BLOCK, :]
        m = jnp.maximum(jnp.max(s, axis=0, keepdims=True), sinks[g])
        p_ref[g, 0:nk, :] = jnp.exp2(s - m).astype(BF16)
        sw_ref[g] = jnp.exp2(sinks[g] - m)

    def output(g, n, first, last):
        t0, t1 = tiles(first, last)
        nk = (t1 - t0) * BLOCK
        r0 = pl.multiple_of(n * BLOCK, BLOCK)
        v_t = jnp.concatenate([vt_ref[g * nblk + n + t - 1] for t in range(t0, t1)], axis=1)
        ones = jnp.ones((2 * SUBLANES, nk), BF16)
        o = _dot(jnp.concatenate([v_t, ones], axis=0), p_ref[g, 0:nk, :])
        denom = o[A_HEAD_DIM:A_HEAD_DIM + 1, :] + sw_ref[g]
        o = o[0:A_HEAD_DIM, :] / denom
        for h in range(A_GROUP):
            col = (g * A_GROUP + h) * A_HEAD_DIM
            o_ref[pl.ds(r0, BLOCK), col:col + A_HEAD_DIM] = (
                o[:, h * BLOCK:(h + 1) * BLOCK].T.astype(o_ref.dtype))

    def both(fn, *args):
        for g in range(A_KV_HEADS):
            fn(g, *args)

    def body(n, carry):
        both(output, n, False, False)
        both(scores, n + 1, False, False)
        return carry

    both(scores, 0, True, False)
    both(output, 0, True, False)
    both(scores, 1, False, False)
    lax.fori_loop(1, nblk - 2, body, 0)
    both(output, nblk - 2, False, False)
    both(scores, nblk - 1, False, True)
    both(output, nblk - 1, False, True)


def _windowed_attention(proj, bias_tab, sink, bsz, seq):
    qw = A_GROUP * A_HEAD_DIM
    nblk = seq // BLOCK
    return pl.pallas_call(
        _attn_kernel,
        grid=(bsz,),
        in_specs=[pl.BlockSpec(memory_space=pltpu.SMEM),
                  pl.BlockSpec((seq, A_Q), lambda b: (b, COL_AQ // A_Q)),
                  pl.BlockSpec((seq, A_KV), lambda b: (b, COL_AK // A_KV)),
                  pl.BlockSpec((seq, A_KV), lambda b: (b, COL_AV // A_KV)),
                  pl.BlockSpec(bias_tab.shape, lambda b: (0, 0, 0))],
        out_specs=pl.BlockSpec((seq, A_Q), lambda b: (b, 0)),
        out_shape=jax.ShapeDtypeStruct((bsz * seq, A_Q), BF16),
        scratch_shapes=[pltpu.VMEM((seq, A_KV), BF16),
                        pltpu.VMEM((A_KV_HEADS * nblk, A_HEAD_DIM, BLOCK), BF16),
                        pltpu.VMEM((A_KV_HEADS, 3 * BLOCK, qw), BF16),
                        pltpu.VMEM((A_KV_HEADS, 1, qw), F32)],
        compiler_params=_cparams(("parallel",)),
        name="windowed_attention",
    )(sink, proj, proj, proj, bias_tab)


def _t5_bucket(rel):
    nb = N_BUCKETS // 2
    max_exact = nb // 2
    ret = jnp.where(rel > 0, nb, 0)
    n = jnp.abs(rel)
    nf = jnp.maximum(n, 1).astype(F32)
    large = max_exact + (jnp.log(nf / max_exact) / math.log(MAX_DISTANCE / max_exact)
                         * (nb - max_exact)).astype(jnp.int32)
    large = jnp.minimum(large, nb - 1)
    return ret + jnp.where(n < max_exact, n, large)


def _attention_bias_table(rel_bias):
    kj = jnp.arange(3 * BLOCK)[:, None]
    qi = jnp.arange(BLOCK)[None, :]
    rel = kj - BLOCK - qi
    onehot = jax.nn.one_hot(_t5_bucket(rel), N_BUCKETS, dtype=F32)
    bias = jnp.einsum("kqn,nh->hkq", onehot, rel_bias.astype(F32), precision=lax.Precision.HIGHEST)
    bias = jnp.where((jnp.abs(rel) <= WINDOW)[None], bias * LOG2E, NEG_INF)
    bias = bias.reshape(A_KV_HEADS, A_GROUP, 3 * BLOCK, BLOCK)
    return jnp.transpose(bias, (0, 2, 1, 3)).reshape(A_KV_HEADS, 3 * BLOCK, A_GROUP * BLOCK)


def _gla_masks():
    c = GLA_BLOCK
    i = np.arange(c)[:, None]
    j = np.arange(c)[None, :]
    fwd = [i == j]
    for s in GLA_LEVELS:
        fwd.append((i // (2 * s) == j // (2 * s)) & ((i // s) % 2 == 1) & ((j // s) % 2 == 0))
    fwd = np.stack(fwd)
    return jnp.asarray(np.stack([fwd, fwd.transpose(0, 2, 1)]).astype(np.float32), dtype=BF16)


def _gla_block(direction, rows, q_ref, k_ref, v_ref, z_ref, w2_ref, b2_ref, m_ref, st_ref):
    c = GLA_BLOCK
    q = q_ref[rows, :] * (B_KEY_DIM ** -0.5)
    k = k_ref[rows, :]
    v = v_ref[rows, :].astype(BF16)
    gate_cols = slice(direction * B_QK, (direction + 1) * B_QK)
    pre = _dot(z_ref[rows, :].astype(BF16), w2_ref[:, gate_cols]) + b2_ref[:, gate_cols]
    log_a = -_softplus(-pre) * (LOG2E / GATE_TAU)

    row = lax.broadcasted_iota(jnp.int32, (c, B_QK), 0)
    first_head = lax.broadcasted_iota(jnp.int32, (c, LANE), 1) < B_KEY_DIM
    pair = [slice(p * LANE, (p + 1) * LANE) for p in range(GLA_PAIRS)]
    q_b = [q[:, ps].astype(BF16) for ps in pair]
    k_b = [k[:, ps].astype(BF16) for ps in pair]
    k_first = [jnp.where(first_head, kb, jnp.zeros_like(kb)) for kb in k_b]
    k_second = [jnp.where(first_head, jnp.zeros_like(kb), kb) for kb in k_b]

    def exponents(p_s, t_s):
        if direction == 0:
            return p_s, t_s - p_s
        return t_s - p_s + log_a, p_s - log_a

    def scaled(e_q, e_k, p):
        w_q = jnp.exp2(e_q[:, pair[p]]).astype(BF16)
        w_k = jnp.exp2(e_k[:, pair[p]]).astype(BF16)
        return q_b[p] * w_q, jnp.concatenate([k_first[p] * w_k, k_second[p] * w_k], axis=0)

    attn = [None] * GLA_PAIRS

    def add_level(level, q_t, k_t, p):
        mask = m_ref[direction, level]
        contrib = jnp.concatenate([mask, mask], axis=1) * _dot_nt(q_t, k_t).astype(BF16)
        attn[p] = contrib if attn[p] is None else attn[p] + contrib

    if direction == 0:
        for p in range(GLA_PAIRS):
            add_level(0, q_b[p], jnp.concatenate([k_first[p], k_second[p]], axis=0), p)
    p_s = log_a
    t_s = log_a
    for li, s in enumerate(GLA_LEVELS):
        e_q, e_k = exponents(p_s, t_s)
        for p in range(GLA_PAIRS):
            q_t, k_t = scaled(e_q, e_k, p)
            add_level(li + 1, q_t, k_t, p)
        upper = (row & s) != 0
        t_before = pltpu.roll(t_s, s, 0)
        t_after = pltpu.roll(t_s, c - s, 0)
        p_s = p_s + jnp.where(upper, t_before, 0.0)
        t_s = t_s + jnp.where(upper, t_before, t_after)

    e_q, e_k = exponents(p_s, t_s)
    state_decay = jnp.exp2(t_s[0:1, :])
    own_block = lax.broadcasted_iota(jnp.int32, (2 * B_VAL_DIM, LANE), 0) < B_VAL_DIM
    own_block = own_block == (lax.broadcasted_iota(jnp.int32, (2 * B_VAL_DIM, LANE), 1) < B_KEY_DIM)
    zero_v = jnp.zeros((c, B_VAL_DIM), BF16)
    outs = []
    for p in range(GLA_PAIRS):
        v_pair = v[:, 2 * p * B_VAL_DIM:(2 * p + 2) * B_VAL_DIM]
        v_diag = jnp.concatenate(
            [jnp.concatenate([v_pair[:, :B_VAL_DIM], zero_v], axis=1),
             jnp.concatenate([zero_v, v_pair[:, B_VAL_DIM:]], axis=1)], axis=0)
        state_t = st_ref[direction, p]
        q_t = q_b[p] * jnp.exp2(e_q[:, pair[p]]).astype(BF16)
        k_t = k_b[p] * jnp.exp2(e_k[:, pair[p]]).astype(BF16)
        o = _dot_nt(q_t, state_t.astype(BF16)) + _dot(attn[p], v_diag)
        update = jnp.where(own_block, _dot_tn(v_pair, k_t), 0.0)
        st_ref[direction, p] = state_t * state_decay[:, pair[p]] + update
        outs.append(o)
    return jnp.concatenate(outs, axis=1)


def _gla_kernel(q_ref, k_ref, v_ref, g_ref, z_ref, w2_ref, b2_ref, gn_ref, m_ref,
                o_ref, of_ref, ob_ref, st_ref):
    nblk = q_ref.shape[0] // GLA_BLOCK
    st_ref[...] = jnp.zeros(st_ref.shape, F32)
    block = functools.partial(_gla_block, q_ref=q_ref, k_ref=k_ref, v_ref=v_ref, z_ref=z_ref,
                              w2_ref=w2_ref, b2_ref=b2_ref, m_ref=m_ref, st_ref=st_ref)

    def rows_of(n):
        return pl.ds(pl.multiple_of(n * GLA_BLOCK, GLA_BLOCK), GLA_BLOCK)

    def finish(rows, o):
        normed = []
        for h in range(B_HEADS):
            oh = o[:, h * B_VAL_DIM:(h + 1) * B_VAL_DIM]
            normed.append(oh * _rms_scale(oh))
        g = g_ref[rows, :]
        out = jnp.concatenate(normed, axis=1) * gn_ref[...] * (g * jax.nn.sigmoid(g))
        o_ref[rows, :] = out.astype(o_ref.dtype)

    def first_half(n, carry):
        rows_f, rows_b = rows_of(n), rows_of(nblk - 1 - n)
        of_ref[rows_f, :] = block(0, rows_f)
        ob_ref[rows_b, :] = block(1, rows_b)
        return carry

    def second_half(n, carry):
        rows_f, rows_b = rows_of(n), rows_of(nblk - 1 - n)
        finish(rows_f, block(0, rows_f) + ob_ref[rows_f, :])
        finish(rows_b, of_ref[rows_b, :] + block(1, rows_b))
        return carry

    lax.fori_loop(0, nblk // 2, first_half, 0)
    lax.fori_loop(nblk // 2, nblk, second_half, 0)


def _gla(proj, w2, layer, b2, gn, masks, bsz, seq):
    def col(width, start):
        return pl.BlockSpec((seq, width), lambda b: (b, start // width))

    def whole(a):
        return pl.BlockSpec(a.shape, lambda b: (0,) * a.ndim)

    return pl.pallas_call(
        _gla_kernel,
        grid=(bsz,),
        in_specs=[col(B_QK, COL_BQ), col(B_QK, COL_BK), col(B_V, COL_BV), col(B_V, COL_BG),
                  col(LANE, COL_Z), pl.BlockSpec((None,) + w2.shape[1:], lambda b: (layer, 0, 0)),
                  whole(b2), whole(gn), whole(masks)],
        out_specs=pl.BlockSpec((seq, B_V), lambda b: (b, 0)),
        out_shape=jax.ShapeDtypeStruct((bsz * seq, B_V), BF16),
        scratch_shapes=[pltpu.VMEM((seq, B_V), F32), pltpu.VMEM((seq, B_V), F32),
                        pltpu.VMEM((2, GLA_PAIRS, 2 * B_VAL_DIM, LANE), F32)],
        compiler_params=_cparams(("parallel",)),
        name="gla",
    )(proj, proj, proj, proj, proj, w2, b2, gn, masks)


LRU_SEG = 256
SUBLANES = 8
LRU_PITCH = LRU_SEG + SUBLANES


def _lru_kernel(x_ref, y_ref, cw_ref, cb_ref, wa_ref, wx_ref, ba_ref, bx_ref, lam_ref, o_ref,
                xp_ref, af_ref, hf_ref, ab_ref, hb_ref):
    seq = x_ref.shape[0]
    ch = x_ref.shape[1]
    nseg = seq // LRU_SEG
    halo = SUBLANES
    xp_ref[0:halo, :] = jnp.zeros((halo, ch), F32)
    xp_ref[seq + halo:seq + 2 * halo, :] = jnp.zeros((halo, ch), F32)
    xp_ref[halo:seq + halo, :] = x_ref[...]
    decay_rate = _softplus(-lam_ref[...])
    scan_refs = ((af_ref, hf_ref), (ab_ref, hb_ref))

    def gates(c, carry):
        r0 = pl.multiple_of(c * LRU_SEG, LRU_SEG)
        dst = pl.ds(pl.multiple_of(c * LRU_PITCH, SUBLANES), LRU_SEG)
        ext = LRU_SEG + 2 * halo
        xe = xp_ref[pl.ds(r0, ext), :]
        xc = (cw_ref[0:1, :] * pltpu.roll(xe, 2, 0) + cw_ref[1:2, :] * pltpu.roll(xe, 1, 0)
              + cw_ref[2:3, :] * xe + cw_ref[3:4, :] * pltpu.roll(xe, ext - 1, 0))
        xc = xc[halo:halo + LRU_SEG] + cb_ref[...]
        xb = xc.astype(BF16)
        for d in range(2):
            r = _sigmoid(_dot(xb, wa_ref[d, 0]) + ba_ref[d:d + 1, :])
            i = _sigmoid(_dot(xb, wx_ref[d, 0]) + bx_ref[d:d + 1, :])
            log_a = -LRU_C * r * decay_rate[d:d + 1, :]
            t = jnp.tanh(log_a)
            a_ref, u_ref = scan_refs[d]
            a_ref[dst, :] = jnp.exp(log_a)
            u_ref[dst, :] = jnp.sqrt(-2.0 * t / (1.0 - t)) * (i * xc)
        return carry

    lax.fori_loop(0, nseg, gates, 0)

    def scan_step(j, carry):
        hf, pf, hb, pb = carry
        rf = pl.ds(j, nseg, stride=LRU_PITCH)
        a = af_ref[rf, :]
        hf = a * hf + hf_ref[rf, :]
        pf = a * pf
        hf_ref[rf, :] = hf
        af_ref[rf, :] = pf
        rb = pl.ds(LRU_SEG - 1 - j, nseg, stride=LRU_PITCH)
        a = ab_ref[rb, :]
        hb = a * hb + hb_ref[rb, :]
        pb = a * pb
        hb_ref[rb, :] = hb
        ab_ref[rb, :] = pb
        return hf, pf, hb, pb

    zeros = jnp.zeros((nseg, ch), F32)
    ones = jnp.ones((nseg, ch), F32)
    lax.fori_loop(0, LRU_SEG, scan_step, (zeros, ones, zeros, ones), unroll=4)

    enter_f = [None] * nseg
    state = jnp.zeros((1, ch), F32)
    for s in range(nseg):
        enter_f[s] = state
        last = s * LRU_PITCH + LRU_SEG - 1
        state = hf_ref[last:last + 1, :] + af_ref[last:last + 1, :] * state
    enter_b = [None] * nseg
    state = jnp.zeros((1, ch), F32)
    for s in reversed(range(nseg)):
        enter_b[s] = state
        first = s * LRU_PITCH
        state = hb_ref[first:first + 1, :] + ab_ref[first:first + 1, :] * state

    for s in range(nseg):
        seg = slice(s * LRU_PITCH, s * LRU_PITCH + LRU_SEG)
        rows = slice(s * LRU_SEG, (s + 1) * LRU_SEG)
        h = (hf_ref[seg, :] + af_ref[seg, :] * enter_f[s]) + (hb_ref[seg, :] + ab_ref[seg, :] * enter_b[s])
        o_ref[rows, :] = (h * jax.nn.gelu(y_ref[rows, :])).astype(o_ref.dtype)


def _rglru(proj, conv_w, conv_b, w_a, w_x, layer, b_a, b_x, lam, bsz, seq):
    ch = C_BLOCK_DIM

    def col(start):
        return pl.BlockSpec((seq, ch), lambda b, c: (b, start // ch + c))

    def per_block(rows):
        return pl.BlockSpec((rows, ch), lambda b, c: (0, c))

    gate_w = pl.BlockSpec((None, 2, 1, ch, ch), lambda b, c: (layer, 0, c, 0, 0))
    return pl.pallas_call(
        _lru_kernel,
        grid=(bsz, C_BLOCKS),
        in_specs=[col(COL_CX), col(COL_CY), per_block(CONV_WIDTH), per_block(1), gate_w, gate_w,
                  per_block(2), per_block(2), per_block(2)],
        out_specs=pl.BlockSpec((seq, ch), lambda b, c: (b, c)),
        out_shape=jax.ShapeDtypeStruct((bsz * seq, C_WIDTH), BF16),
        scratch_shapes=([pltpu.VMEM((seq + 2 * SUBLANES, ch), F32)]
                        + [pltpu.VMEM((seq // LRU_SEG * LRU_PITCH, ch), F32)] * 4),
        compiler_params=_cparams(("parallel", "parallel")),
        name="rglru",
    )(proj, proj, conv_w, conv_b.reshape(1, C_WIDTH), w_a, w_x, b_a, b_x, lam)


def _out_proj_kernel(oa_ref, ob_ref, oc_ref, w_hbm, g_ref, x_ref, o_ref, w_ref, stage_ref, sem_ref, *, layer):
    @pl.when(pl.program_id(0) == 0)
    def _():
        _fetch_weight(w_hbm, layer, w_ref, stage_ref, sem_ref)

    for r in _row_groups(x_ref.shape[0]):
        mixed = (_dot(oa_ref[r, :], w_ref[0:A_Q, :]) + _dot(ob_ref[r, :], w_ref[A_Q:A_Q + B_V, :])
                 + _dot(oc_ref[r, :], w_ref[A_Q + B_V:A_Q + B_V + C_WIDTH, :]))
        o_ref[r, :] = x_ref[r, :] + mixed * _rms_scale(mixed) * g_ref[...]


def _out_proj(oa, ob, oc, w, layer, g, x, tm):
    m, d = x.shape

    def rows(width):
        return pl.BlockSpec((tm, width), lambda i: (i, 0))

    return pl.pallas_call(
        functools.partial(_out_proj_kernel, layer=layer),
        grid=(m // tm,),
        in_specs=[rows(A_Q), rows(B_V), rows(C_WIDTH),
                  pl.BlockSpec(memory_space=pl.ANY),
                  pl.BlockSpec((1, d), lambda i: (0, 0)),
                  rows(d)],
        out_specs=rows(d),
        out_shape=jax.ShapeDtypeStruct((m, d), F32),
        scratch_shapes=_weight_scratch(w.shape[1], w.shape[2]),
        compiler_params=_cparams(("arbitrary",)),
        name="out_proj",
    )(oa, ob, oc, w, g.reshape(1, d), x)


def _cross_attn_kernel(x_ref, gpre_ref, wq_hbm, k_ref, v_ref, wo_hbm, gpost_ref, o_ref,
                       wq_ref, wo_ref, stage_ref, sem_ref, *, layer):
    @pl.when(jnp.logical_and(pl.program_id(0) == 0, pl.program_id(1) == 0))
    def _():
        _fetch_weight(wq_hbm, layer, wq_ref, stage_ref, sem_ref)
        _fetch_weight(wo_hbm, layer, wo_ref, stage_ref, sem_ref)

    x = x_ref[...]
    hn = (x * _rms_scale(x) * gpre_ref[...]).astype(BF16)
    q = _dot(hn, wq_ref[...]).astype(BF16)
    acc = None
    for h in range(X_HEADS):
        hs = slice(h * X_HEAD_DIM, (h + 1) * X_HEAD_DIM)
        s = _dot_nt(q[:, hs], k_ref[:, hs]) * (X_HEAD_DIM ** -0.5)
        m = jnp.max(s, axis=-1, keepdims=True)
        p = jnp.exp(s - m)
        o = _dot(p.astype(BF16), v_ref[:, hs]) / jnp.sum(p, axis=-1, keepdims=True)
        part = _dot(o.astype(BF16), wo_ref[hs, :])
        acc = part if acc is None else acc + part
    o_ref[...] = x + acc * _rms_scale(acc) * gpost_ref[...]


def _cross_attention(x, gpre, wq, k, v, wo, layer, gpost, bsz, seq, tq):
    d = D_MODEL
    per_row = seq // tq
    mem_spec = pl.BlockSpec((MEM_LEN, d), lambda b, i: (b, 0))
    gain_spec = pl.BlockSpec((1, d), lambda b, i: (0, 0))
    weight_spec = pl.BlockSpec(memory_space=pl.ANY)
    w_bf, stage, sem = _weight_scratch(d, d)
    row_spec = pl.BlockSpec((tq, d), lambda b, i: (b * per_row + i, 0))
    return pl.pallas_call(
        functools.partial(_cross_attn_kernel, layer=layer),
        grid=(bsz, per_row),
        in_specs=[row_spec, gain_spec, weight_spec, mem_spec, mem_spec, weight_spec, gain_spec],
        out_specs=row_spec,
        out_shape=jax.ShapeDtypeStruct((bsz * seq, d), F32),
        scratch_shapes=[w_bf, w_bf, stage, sem],
        compiler_params=_cparams(("arbitrary", "arbitrary")),
        name="cross_attention",
    )(x, gpre.reshape(1, d), wq, k, v, wo, gpost.reshape(1, d))


def _ffn_kernel(x_ref, gpre_ref, wu_ref, wd_ref, gpost_ref, o_ref, hn_ref, acc_ref):
    f = pl.program_id(1)
    last = pl.num_programs(1) - 1

    def step(is_first, is_last):
        for r in _row_groups(x_ref.shape[0]):
            if is_first:
                x = x_ref[r, :]
                hn_ref[r, :] = (x * _rms_scale(x) * gpre_ref[...]).astype(BF16)
            up = _dot(hn_ref[r, :], wu_ref[...])
            act = jnp.square(jnp.maximum(up, 0.0)).astype(BF16)
            part = _dot(act, wd_ref[...])
            ff = part if is_first else acc_ref[r, :] + part
            if is_last:
                o_ref[r, :] = x_ref[r, :] + ff * _rms_scale(ff) * gpost_ref[...]
            else:
                acc_ref[r, :] = ff

    pl.when(f == 0)(functools.partial(step, True, False))
    pl.when(jnp.logical_and(f > 0, f < last))(functools.partial(step, False, False))
    pl.when(f == last)(functools.partial(step, False, True))


def _ffn_rest(x, gpre, w_up, w_down, gpost, tm, tf):
    m, d = x.shape
    dff = w_up.shape[1]
    return pl.pallas_call(
        _ffn_kernel,
        grid=(m // tm - 1, dff // tf),
        in_specs=[pl.BlockSpec((tm, d), lambda i, f: (i + 1, 0)),
                  pl.BlockSpec((1, d), lambda i, f: (0, 0)),
                  pl.BlockSpec((d, tf), lambda i, f: (0, f)),
                  pl.BlockSpec((tf, d), lambda i, f: (f, 0)),
                  pl.BlockSpec((1, d), lambda i, f: (0, 0))],
        out_specs=pl.BlockSpec((tm, d), lambda i, f: (i + 1, 0)),
        out_shape=jax.ShapeDtypeStruct((m, d), F32),
        input_output_aliases={0: 0},
        scratch_shapes=[pltpu.VMEM((tm, d), BF16), pltpu.VMEM((tm, d), F32)],
        compiler_params=_cparams(("parallel", "arbitrary")),
        name="ffn",
    )(x, gpre.reshape(1, d), w_up, w_down, gpost.reshape(1, d))


def _ffn_first_kernel(x_ref, gpre_ref, wu_ref, wd_ref, gpost_ref, o_ref, wub_ref, wdb_ref, hn_ref, acc_ref):
    f = pl.program_id(0)
    wu = wu_ref[...].astype(BF16)
    wd = wd_ref[...].astype(BF16)
    wub_ref[...] = wu
    wdb_ref[...] = wd

    @pl.when(f == 0)
    def _():
        x = x_ref[...]
        hn_ref[...] = (x * _rms_scale(x) * gpre_ref[...]).astype(BF16)
        acc_ref[...] = jnp.zeros(acc_ref.shape, F32)

    up = _dot(hn_ref[...], wu)
    act = jnp.square(jnp.maximum(up, 0.0)).astype(BF16)
    acc_ref[...] += _dot(act, wd)

    @pl.when(f == pl.num_programs(0) - 1)
    def _():
        ff = acc_ref[...]
        o_ref[...] = x_ref[...] + ff * _rms_scale(ff) * gpost_ref[...]


def _ffn_first(x, gpre, w_up, w_down, layer, gpost, tm, tf):
    m, d = x.shape
    dff = w_up.shape[2]
    return pl.pallas_call(
        _ffn_first_kernel,
        grid=(dff // tf,),
        in_specs=[pl.BlockSpec((tm, d), lambda f: (0, 0)),
                  pl.BlockSpec((1, d), lambda f: (0, 0)),
                  pl.BlockSpec((None, d, tf), lambda f: (layer, 0, f)),
                  pl.BlockSpec((None, tf, d), lambda f: (layer, f, 0)),
                  pl.BlockSpec((1, d), lambda f: (0, 0))],
        out_specs=[pl.BlockSpec((tm, d), lambda f: (0, 0)),
                   pl.BlockSpec((d, tf), lambda f: (0, f)),
                   pl.BlockSpec((tf, d), lambda f: (f, 0))],
        out_shape=[jax.ShapeDtypeStruct((tm, d), F32), jax.ShapeDtypeStruct((d, dff), BF16),
                   jax.ShapeDtypeStruct((dff, d), BF16)],
        scratch_shapes=[pltpu.VMEM((tm, d), BF16), pltpu.VMEM((tm, d), F32)],
        compiler_params=_cparams(("arbitrary",)),
        name="ffn_first",
    )(x, gpre.reshape(1, d), w_up, w_down, gpost.reshape(1, d))


def _ffn(x, gpre, w_up, w_down, layer, gpost, tm):
    first, w_up_b, w_down_b = _ffn_first(x, gpre, w_up, w_down, layer, gpost, tm, tf=512)
    rest = _ffn_rest(x, gpre, w_up_b, w_down_b, gpost, tm, tf=1024)
    return lax.dynamic_update_slice(rest, first, (0, 0))


def _gate_expansion(w2_f, w2_b):
    w = jnp.zeros((w2_f.shape[0], LANE, 2 * B_QK), F32)
    w = w.at[:, 0:GATE_RANK, 0:B_QK].set(w2_f)
    w = w.at[:, GATE_RANK:2 * GATE_RANK, B_QK:].set(w2_b)
    return w.astype(BF16)


def kernel(x, mem, rel_bias, w_in, w_out, attn_sink, gla_w2_f, gla_b2_f, gla_w2_b, gla_b2_b, gla_norm, conv_w, conv_b, lru_wa, lru_ba, lru_wx, lru_bx, lru_lambda, xq, xk, xv, xo, w_up, w_down, norm_mix_pre, norm_mix_post, norm_mem, norm_x_pre, norm_x_post, norm_ff_pre, norm_ff_post):
    bsz, seq, d = x.shape
    assert d == D_MODEL and seq == SUBLANES * LRU_SEG
    xs = x.reshape(bsz * seq, d)
    mems = mem.reshape(bsz * mem.shape[1], d)
    bias_tab = _attention_bias_table(rel_bias)
    masks = _gla_masks()
    w2_b = _gate_expansion(gla_w2_f, gla_w2_b)
    b2 = jnp.concatenate([gla_b2_f, gla_b2_b], axis=-1)
    bf = lambda w: w.astype(BF16)
    wa_b, wx_b = bf(lru_wa), bf(lru_wx)
    for l in range(DEPTH):
        proj = _in_proj(xs, norm_mix_pre[l], w_in, l, tm=512)
        oa = _windowed_attention(proj, bias_tab, attn_sink[l], bsz, seq)
        ob = _gla(proj, w2_b, l, b2[l].reshape(1, 2 * B_QK), gla_norm[l].reshape(1, B_V), masks, bsz, seq)
        oc = _rglru(proj, conv_w[l], conv_b[l], wa_b, wx_b, l, lru_ba[l], lru_bx[l], lru_lambda[l],
                    bsz, seq)
        xs = _out_proj(oa, ob, oc, w_out, l, norm_mix_post[l], xs, tm=512)

        k, v = _kv_proj(mems, norm_mem[l], xk, xv, l, tm=512)
        xs = _cross_attention(xs, norm_x_pre[l], xq, k, v, xo, l, norm_x_post[l], bsz, seq, tq=512)

        xs = _ffn(xs, norm_ff_pre[l], w_up, w_down, l, norm_ff_post[l], tm=512)
    return xs.reshape(bsz, seq, d)
```
